```python
import math
import jax, jax.numpy as jnp
from jax import lax
import numpy as np

D_MODEL = 1024
BATCH = 8
SEQ = 4096
DEPTH = 4

N_MIXERS = 3
N_A = len(range(0, DEPTH, N_MIXERS))
N_B = len(range(1, DEPTH, N_MIXERS))
N_C = len(range(2, DEPTH, N_MIXERS))
RMS_EPS = 1e-6
D_FF = 2816
HEAD_DIM = 64
ROPE_DIM = HEAD_DIM // 4
ROPE_THETA = 500000.0
ATTN_SCALE = HEAD_DIM ** -0.5
MASK_NEG = -1e30

SSM_D_INNER = 2 * D_MODEL
SSM_HEAD_DIM = 64
SSM_HEADS = SSM_D_INNER // SSM_HEAD_DIM
SSM_GROUPS = 8
SSM_STATE = 128
SSM_CONV = 4
SSM_CHUNK = 128
SSM_CONV_DIM = SSM_D_INNER + 2 * SSM_GROUPS * SSM_STATE
SSM_IN_DIM = SSM_D_INNER + SSM_CONV_DIM + SSM_HEADS

SWA_HEADS = D_MODEL // HEAD_DIM
SWA_KV_HEADS = 4
SWA_WINDOW = 128
SWA_BLOCK = 128
SWA_QKV_DIM = (SWA_HEADS + 2 * SWA_KV_HEADS) * HEAD_DIM

NSA_HEADS = D_MODEL // HEAD_DIM
NSA_KV_HEADS = 4
NSA_CMP_LEN = 32
NSA_CMP_STRIDE = 16
NSA_CMP_HIDDEN = 256
NSA_SEL_BLOCK = 64
NSA_TOP_N = 8
NSA_WINDOW = 512
NSA_Q_BLOCK = 64
NSA_IN_DIM = NSA_HEADS * HEAD_DIM + 6 * NSA_KV_HEADS * HEAD_DIM + 3 * NSA_HEADS

kernel_name = 'hybrid_ssd_swa_nsa_macaron'


def rms_norm(x, w):
    xf = x.astype(jnp.float32)
    y = xf * lax.rsqrt(jnp.mean(xf * xf, axis=-1, keepdims=True) + RMS_EPS)
    return (y * w.astype(jnp.float32)).astype(x.dtype)


def swiglu(x, w_in, w_out):
    gate, up = jnp.split(x @ w_in, 2, axis=-1)
    return (jax.nn.silu(gate) * up) @ w_out


def rope_tables(positions):
    inv_freq = ROPE_THETA ** (-jnp.arange(0, ROPE_DIM, 2, dtype=jnp.float32) / ROPE_DIM)
    ang = positions.astype(jnp.float32)[..., None] * inv_freq
    return jnp.cos(ang), jnp.sin(ang)


def apply_rope(x, cos, sin):
    half = ROPE_DIM // 2
    c = cos[:, :, None, :].astype(x.dtype)
    s = sin[:, :, None, :].astype(x.dtype)
    x1 = x[..., :half]
    x2 = x[..., half:ROPE_DIM]
    return jnp.concatenate([x1 * c - x2 * s, x2 * c + x1 * s, x[..., ROPE_DIM:]], axis=-1)


def masked_softmax(logits, mask):
    p = jax.nn.softmax(jnp.where(mask, logits, MASK_NEG), axis=-1)
    return jnp.where(mask, p, 0.0)


def ssd_scan(x, dt, a, bm, cm):
    b, s, h, p = x.shape
    g, n = bm.shape[2], bm.shape[3]
    r = h // g
    q = SSM_CHUNK
    c = s // q
    xd = (x.astype(jnp.float32) * dt[..., None]).reshape(b, c, q, g, r, p)
    ad = (dt * a).reshape(b, c, q, g, r).transpose(0, 3, 4, 1, 2)
    bc = bm.astype(jnp.float32).reshape(b, c, q, g, n)
    cc = cm.astype(jnp.float32).reshape(b, c, q, g, n)
    a_cum = jnp.cumsum(ad, axis=-1)
    causal = jnp.tril(jnp.ones((q, q), dtype=bool))
    diff = a_cum[..., :, None] - a_cum[..., None, :]
    decay = jnp.exp(jnp.where(causal, diff, -jnp.inf))
    cb = jnp.einsum('bclgn,bcsgn->bgcls', cc, bc)
    y_diag = jnp.einsum('bgcls,bgrcls,bcsgrp->bclgrp', cb, decay, xd)
    decay_states = jnp.exp(a_cum[..., -1:] - a_cum)
    states = jnp.einsum('bcsgn,bgrcs,bcsgrp->bcgrpn', bc, decay_states, xd)
    chunk_decay = jnp.exp(a_cum[..., -1])

    def step(carry, inp):
        dec, st = inp
        return carry * dec[..., None, None] + st, carry

    init = jnp.zeros((b, g, r, p, n), jnp.float32)
    _, prev = lax.scan(step, init, (jnp.moveaxis(chunk_decay, -1, 0), jnp.moveaxis(states, 1, 0)))
    prev = jnp.moveaxis(prev, 0, 1)
    y_off = jnp.einsum('bclgn,bcgrpn,bgrcl->bclgrp', cc, prev, jnp.exp(a_cum))
    return (y_diag + y_off).reshape(b, s, h, p)


def mamba2_mixer(u, w_in, conv_w, conv_b, dt_bias, a_log, d_skip, norm_w, w_out):
    b, s, _ = u.shape
    zxbcdt = u @ w_in
    z, xbc, dt = jnp.split(zxbcdt, [SSM_D_INNER, SSM_D_INNER + SSM_CONV_DIM], axis=-1)
    xbc = lax.conv_general_dilated(xbc, conv_w[:, None, :], window_strides=(1,),
                                   padding=[(SSM_CONV - 1, 0)],
                                   dimension_numbers=('NWC', 'WIO', 'NWC'),
                                   feature_group_count=SSM_CONV_DIM) + conv_b
    xbc = jax.nn.silu(xbc)
    xs, bm, cm = jnp.split(xbc, [SSM_D_INNER, SSM_D_INNER + SSM_GROUPS * SSM_STATE], axis=-1)
    xs = xs.reshape(b, s, SSM_HEADS, SSM_HEAD_DIM)
    bm = bm.reshape(b, s, SSM_GROUPS, SSM_STATE)
    cm = cm.reshape(b, s, SSM_GROUPS, SSM_STATE)
    dt = jax.nn.softplus(dt.astype(jnp.float32) + dt_bias.astype(jnp.float32))
    a = -jnp.exp(a_log.astype(jnp.float32))
    y = ssd_scan(xs, dt, a, bm, cm)
    y = y + d_skip.astype(jnp.float32)[:, None] * xs.astype(jnp.float32)
    y = y.reshape(b, s, SSM_D_INNER)
    gated = y * jax.nn.silu(z.astype(jnp.float32))
    gg = gated.reshape(b, s, SSM_GROUPS, SSM_D_INNER // SSM_GROUPS)
    gg = gg * lax.rsqrt(jnp.mean(gg * gg, axis=-1, keepdims=True) + RMS_EPS)
    out = (gg.reshape(b, s, SSM_D_INNER) * norm_w.astype(jnp.float32)).astype(u.dtype)
    return out @ w_out


def swa_sink_attention(u, w_qkv, b_qkv, sinks, w_o, b_o, cos, sin):
    b, s, _ = u.shape
    H, KV = SWA_HEADS, SWA_KV_HEADS
    R = H // KV
    qkv = u @ w_qkv + b_qkv
    q, k, v = jnp.split(qkv, [H * HEAD_DIM, (H + KV) * HEAD_DIM], axis=-1)
    q = apply_rope(q.reshape(b, s, H, HEAD_DIM), cos, sin).reshape(b, s, KV, R, HEAD_DIM)
    k = apply_rope(k.reshape(b, s, KV, HEAD_DIM), cos, sin)
    v = v.reshape(b, s, KV, HEAD_DIM)
    kp = jnp.pad(k, ((0, 0), (SWA_WINDOW, 0), (0, 0), (0, 0)))
    vp = jnp.pad(v, ((0, 0), (SWA_WINDOW, 0), (0, 0), (0, 0)))
    span = SWA_BLOCK + SWA_WINDOW
    sink = sinks.astype(jnp.float32).reshape(1, KV, R, 1, 1)

    def block(i):
        t0 = i * SWA_BLOCK
        qb = lax.dynamic_slice_in_dim(q, t0, SWA_BLOCK, axis=1)
        kb = lax.dynamic_slice_in_dim(kp, t0, span, axis=1)
        vb = lax.dynamic_slice_in_dim(vp, t0, span, axis=1)
        qpos = t0 + jnp.arange(SWA_BLOCK)
        kpos = t0 - SWA_WINDOW + jnp.arange(span)
        delta = qpos[:, None] - kpos[None, :]
        mask = (delta >= 0) & (delta < SWA_WINDOW) & (kpos[None, :] >= 0)
        logits = jnp.einsum('bqhrd,bkhd->bhrqk', qb, kb).astype(jnp.float32) * ATTN_SCALE
        logits = jnp.where(mask, logits, -jnp.inf)
        sink_col = jnp.broadcast_to(sink, logits.shape[:-1] + (1,))
        probs = jax.nn.softmax(jnp.concatenate([logits, sink_col], axis=-1), axis=-1)[..., :-1]
        return jnp.einsum('bhrqk,bkhd->bqhrd', probs.astype(vb.dtype), vb)

    outs = lax.map(block, jnp.arange(s // SWA_BLOCK))
    o = jnp.moveaxis(outs, 0, 1).reshape(b, s, H * HEAD_DIM)
    return o @ w_o + b_o


def compress_blocks(blk, pe, w1, w2):
    bb, nn, ll, kv, dd = blk.shape
    z = (blk + pe[:, None, :]).transpose(0, 1, 3, 2, 4).reshape(bb, nn, kv, ll * dd)
    return jax.nn.silu(z @ w1) @ w2


def nsa_attention(u, w_in, pe_k, k_w1, k_w2, pe_v, v_w1, v_w2, w_o, cos, sin):
    b, s, _ = u.shape
    H, KV = NSA_HEADS, NSA_KV_HEADS
    R = H // KV
    qd, kvd = H * HEAD_DIM, KV * HEAD_DIM
    splits = [int(t) for t in np.cumsum([qd] + [kvd] * 6)]
    q, kc, vc, ks, vs, kw, vw, g = jnp.split(u @ w_in, splits, axis=-1)
    q = apply_rope(q.reshape(b, s, H, HEAD_DIM), cos, sin).reshape(b, s, KV, R, HEAD_DIM)
    kc = apply_rope(kc.reshape(b, s, KV, HEAD_DIM), cos, sin)
    ks = apply_rope(ks.reshape(b, s, KV, HEAD_DIM), cos, sin)
    kw = apply_rope(kw.reshape(b, s, KV, HEAD_DIM), cos, sin)
    vc = vc.reshape(b, s, KV, HEAD_DIM)
    vs = vs.reshape(b, s, KV, HEAD_DIM)
    vw = vw.reshape(b, s, KV, HEAD_DIM)
    gates = jax.nn.sigmoid(g.astype(jnp.float32)).astype(u.dtype).reshape(b, s, KV, R, 3)

    n_cmp = (s - NSA_CMP_LEN) // NSA_CMP_STRIDE + 1
    idx = np.arange(n_cmp)[:, None] * NSA_CMP_STRIDE + np.arange(NSA_CMP_LEN)[None, :]
    k_cmp = compress_blocks(kc[:, idx], pe_k, k_w1, k_w2)
    v_cmp = compress_blocks(vc[:, idx], pe_v, v_w1, v_w2)
    cmp_end = jnp.asarray(idx[:, -1].astype(np.int32))

    n_sel = s // NSA_SEL_BLOCK
    top_n = min(NSA_TOP_N, n_sel)
    sel_lo = np.arange(n_sel)[:, None] * NSA_SEL_BLOCK
    cmp_lo = np.arange(n_cmp)[None, :] * NSA_CMP_STRIDE
    ov = np.clip(np.minimum(sel_lo + NSA_SEL_BLOCK, cmp_lo + NSA_CMP_LEN) - np.maximum(sel_lo, cmp_lo), 0, None)
    overlap = jnp.asarray((ov / NSA_CMP_LEN).astype(np.float32))
    ks_blk = ks.reshape(b, n_sel, NSA_SEL_BLOCK, KV, HEAD_DIM).transpose(0, 3, 1, 2, 4)
    vs_blk = vs.reshape(b, n_sel, NSA_SEL_BLOCK, KV, HEAD_DIM).transpose(0, 3, 1, 2, 4)
    sel_ids = jnp.arange(n_sel)
    bi = jnp.arange(b)[:, None, None, None]
    hi = jnp.arange(KV)[None, :, None, None]

    kwp = jnp.pad(kw, ((0, 0), (NSA_WINDOW, 0), (0, 0), (0, 0)))
    vwp = jnp.pad(vw, ((0, 0), (NSA_WINDOW, 0), (0, 0), (0, 0)))
    span = NSA_Q_BLOCK + NSA_WINDOW

    def block(i):
        t0 = i * NSA_Q_BLOCK
        qb = lax.dynamic_slice_in_dim(q, t0, NSA_Q_BLOCK, axis=1)
        qpos = t0 + jnp.arange(NSA_Q_BLOCK)
        mc = cmp_end[None, :] <= qpos[:, None]
        lc = jnp.einsum('bqhrd,bchd->bhrqc', qb, k_cmp).astype(jnp.float32) * ATTN_SCALE
        pc = masked_softmax(lc, mc)
        o_cmp = jnp.einsum('bhrqc,bchd->bqhrd', pc.astype(v_cmp.dtype), v_cmp)
        imp = jnp.einsum('bhrqc,jc->bhqj', pc, overlap)
        cur = qpos // NSA_SEL_BLOCK
        forced = (sel_ids[None, :] == cur[:, None]) | (sel_ids[None, :] == 0)
        imp = jnp.where(forced, jnp.inf, imp)
        imp = jnp.where(sel_ids[None, :] * NSA_SEL_BLOCK <= qpos[:, None], imp, -jnp.inf)
        _, top = lax.top_k(imp, top_n)
        kg = ks_blk[bi, hi, top]
        vg = vs_blk[bi, hi, top]
        kpos = top[..., None] * NSA_SEL_BLOCK + jnp.arange(NSA_SEL_BLOCK)
        ms = (kpos <= qpos[None, None, :, None, None])[:, :, None]
        ls = jnp.einsum('bqhrd,bhqnkd->bhrqnk', qb, kg).astype(jnp.float32) * ATTN_SCALE
        ls = jnp.where(ms, ls, -jnp.inf)
        shp = ls.shape
        ps = jax.nn.softmax(ls.reshape(shp[:4] + (shp[4] * shp[5],)), axis=-1).reshape(shp)
        o_sel = jnp.einsum('bhrqnk,bhqnkd->bqhrd', ps.astype(vg.dtype), vg)
        kb = lax.dynamic_slice_in_dim(kwp, t0, span, axis=1)
        vb = lax.dynamic_slice_in_dim(vwp, t0, span, axis=1)
        wpos = t0 - NSA_WINDOW + jnp.arange(span)
        delta = qpos[:, None] - wpos[None, :]
        mw = (delta >= 0) & (delta < NSA_WINDOW) & (wpos[None, :] >= 0)
        lw = jnp.einsum('bqhrd,bkhd->bhrqk', qb, kb).astype(jnp.float32) * ATTN_SCALE
        pw = masked_softmax(lw, mw)
        o_win = jnp.einsum('bhrqk,bkhd->bqhrd', pw.astype(vb.dtype), vb)
        gb = lax.dynamic_slice_in_dim(gates, t0, NSA_Q_BLOCK, axis=1)
        return gb[..., 0:1] * o_cmp + gb[..., 1:2] * o_sel + gb[..., 2:3] * o_win

    outs = lax.map(block, jnp.arange(s // NSA_Q_BLOCK))
    o = jnp.moveaxis(outs, 0, 1).reshape(b, s, H * HEAD_DIM)
    return o @ w_o


def setup_inputs(seed: int = 0) -> dict:
    key = jax.random.key(seed)
    keys = iter(jax.random.split(key, 48))
    f32 = jnp.float32

    def nrm(shape, scale):
        return jax.random.normal(next(keys), shape, f32) * scale

    x = nrm((BATCH, SEQ, D_MODEL), 1.0)
    positions = jnp.broadcast_to(jnp.arange(SEQ, dtype=jnp.int32), (BATCH, SEQ))
    ln_ffn1 = 1.0 + nrm((DEPTH, D_MODEL), 0.01)
    ffn1_w_in = nrm((DEPTH, D_MODEL, 2 * D_FF), D_MODEL ** -0.5)
    ffn1_w_out = nrm((DEPTH, D_FF, D_MODEL), D_FF ** -0.5)
    ln_mix = 1.0 + nrm((DEPTH, D_MODEL), 0.01)
    ln_ffn2 = 1.0 + nrm((DEPTH, D_MODEL), 0.01)
    ffn2_w_in = nrm((DEPTH, D_MODEL, 2 * D_FF), D_MODEL ** -0.5)
    ffn2_w_out = nrm((DEPTH, D_FF, D_MODEL), D_FF ** -0.5)
    ssm_w_in = nrm((N_A, D_MODEL, SSM_IN_DIM), D_MODEL ** -0.5)
    ssm_conv_w = nrm((N_A, SSM_CONV, SSM_CONV_DIM), SSM_CONV ** -0.5)
    ssm_conv_b = nrm((N_A, SSM_CONV_DIM), 0.01)
    dt0 = jnp.exp(jax.random.uniform(next(keys), (N_A, SSM_HEADS), f32, math.log(1e-3), math.log(1e-1)))
    ssm_dt_bias = dt0 + jnp.log(-jnp.expm1(-dt0))
    ssm_a_log = jnp.log(jax.random.uniform(next(keys), (N_A, SSM_HEADS), f32, 1.0, 16.0))
    ssm_d = 1.0 + nrm((N_A, SSM_HEADS), 0.1)
    ssm_norm_w = 1.0 + nrm((N_A, SSM_D_INNER), 0.01)
    ssm_w_out = nrm((N_A, SSM_D_INNER, D_MODEL), SSM_D_INNER ** -0.5)
    swa_w_qkv = nrm((N_B, D_MODEL, SWA_QKV_DIM), D_MODEL ** -0.5)
    swa_b_qkv = nrm((N_B, SWA_QKV_DIM), 0.01)
    swa_sinks = nrm((N_B, SWA_HEADS), 0.5)
    swa_w_o = nrm((N_B, SWA_HEADS * HEAD_DIM, D_MODEL), (SWA_HEADS * HEAD_DIM) ** -0.5)
    swa_b_o = nrm((N_B, D_MODEL), 0.01)
    nsa_w_in = nrm((N_C, D_MODEL, NSA_IN_DIM), D_MODEL ** -0.5)
    nsa_pe_k = nrm((N_C, NSA_CMP_LEN, HEAD_DIM), 0.02)
    nsa_k_w1 = nrm((N_C, NSA_CMP_LEN * HEAD_DIM, NSA_CMP_HIDDEN), (NSA_CMP_LEN * HEAD_DIM) ** -0.5)
    nsa_k_w2 = nrm((N_C, NSA_CMP_HIDDEN, HEAD_DIM), NSA_CMP_HIDDEN ** -0.5)
    nsa_pe_v = nrm((N_C, NSA_CMP_LEN, HEAD_DIM), 0.02)
    nsa_v_w1 = nrm((N_C, NSA_CMP_LEN * HEAD_DIM, NSA_CMP_HIDDEN), (NSA_CMP_LEN * HEAD_DIM) ** -0.5)
    nsa_v_w2 = nrm((N_C, NSA_CMP_HIDDEN, HEAD_DIM), NSA_CMP_HIDDEN ** -0.5)
    nsa_w_o = nrm((N_C, NSA_HEADS * HEAD_DIM, D_MODEL), (NSA_HEADS * HEAD_DIM) ** -0.5)
    final_norm = 1.0 + nrm((D_MODEL,), 0.01)
    return {'x': x, 'positions': positions,
            'ln_ffn1': ln_ffn1, 'ffn1_w_in': ffn1_w_in, 'ffn1_w_out': ffn1_w_out,
            'ln_mix': ln_mix, 'ln_ffn2': ln_ffn2, 'ffn2_w_in': ffn2_w_in, 'ffn2_w_out': ffn2_w_out,
            'ssm_w_in': ssm_w_in, 'ssm_conv_w': ssm_conv_w, 'ssm_conv_b': ssm_conv_b,
            'ssm_dt_bias': ssm_dt_bias, 'ssm_a_log': ssm_a_log, 'ssm_d': ssm_d,
            'ssm_norm_w': ssm_norm_w, 'ssm_w_out': ssm_w_out,
            'swa_w_qkv': swa_w_qkv, 'swa_b_qkv': swa_b_qkv, 'swa_sinks': swa_sinks,
            'swa_w_o': swa_w_o, 'swa_b_o': swa_b_o,
            'nsa_w_in': nsa_w_in, 'nsa_pe_k': nsa_pe_k, 'nsa_k_w1': nsa_k_w1, 'nsa_k_w2': nsa_k_w2,
            'nsa_pe_v': nsa_pe_v, 'nsa_v_w1': nsa_v_w1, 'nsa_v_w2': nsa_v_w2, 'nsa_w_o': nsa_w_o,
            'final_norm': final_norm}


def reference(x, positions, ln_ffn1, ffn1_w_in, ffn1_w_out, ln_mix, ln_ffn2, ffn2_w_in, ffn2_w_out,
              ssm_w_in, ssm_conv_w, ssm_conv_b, ssm_dt_bias, ssm_a_log, ssm_d, ssm_norm_w, ssm_w_out,
              swa_w_qkv, swa_b_qkv, swa_sinks, swa_w_o, swa_b_o,
              nsa_w_in, nsa_pe_k, nsa_k_w1, nsa_k_w2, nsa_pe_v, nsa_v_w1, nsa_v_w2, nsa_w_o,
              final_norm):
    cos, sin = rope_tables(positions)
    h = x
    for i in range(DEPTH):
        kind = i % N_MIXERS
        inst = i // N_MIXERS
        h = h + 0.5 * swiglu(rms_norm(h, ln_ffn1[i]), ffn1_w_in[i], ffn1_w_out[i])
        u = rms_norm(h, ln_mix[i])
        if kind == 0:
            m = mamba2_mixer(u, ssm_w_in[inst], ssm_conv_w[inst], ssm_conv_b[inst], ssm_dt_bias[inst],
                             ssm_a_log[inst], ssm_d[inst], ssm_norm_w[inst], ssm_w_out[inst])
        elif kind == 1:
            m = swa_sink_attention(u, swa_w_qkv[inst], swa_b_qkv[inst], swa_sinks[inst],
                                   swa_w_o[inst], swa_b_o[inst], cos, sin)
        else:
            m = nsa_attention(u, nsa_w_in[inst], nsa_pe_k[inst], nsa_k_w1[inst], nsa_k_w2[inst],
                              nsa_pe_v[inst], nsa_v_w1[inst], nsa_v_w2[inst], nsa_w_o[inst], cos, sin)
        h = h + m
        h = h + 0.5 * swiglu(rms_norm(h, ln_ffn2[i]), ffn2_w_in[i], ffn2_w_out[i])
    return rms_norm(h, final_norm)
```

```python
import functools
import math

import numpy as np
import jax
import jax.numpy as jnp
from jax import lax
from jax.experimental import pallas as pl
from jax.experimental.pallas import tpu as pltpu

F32 = jnp.float32
BF16 = jnp.bfloat16

D_MODEL = 1024
RMS_EPS = 1e-6
D_FF = 2816
HEAD_DIM = 64
ROPE_DIM = HEAD_DIM // 4
ROPE_THETA = 500000.0
ATTN_SCALE = HEAD_DIM ** -0.5
MASK_NEG = -1e30

SSM_D_INNER = 2 * D_MODEL
SSM_HEADS = 32
SSM_GROUPS = 8
SSM_STATE = 128
SSM_CONV = 4
SSM_CHUNK = 128
SSM_CONV_DIM = SSM_D_INNER + 2 * SSM_GROUPS * SSM_STATE
SSM_IN_DIM = SSM_D_INNER + SSM_CONV_DIM + SSM_HEADS
SSM_IN_PAD = SSM_D_INNER + SSM_CONV_DIM + 256

N_HEADS = 16
KV_HEADS = 4
GQA_R = N_HEADS // KV_HEADS
KV_DIM = KV_HEADS * HEAD_DIM

SWA_WINDOW = 128
SWA_BLOCK = 128

NSA_CMP_LEN = 32
NSA_CMP_STRIDE = 16
NSA_CMP_HIDDEN = 256
NSA_SEL_BLOCK = 64
NSA_TOP_N = 8
NSA_WINDOW = 512
NSA_Q_BLOCK = 64
NSA_SEL_CHUNK = 512

LANES = 128
VMEM_LIMIT = 56 * 1024 * 1024
TOKEN_TILE = 1024
FF_TILE = 256
PROJ_TILE = 256


def _cparams(*sem):
    return pltpu.CompilerParams(dimension_semantics=sem, vmem_limit_bytes=VMEM_LIMIT)


def _dot(a, b):
    return jnp.dot(a, b, preferred_element_type=F32)


def _dot_nt(a, b):
    return lax.dot_general(a, b, (((1,), (1,)), ((), ())), preferred_element_type=F32)


def _sigmoid(x):
    return 1.0 / (1.0 + jnp.exp(-x))


def _rms(x, w):
    return x * lax.rsqrt(jnp.mean(x * x, axis=-1, keepdims=True) + RMS_EPS) * w


def _norm_proj_kernel(*refs, rope_tiles, tn):
    if rope_tiles:
        x_ref, lnw_ref, w_ref, b_ref, c_ref, sa_ref, sb_ref, o_ref, xn_ref = refs
    else:
        x_ref, lnw_ref, w_ref, b_ref, o_ref, xn_ref = refs
    j = pl.program_id(1)

    @pl.when(j == 0)
    def _():
        xn_ref[...] = _rms(x_ref[...], lnw_ref[...]).astype(BF16)

    acc = _dot(xn_ref[...], w_ref[...]) + b_ref[...]
    if not rope_tiles:
        o_ref[...] = acc.astype(o_ref.dtype)
        return
    is_rope = functools.reduce(jnp.logical_or, [j == t for t in rope_tiles])

    @pl.when(is_rope)
    def _():
        c = c_ref[...]
        sa = sa_ref[...]
        sb = sb_ref[...]
        for k in range(tn // LANES):
            seg = acc[:, LANES * k:LANES * (k + 1)]
            rot = (seg * c + pltpu.roll(seg, LANES - ROPE_DIM // 2, 1) * sa
                   + pltpu.roll(seg, ROPE_DIM // 2, 1) * sb)
            o_ref[:, LANES * k:LANES * (k + 1)] = rot.astype(o_ref.dtype)

    @pl.when(jnp.logical_not(is_rope))
    def _():
        o_ref[...] = acc.astype(o_ref.dtype)


def norm_proj(h, ln_w, w, bias, out_dtype, rope=None, rope_tiles=()):
    t, d = h.shape
    n = w.shape[1]
    tm = min(TOKEN_TILE, t)
    tn = PROJ_TILE
    assert n % tn == 0 and t % tm == 0
    in_specs = [pl.BlockSpec((tm, d), lambda i, j: (i, 0)),
                pl.BlockSpec((1, d), lambda i, j: (0, 0)),
                pl.BlockSpec((d, tn), lambda i, j: (0, j)),
                pl.BlockSpec((1, tn), lambda i, j: (0, j))]
    args = [h, ln_w.reshape(1, d), w, bias.reshape(1, n)]
    if rope_tiles:
        in_specs += [pl.BlockSpec((tm, LANES), lambda i, j: (i, 0))] * 3
        args += list(rope)
    return pl.pallas_call(
        functools.partial(_norm_proj_kernel, rope_tiles=tuple(rope_tiles), tn=tn),
        grid=(t // tm, n // tn),
        in_specs=in_specs,
        out_specs=pl.BlockSpec((tm, tn), lambda i, j: (i, j)),
        out_shape=jax.ShapeDtypeStruct((t, n), out_dtype),
        scratch_shapes=[pltpu.VMEM((tm, d), BF16)],
        compiler_params=_cparams("parallel", "arbitrary"),
        name="norm_proj",
    )(*args)


def _out_proj_kernel(y_ref, w_ref, b_ref, r_ref, o_ref):
    o_ref[...] = r_ref[...] + _dot(y_ref[...], w_ref[...]) + b_ref[...]


def out_proj(y, w, bias, res):
    t, k = y.shape
    d = w.shape[1]
    tm = min(TOKEN_TILE, t)
    return pl.pallas_call(
        _out_proj_kernel,
        grid=(t // tm,),
        in_specs=[pl.BlockSpec((tm, k), lambda i: (i, 0)),
                  pl.BlockSpec((k, d), lambda i: (0, 0)),
                  pl.BlockSpec((1, d), lambda i: (0, 0)),
                  pl.BlockSpec((tm, d), lambda i: (i, 0))],
        out_specs=pl.BlockSpec((tm, d), lambda i: (i, 0)),
        out_shape=jax.ShapeDtypeStruct((t, d), F32),
        compiler_params=_cparams("parallel"),
        name="out_proj",
    )(y, w, bias.reshape(1, d), res)


def _ffn_kernel(x_ref, lnw_ref, wg_ref, wu_ref, wo_ref, fnw_ref, o_ref, xn_ref, acc_ref, *, final_norm):
    f = pl.program_id(1)

    @pl.when(f == 0)
    def _():
        xn_ref[...] = _rms(x_ref[...], lnw_ref[...]).astype(BF16)

    xn = xn_ref[...]
    g = _dot(xn, wg_ref[...])
    u = _dot(xn, wu_ref[...])
    part = _dot((g * _sigmoid(g) * u).astype(BF16), wo_ref[...])

    @pl.when(f == 0)
    def _():
        acc_ref[...] = part

    @pl.when(f > 0)
    def _():
        acc_ref[...] += part

    @pl.when(f == pl.num_programs(1) - 1)
    def _():
        y = x_ref[...] + 0.5 * acc_ref[...]
        if final_norm:
            y = _rms(y, fnw_ref[...])
        o_ref[...] = y


def ffn(h, ln_w, w_in, w_out, final_w=None):
    t, d = h.shape
    tm = min(TOKEN_TILE, t)
    tf = FF_TILE
    nf = D_FF // tf
    final_norm = final_w is not None
    fnw = (final_w if final_norm else ln_w).reshape(1, d)
    return pl.pallas_call(
        functools.partial(_ffn_kernel, final_norm=final_norm),
        grid=(t // tm, nf),
        in_specs=[pl.BlockSpec((tm, d), lambda i, f: (i, 0)),
                  pl.BlockSpec((1, d), lambda i, f: (0, 0)),
                  pl.BlockSpec((d, tf), lambda i, f: (0, f)),
                  pl.BlockSpec((d, tf), lambda i, f: (0, nf + f)),
                  pl.BlockSpec((tf, d), lambda i, f: (f, 0)),
                  pl.BlockSpec((1, d), lambda i, f: (0, 0))],
        out_specs=pl.BlockSpec((tm, d), lambda i, f: (i, 0)),
        out_shape=jax.ShapeDtypeStruct((t, d), F32),
        scratch_shapes=[pltpu.VMEM((tm, d), BF16), pltpu.VMEM((tm, d), F32)],
        compiler_params=_cparams("parallel", "arbitrary"),
        name="ffn",
    )(h, ln_w.reshape(1, d), w_in, w_in, w_out, fnw)


def _expand_heads(mat, g):
    rows = mat.shape[0]
    lane = lax.broadcasted_iota(jnp.int32, (rows, LANES), 1)
    pieces = []
    for p in range(2):
        h0 = GQA_R * g + 2 * p
        a = jnp.broadcast_to(mat[:, h0:h0 + 1], (rows, LANES))
        b = jnp.broadcast_to(mat[:, h0 + 1:h0 + 2], (rows, LANES))
        pieces.append(jnp.where(lane < HEAD_DIM, a, b))
    return jnp.concatenate(pieces, axis=1)


def _ssd_kernel(z_ref, x_ref, bc_ref, dt_ref, cw_ref, cb_ref, dtb_ref, alog_ref, dsk_ref, nw_ref,
                tril_ref, o_ref, state_ref, xbuf_ref):
    q = SSM_CHUNK
    c = pl.program_id(1)

    @pl.when(c == 0)
    def _():
        state_ref[...] = jnp.zeros(state_ref.shape, F32)
        xbuf_ref[0:8, :] = jnp.zeros((8, SSM_CONV_DIM), F32)

    xbuf_ref[8:8 + q, 0:SSM_D_INNER] = x_ref[0]
    xbuf_ref[8:8 + q, SSM_D_INNER:SSM_CONV_DIM] = bc_ref[0]

    def conv_silu(c0, width):
        acc = cb_ref[:, c0:c0 + width]
        for k in range(SSM_CONV):
            r0 = 8 - (SSM_CONV - 1) + k
            acc = acc + cw_ref[k:k + 1, c0:c0 + width] * xbuf_ref[r0:r0 + q, c0:c0 + width]
        return acc * _sigmoid(acc)

    dtr = dt_ref[0] + dtb_ref[...]
    dt = jnp.maximum(dtr, 0.0) + jnp.log1p(jnp.exp(-jnp.abs(dtr)))
    ad = dt * (-jnp.exp(alog_ref[...]))
    acum = jnp.dot(tril_ref[...], ad, precision=lax.Precision.HIGHEST,
                   preferred_element_type=F32)
    acum_row = acum.T
    a_last = acum[q - 1:q, :]
    exp_acum = jnp.exp(acum)
    decay_st = jnp.exp(a_last - acum)
    chunk_decay = jnp.exp(a_last)
    causal = (lax.broadcasted_iota(jnp.int32, (q, q), 0) >= lax.broadcasted_iota(jnp.int32, (q, q), 1))
    head_of_lane = lax.broadcasted_iota(jnp.int32, (q, 2 * LANES), 1) // HEAD_DIM

    for g in range(SSM_GROUPS):
        gw = GQA_R * HEAD_DIM
        cs = slice(gw * g, gw * (g + 1))
        xs = conv_silu(gw * g, gw)
        bm = conv_silu(SSM_D_INNER + SSM_STATE * g, SSM_STATE)
        cm = conv_silu(SSM_D_INNER + SSM_GROUPS * SSM_STATE + SSM_STATE * g, SSM_STATE)
        xd = xs * _expand_heads(dt, g)
        cbf = cm.astype(BF16)
        cb = _dot_nt(cbf, bm.astype(BF16))
        st = state_ref[g]
        y = _dot(cbf, st.astype(BF16)) * _expand_heads(exp_acum, g)
        for r in range(GQA_R):
            hh = GQA_R * g + r
            diff = acum[:, hh:hh + 1] - acum_row[hh:hh + 1, :]
            decay = jnp.exp(jnp.where(causal, diff, -jnp.inf))
            xdm = jnp.where(head_of_lane == r, xd, 0.0).astype(BF16)
            y = y + _dot((cb * decay).astype(BF16), xdm)
        state_ref[g] = (st * _expand_heads(chunk_decay, g)
                        + _dot(bm.T.astype(BF16), (xd * _expand_heads(decay_st, g)).astype(BF16)))
        y = y + dsk_ref[:, cs] * xs
        zz = z_ref[0, :, cs]
        gated = y * (zz * _sigmoid(zz))
        o_ref[0, :, cs] = _rms(gated, nw_ref[:, cs]).astype(o_ref.dtype)

    xbuf_ref[0:8, :] = xbuf_ref[q:q + 8, :]


def ssd_core(zx, conv_w, conv_b, dt_bias, a_log, d_skip, norm_w, b, s):
    q = SSM_CHUNK
    zx3 = zx.reshape(b, s, SSM_IN_PAD)
    pad = lambda v: jnp.pad(v.astype(F32), (0, LANES - SSM_HEADS)).reshape(1, LANES)
    tril = jnp.asarray(np.tril(np.ones((q, q), np.float32)))
    wide = SSM_D_INNER
    full = lambda shape: pl.BlockSpec(shape, lambda i, c: (0,) * len(shape))
    out = pl.pallas_call(
        _ssd_kernel,
        grid=(b, s // q),
        in_specs=[pl.BlockSpec((1, q, wide), lambda i, c: (i, c, 0)),
                  pl.BlockSpec((1, q, wide), lambda i, c: (i, c, 1)),
                  pl.BlockSpec((1, q, wide), lambda i, c: (i, c, 2)),
                  pl.BlockSpec((1, q, LANES), lambda i, c: (i, c, (SSM_D_INNER + SSM_CONV_DIM) // LANES)),
                  full((SSM_CONV, SSM_CONV_DIM)), full((1, SSM_CONV_DIM)),
                  full((1, LANES)), full((1, LANES)), full((1, wide)), full((1, wide)),
                  full((q, q))],
        out_specs=pl.BlockSpec((1, q, wide), lambda i, c: (i, c, 0)),
        out_shape=jax.ShapeDtypeStruct((b, s, wide), BF16),
        scratch_shapes=[pltpu.VMEM((SSM_GROUPS, SSM_STATE, GQA_R * HEAD_DIM), F32),
                        pltpu.VMEM((q + 8, SSM_CONV_DIM), F32)],
        compiler_params=_cparams("parallel", "arbitrary"),
        name="ssd_core",
    )(zx3, zx3, zx3, zx3, conv_w.astype(F32), conv_b.reshape(1, -1).astype(F32),
      pad(dt_bias), pad(a_log), jnp.repeat(d_skip.astype(F32), HEAD_DIM).reshape(1, wide),
      norm_w.reshape(1, wide).astype(F32), tril)
    return out.reshape(b * s, wide)


def _swa_kernel(q_ref, kc_ref, kp_ref, vc_ref, vp_ref, sink_ref, o_ref):
    i = pl.program_id(1)
    w = SWA_BLOCK
    row = lax.broadcasted_iota(jnp.int32, (w, 2 * w), 0)
    col = lax.broadcasted_iota(jnp.int32, (w, 2 * w), 1)
    valid = ((col < w) & (col > row) & (i > 0)) | ((col >= w) & ((col - w) <= row))
    for kv in range(KV_HEADS):
        ks = slice(HEAD_DIM * kv, HEAD_DIM * (kv + 1))
        k = jnp.concatenate([kp_ref[0, :, ks], kc_ref[0, :, ks]], axis=0)
        v = jnp.concatenate([vp_ref[0, :, ks], vc_ref[0, :, ks]], axis=0)
        for pair in range(GQA_R // 2):
            outs = []
            for r in (2 * pair, 2 * pair + 1):
                h = GQA_R * kv + r
                qh = q_ref[0, :, HEAD_DIM * h:HEAD_DIM * (h + 1)]
                sc = jnp.where(valid, _dot_nt(qh, k) * ATTN_SCALE, -jnp.inf)
                sink = sink_ref[:, h:h + 1]
                m = jnp.maximum(jnp.max(sc, axis=-1, keepdims=True), sink)
                e = jnp.exp(sc - m)
                den = jnp.sum(e, axis=-1, keepdims=True) + jnp.exp(sink - m)
                outs.append(_dot((e / den).astype(BF16), v))
            h0 = GQA_R * kv + 2 * pair
            o_ref[0, :, HEAD_DIM * h0:HEAD_DIM * (h0 + 2)] = jnp.concatenate(outs, axis=1).astype(o_ref.dtype)


def swa_core(qkv, sinks, b, s):
    w = SWA_BLOCK
    qkv3 = qkv.reshape(b, s, -1)
    qd = N_HEADS * HEAD_DIM
    kcol = qd // KV_DIM
    prev = lambda i, j: (i, jnp.maximum(j - 1, 0))
    out = pl.pallas_call(
        _swa_kernel,
        grid=(b, s // w),
        in_specs=[pl.BlockSpec((1, w, qd), lambda i, j: (i, j, 0)),
                  pl.BlockSpec((1, w, KV_DIM), lambda i, j: (i, j, kcol)),
                  pl.BlockSpec((1, w, KV_DIM), lambda i, j: prev(i, j) + (kcol,)),
                  pl.BlockSpec((1, w, KV_DIM), lambda i, j: (i, j, kcol + 1)),
                  pl.BlockSpec((1, w, KV_DIM), lambda i, j: prev(i, j) + (kcol + 1,)),
                  pl.BlockSpec((1, LANES), lambda i, j: (0, 0))],
        out_specs=pl.BlockSpec((1, w, qd), lambda i, j: (i, j, 0)),
        out_shape=jax.ShapeDtypeStruct((b, s, qd), BF16),
        compiler_params=_cparams("parallel", "arbitrary"),
        name="swa_core",
    )(qkv3, qkv3, qkv3, qkv3, qkv3, jnp.pad(sinks.astype(F32), (0, LANES - N_HEADS)).reshape(1, LANES))
    return out.reshape(b * s, qd)


def _compress_kernel(f_ref, pe_ref, w1_ref, w2_ref, o_ref):
    half = NSA_CMP_STRIDE * HEAD_DIM
    x = f_ref[0, 0]
    za = (x + pe_ref[:, 0:half]).astype(BF16)
    zb = (x + pe_ref[:, half:2 * half]).astype(BF16)
    hb = _dot(zb, w1_ref[half:2 * half, :])
    hid = _dot(za, w1_ref[0:half, :]) + pltpu.roll(hb, hb.shape[0] - 1, 0)
    o_ref[0, 0] = _dot((hid * _sigmoid(hid)).astype(BF16), w2_ref[...]).astype(o_ref.dtype)


def compress(tok, pe, w1, w2, b, s):
    nch = s // NSA_CMP_STRIDE
    half = NSA_CMP_STRIDE * HEAD_DIM
    f = tok.reshape(b, nch, NSA_CMP_STRIDE, KV_HEADS, HEAD_DIM).transpose(0, 3, 1, 2, 4).reshape(b, KV_HEADS, nch, half)
    return pl.pallas_call(
        _compress_kernel,
        grid=(b, KV_HEADS),
        in_specs=[pl.BlockSpec((1, 1, nch, half), lambda i, j: (i, j, 0, 0)),
                  pl.BlockSpec((1, 2 * half), lambda i, j: (0, 0)),
                  pl.BlockSpec((2 * half, NSA_CMP_HIDDEN), lambda i, j: (0, 0)),
                  pl.BlockSpec((NSA_CMP_HIDDEN, HEAD_DIM), lambda i, j: (0, 0))],
        out_specs=pl.BlockSpec((1, 1, nch, HEAD_DIM), lambda i, j: (i, j, 0, 0)),
        out_shape=jax.ShapeDtypeStruct((b, KV_HEADS, nch, HEAD_DIM), BF16),
        compiler_params=_cparams("parallel", "parallel"),
        name="nsa_compress",
    )(f, pe.reshape(1, 2 * half).astype(F32), w1.astype(BF16), w2.astype(BF16))


def _masked_softmax(logits, mask):
    l = jnp.where(mask, logits, MASK_NEG)
    e = jnp.exp(l - jnp.max(l, axis=-1, keepdims=True))
    return jnp.where(mask, e / jnp.sum(e, axis=-1, keepdims=True), 0.0)


def _nsa_kernel(q_ref, ks_ref, vs_ref, kw_ref, vw_ref, kcmp_ref, vcmp_ref, g_ref, ovt_ref, exp_ref,
                o_ref, *, seq):
    i = pl.program_id(1)
    qb = NSA_Q_BLOCK
    rows = GQA_R * qb
    n_cmp = seq // NSA_CMP_STRIDE
    n_sel = seq // NSA_SEL_BLOCK
    t0 = i * qb
    qpos = t0 + lax.broadcasted_iota(jnp.int32, (rows, 1), 0) % qb
    gates = _sigmoid(g_ref[0])

    cmp_end = lax.broadcasted_iota(jnp.int32, (rows, n_cmp), 1) * NSA_CMP_STRIDE + (NSA_CMP_LEN - 1)
    mask_c = cmp_end <= qpos
    blk = lax.broadcasted_iota(jnp.int32, (qb, n_sel), 1)
    blk_f = blk.astype(F32)
    forced = (blk == i) | (blk == 0)
    allowed = blk <= i
    span = NSA_WINDOW + qb
    w0 = pl.multiple_of(jnp.maximum(t0 - NSA_WINDOW, 0), qb)
    delta_w = qpos - (w0 + lax.broadcasted_iota(jnp.int32, (rows, span), 1))
    mask_w = (delta_w >= 0) & (delta_w < NSA_WINDOW)
    n_chunks = i // (NSA_SEL_CHUNK // NSA_SEL_BLOCK) + 1

    for kv in range(KV_HEADS):
        ks = slice(HEAD_DIM * kv, HEAD_DIM * (kv + 1))
        qs = jnp.concatenate(
            [q_ref[0, :, HEAD_DIM * (GQA_R * kv + r):HEAD_DIM * (GQA_R * kv + r + 1)] for r in range(GQA_R)],
            axis=0)

        pc = _masked_softmax(_dot_nt(qs, kcmp_ref[0, kv]) * ATTN_SCALE, mask_c)
        o_cmp = _dot(pc.astype(BF16), vcmp_ref[0, kv])
        psum = pc[0:qb] + pc[qb:2 * qb] + pc[2 * qb:3 * qb] + pc[3 * qb:4 * qb]
        imp = jnp.dot(psum, ovt_ref[...], precision=lax.Precision.HIGHEST, preferred_element_type=F32)
        imp = jnp.where(allowed, jnp.where(forced, jnp.inf, imp), -jnp.inf)

        sel = jnp.zeros((qb, n_sel), jnp.bool_)
        for _ in range(NSA_TOP_N):
            best = jnp.max(imp, axis=-1, keepdims=True)
            first = jnp.min(jnp.where(imp == best, blk_f, float(n_sel)), axis=-1, keepdims=True)
            pick = blk_f == first
            sel = sel | pick
            imp = jnp.where(pick, -jnp.inf, imp)
        sel_bf = jnp.where(sel, 1.0, 0.0).astype(BF16)

        def sel_step(c, carry):
            m_run, l_run, acc = carry
            k0 = pl.multiple_of(c * NSA_SEL_CHUNK, NSA_SEL_CHUNK)
            kc = ks_ref[0, pl.ds(k0, NSA_SEL_CHUNK), ks]
            vc = vs_ref[0, pl.ds(k0, NSA_SEL_CHUNK), ks]
            picked = _dot(sel_bf, exp_ref[c])
            picked = jnp.concatenate([picked] * GQA_R, axis=0)
            kpos = k0 + lax.broadcasted_iota(jnp.int32, (rows, NSA_SEL_CHUNK), 1)
            mask = (picked > 0.5) & (kpos <= qpos)
            sc = jnp.where(mask, _dot_nt(qs, kc) * ATTN_SCALE, MASK_NEG)
            m_new = jnp.maximum(m_run, jnp.max(sc, axis=-1, keepdims=True))
            p = jnp.where(mask, jnp.exp(sc - m_new), 0.0)
            alpha = jnp.exp(m_run - m_new)
            return (m_new, alpha * l_run + jnp.sum(p, axis=-1, keepdims=True),
                    alpha * acc + _dot(p.astype(BF16), vc))

        m_fin, l_fin, acc_fin = lax.fori_loop(
            0, n_chunks, sel_step,
            (jnp.full((rows, 1), MASK_NEG, F32), jnp.zeros((rows, 1), F32), jnp.zeros((rows, HEAD_DIM), F32)))
        o_sel = acc_fin / l_fin

        pw = _masked_softmax(_dot_nt(qs, kw_ref[0, pl.ds(w0, span), ks]) * ATTN_SCALE, mask_w)
        o_win = _dot(pw.astype(BF16), vw_ref[0, pl.ds(w0, span), ks])

        outs = []
        for r in range(GQA_R):
            g0 = 3 * (GQA_R * kv + r)
            rs = slice(qb * r, qb * (r + 1))
            outs.append(gates[:, g0:g0 + 1] * o_cmp[rs] + gates[:, g0 + 1:g0 + 2] * o_sel[rs]
                        + gates[:, g0 + 2:g0 + 3] * o_win[rs])
        o_ref[0, :, KV_DIM * kv:KV_DIM * (kv + 1)] = jnp.concatenate(outs, axis=1).astype(o_ref.dtype)


def nsa_core(pa, pb, k_cmp, v_cmp, b, s):
    qb = NSA_Q_BLOCK
    qd = N_HEADS * HEAD_DIM
    n_cmp = s // NSA_CMP_STRIDE
    n_sel = s // NSA_SEL_BLOCK
    pa3 = pa.reshape(b, s, -1)
    pb3 = pb.reshape(b, s, -1)
    sel_lo = np.arange(n_sel)[None, :] * NSA_SEL_BLOCK
    cmp_lo = np.arange(n_cmp)[:, None] * NSA_CMP_STRIDE
    ov = np.clip(np.minimum(sel_lo + NSA_SEL_BLOCK, cmp_lo + NSA_CMP_LEN) - np.maximum(sel_lo, cmp_lo), 0, None)
    ovt = jnp.asarray((ov / NSA_CMP_LEN).astype(np.float32))
    nck = s // NSA_SEL_CHUNK
    key_blk = np.arange(s) // NSA_SEL_BLOCK
    expand = (np.arange(n_sel)[:, None] == key_blk[None, :]).astype(np.float32)
    expand = jnp.asarray(expand.reshape(n_sel, nck, NSA_SEL_CHUNK).transpose(1, 0, 2), BF16)
    kcol = qd // KV_DIM
    seqspec = lambda col: pl.BlockSpec((1, s, KV_DIM), lambda i, j: (i, 0, col))
    out = pl.pallas_call(
        functools.partial(_nsa_kernel, seq=s),
        grid=(b, s // qb),
        in_specs=[pl.BlockSpec((1, qb, qd), lambda i, j: (i, j, 0)),
                  seqspec(kcol), seqspec(kcol + 1), seqspec(kcol + 2), seqspec(kcol + 3),
                  pl.BlockSpec((1, KV_HEADS, n_cmp, HEAD_DIM), lambda i, j: (i, 0, 0, 0)),
                  pl.BlockSpec((1, KV_HEADS, n_cmp, HEAD_DIM), lambda i, j: (i, 0, 0, 0)),
                  pl.BlockSpec((1, qb, KV_DIM), lambda i, j: (i, j, 2)),
                  pl.BlockSpec((n_cmp, n_sel), lambda i, j: (0, 0)),
                  pl.BlockSpec((nck, n_sel, NSA_SEL_CHUNK), lambda i, j: (0, 0, 0))],
        out_specs=pl.BlockSpec((1, qb, qd), lambda i, j: (i, j, 0)),
        out_shape=jax.ShapeDtypeStruct((b, s, qd), BF16),
        compiler_params=_cparams("parallel", "arbitrary"),
        name="nsa_core",
    )(pa3, pa3, pa3, pa3, pa3, k_cmp, v_cmp, pb3, ovt, expand)
    return out.reshape(b * s, qd)


def _rope_tables(positions):
    half = ROPE_DIM // 2
    inv_freq = ROPE_THETA ** (-jnp.arange(0, ROPE_DIM, 2, dtype=F32) / ROPE_DIM)
    ang = positions.astype(F32).reshape(-1)[:, None] * inv_freq
    cos, sin = jnp.cos(ang), jnp.sin(ang)
    t = cos.shape[0]
    ones = jnp.ones((t, HEAD_DIM - ROPE_DIM), F32)
    zeros = jnp.zeros((t, HEAD_DIM - ROPE_DIM), F32)
    zh = jnp.zeros((t, half), F32)
    c = jnp.concatenate([cos, cos, ones], axis=1)
    sa = jnp.concatenate([-sin, zh, zeros], axis=1)
    sb = jnp.concatenate([zh, sin, zeros], axis=1)
    return tuple(jnp.tile(v, (1, LANES // HEAD_DIM)) for v in (c, sa, sb))


def _pad_cols(w, n):
    return jnp.pad(w, ((0, 0), (0, n - w.shape[1])))


def mamba2_mixer(h, ln_w, w_in, conv_w, conv_b, dt_bias, a_log, d_skip, norm_w, w_out, b, s):
    w = _pad_cols(w_in, SSM_IN_PAD).astype(BF16)
    zx = norm_proj(h, ln_w, w, jnp.zeros((SSM_IN_PAD,), F32), F32)
    y = ssd_core(zx, conv_w, conv_b, dt_bias, a_log, d_skip, norm_w, b, s)
    return out_proj(y, w_out.astype(BF16), jnp.zeros((D_MODEL,), F32), h)


def swa_mixer(h, ln_w, w_qkv, b_qkv, sinks, w_o, b_o, rope, b, s):
    n_rope = (N_HEADS + KV_HEADS) * HEAD_DIM // PROJ_TILE
    qkv = norm_proj(h, ln_w, w_qkv.astype(BF16), b_qkv.astype(F32), BF16, rope, tuple(range(n_rope)))
    o = swa_core(qkv, sinks, b, s)
    return out_proj(o, w_o.astype(BF16), b_o.astype(F32), h)


def nsa_mixer(h, ln_w, w_in, pe_k, k_w1, k_w2, pe_v, v_w1, v_w2, w_o, rope, b, s):
    qd = N_HEADS * HEAD_DIM
    cols = lambda k: w_in[:, qd + KV_DIM * k:qd + KV_DIM * (k + 1)]
    wa = jnp.concatenate([w_in[:, :qd], cols(2), cols(3), cols(4), cols(5)], axis=1).astype(BF16)
    wb = _pad_cols(jnp.concatenate([cols(0), cols(1), w_in[:, qd + 6 * KV_DIM:]], axis=1), 3 * KV_DIM).astype(BF16)
    q_tiles = qd // PROJ_TILE
    pa = norm_proj(h, ln_w, wa, jnp.zeros((wa.shape[1],), F32), BF16, rope,
                   tuple(range(q_tiles)) + (q_tiles, q_tiles + 2))
    pb = norm_proj(h, ln_w, wb, jnp.zeros((wb.shape[1],), F32), F32, rope, (0,))
    pb3 = pb.reshape(b, s, -1)
    k_cmp = compress(pb3[..., 0:KV_DIM], pe_k, k_w1, k_w2, b, s)
    v_cmp = compress(pb3[..., KV_DIM:2 * KV_DIM], pe_v, v_w1, v_w2, b, s)
    o = nsa_core(pa, pb, k_cmp, v_cmp, b, s)
    return out_proj(o, w_o.astype(BF16), jnp.zeros((D_MODEL,), F32), h)


def kernel(x, positions, ln_ffn1, ffn1_w_in, ffn1_w_out, ln_mix, ln_ffn2, ffn2_w_in, ffn2_w_out, ssm_w_in, ssm_conv_w, ssm_conv_b, ssm_dt_bias, ssm_a_log, ssm_d, ssm_norm_w, ssm_w_out, swa_w_qkv, swa_b_qkv, swa_sinks, swa_w_o, swa_b_o, nsa_w_in, nsa_pe_k, nsa_k_w1, nsa_k_w2, nsa_pe_v, nsa_v_w1, nsa_v_w2, nsa_w_o, final_norm):
    b, s, d = x.shape
    depth = ln_ffn1.shape[0]
    rope = _rope_tables(positions)
    h = x.reshape(b * s, d)
    for i in range(depth):
        kind, inst = i % 3, i // 3
        h = ffn(h, ln_ffn1[i], ffn1_w_in[i].astype(BF16), ffn1_w_out[i].astype(BF16))
        if kind == 0:
            h = mamba2_mixer(h, ln_mix[i], ssm_w_in[inst], ssm_conv_w[inst], ssm_conv_b[inst],
                             ssm_dt_bias[inst], ssm_a_log[inst], ssm_d[inst], ssm_norm_w[inst],
                             ssm_w_out[inst], b, s)
        elif kind == 1:
            h = swa_mixer(h, ln_mix[i], swa_w_qkv[inst], swa_b_qkv[inst], swa_sinks[inst],
                          swa_w_o[inst], swa_b_o[inst], rope, b, s)
        else:
            h = nsa_mixer(h, ln_mix[i], nsa_w_in[inst], nsa_pe_k[inst], nsa_k_w1[inst], nsa_k_w2[inst],
                          nsa_pe_v[inst], nsa_v_w1[inst], nsa_v_w2[inst], nsa_w_o[inst], rope, b, s)
        h = ffn(h, ln_ffn2[i], ffn2_w_in[i].astype(BF16), ffn2_w_out[i].astype(BF16),
                final_norm if i == depth - 1 else None)
    return h.reshape(b, s, d)
```

```python
import functools
import math

import numpy as np
import jax
import jax.numpy as jnp
from jax import lax
from jax.experimental import pallas as pl
from jax.experimental.pallas import tpu as pltpu

F32 = jnp.float32
BF16 = jnp.bfloat16

D_MODEL = 1024
RMS_EPS = 1e-6
D_FF = 2816
HEAD_DIM = 64
ROPE_DIM = HEAD_DIM // 4
ROPE_THETA = 500000.0
ATTN_SCALE = HEAD_DIM ** -0.5
MASK_NEG = -1e30

SSM_D_INNER = 2 * D_MODEL
SSM_HEADS = 32
SSM_GROUPS = 8
SSM_STATE = 128
SSM_CONV = 4
SSM_CHUNK = 128
SSM_CONV_DIM = SSM_D_INNER + 2 * SSM_GROUPS * SSM_STATE
SSM_IN_DIM = SSM_D_INNER + SSM_CONV_DIM + SSM_HEADS
SSM_IN_PAD = SSM_D_INNER + SSM_CONV_DIM + 256

N_HEADS = 16
KV_HEADS = 4
GQA_R = N_HEADS // KV_HEADS
KV_DIM = KV_HEADS * HEAD_DIM

SWA_WINDOW = 128
SWA_BLOCK = 128

NSA_CMP_LEN = 32
NSA_CMP_STRIDE = 16
NSA_CMP_HIDDEN = 256
NSA_SEL_BLOCK = 64
NSA_TOP_N = 8
NSA_WINDOW = 512
NSA_Q_BLOCK = 64
NSA_SEL_CHUNK = 512

LANES = 128
VMEM_LIMIT = 56 * 1024 * 1024
TOKEN_TILE = 1024
FF_TILE = 256
PROJ_TILE = 256


def _cparams(*sem):
    return pltpu.CompilerParams(dimension_semantics=sem, vmem_limit_bytes=VMEM_LIMIT)


def _dot(a, b):
    return jnp.dot(a, b, preferred_element_type=F32)


def _dot_nt(a, b):
    return lax.dot_general(a, b, (((1,), (1,)), ((), ())), preferred_element_type=F32)


def _sigmoid(x):
    return 1.0 / (1.0 + jnp.exp(-x))


def _rms(x, w):
    return x * lax.rsqrt(jnp.mean(x * x, axis=-1, keepdims=True) + RMS_EPS) * w


def _norm_proj_kernel(*refs, rope_tiles, tn):
    if rope_tiles:
        x_ref, lnw_ref, w_ref, b_ref, c_ref, sa_ref, sb_ref, o_ref, xn_ref = refs
    else:
        x_ref, lnw_ref, w_ref, b_ref, o_ref, xn_ref = refs
    j = pl.program_id(1)

    @pl.when(j == 0)
    def _():
        xn_ref[...] = _rms(x_ref[...], lnw_ref[...]).astype(BF16)

    acc = _dot(xn_ref[...], w_ref[...]) + b_ref[...]
    if not rope_tiles:
        o_ref[...] = acc.astype(o_ref.dtype)
        return
    is_rope = functools.reduce(jnp.logical_or, [j == t for t in rope_tiles])

    @pl.when(is_rope)
    def _():
        c = c_ref[...]
        sa = sa_ref[...]
        sb = sb_ref[...]
        for k in range(tn // LANES):
            seg = acc[:, LANES * k:LANES * (k + 1)]
            rot = (seg * c + pltpu.roll(seg, LANES - ROPE_DIM // 2, 1) * sa
                   + pltpu.roll(seg, ROPE_DIM // 2, 1) * sb)
            o_ref[:, LANES * k:LANES * (k + 1)] = rot.astype(o_ref.dtype)

    @pl.when(jnp.logical_not(is_rope))
    def _():
        o_ref[...] = acc.astype(o_ref.dtype)


def norm_proj(h, ln_w, w, bias, out_dtype, rope=None, rope_tiles=()):
    t, d = h.shape
    n = w.shape[1]
    tm = min(TOKEN_TILE, t)
    tn = PROJ_TILE
    assert n % tn == 0 and t % tm == 0
    in_specs = [pl.BlockSpec((tm, d), lambda i, j: (i, 0)),
                pl.BlockSpec((1, d), lambda i, j: (0, 0)),
                pl.BlockSpec((d, tn), lambda i, j: (0, j)),
                pl.BlockSpec((1, tn), lambda i, j: (0, j))]
    args = [h, ln_w.reshape(1, d), w, bias.reshape(1, n)]
    if rope_tiles:
        in_specs += [pl.BlockSpec((tm, LANES), lambda i, j: (i, 0))] * 3
        args += list(rope)
    return pl.pallas_call(
        functools.partial(_norm_proj_kernel, rope_tiles=tuple(rope_tiles), tn=tn),
        grid=(t // tm, n // tn),
        in_specs=in_specs,
        out_specs=pl.BlockSpec((tm, tn), lambda i, j: (i, j)),
        out_shape=jax.ShapeDtypeStruct((t, n), out_dtype),
        scratch_shapes=[pltpu.VMEM((tm, d), BF16)],
        compiler_params=_cparams("parallel", "arbitrary"),
        name="norm_proj",
    )(*args)


def _out_proj_kernel(y_ref, w_ref, b_ref, r_ref, o_ref):
    o_ref[...] = r_ref[...] + _dot(y_ref[...], w_ref[...]) + b_ref[...]


def out_proj(y, w, bias, res):
    t, k = y.shape
    d = w.shape[1]
    tm = min(TOKEN_TILE, t)
    return pl.pallas_call(
        _out_proj_kernel,
        grid=(t // tm,),
        in_specs=[pl.BlockSpec((tm, k), lambda i: (i, 0)),
                  pl.BlockSpec((k, d), lambda i: (0, 0)),
                  pl.BlockSpec((1, d), lambda i: (0, 0)),
                  pl.BlockSpec((tm, d), lambda i: (i, 0))],
        out_specs=pl.BlockSpec((tm, d), lambda i: (i, 0)),
        out_shape=jax.ShapeDtypeStruct((t, d), F32),
        compiler_params=_cparams("parallel"),
        name="out_proj",
    )(y, w, bias.reshape(1, d), res)


def _ffn_kernel(x_ref, lnw_ref, wg_ref, wu_ref, wo_ref, fnw_ref, o_ref, xn_ref, acc_ref, *, final_norm):
    f = pl.program_id(1)

    @pl.when(f == 0)
    def _():
        xn_ref[...] = _rms(x_ref[...], lnw_ref[...]).astype(BF16)

    xn = xn_ref[...]
    g = _dot(xn, wg_ref[...])
    u = _dot(xn, wu_ref[...])
    part = _dot((g * _sigmoid(g) * u).astype(BF16), wo_ref[...])

    @pl.when(f == 0)
    def _():
        acc_ref[...] = part

    @pl.when(f > 0)
    def _():
        acc_ref[...] += part

    @pl.when(f == pl.num_programs(1) - 1)
    def _():
        y = x_ref[...] + 0.5 * acc_ref[...]
        if final_norm:
            y = _rms(y, fnw_ref[...])
        o_ref[...] = y


def ffn(h, ln_w, w_in, w_out, final_w=None):
    t, d = h.shape
    tm = min(TOKEN_TILE, t)
    tf = FF_TILE
    nf = D_FF // tf
    final_norm = final_w is not None
    fnw = (final_w if final_norm else ln_w).reshape(1, d)
    return pl.pallas_call(
        functools.partial(_ffn_kernel, final_norm=final_norm),
        grid=(t // tm, nf),
        in_specs=[pl.BlockSpec((tm, d), lambda i, f: (i, 0)),
                  pl.BlockSpec((1, d), lambda i, f: (0, 0)),
                  pl.BlockSpec((d, tf), lambda i, f: (0, f)),
                  pl.BlockSpec((d, tf), lambda i, f: (0, nf + f)),
                  pl.BlockSpec((tf, d), lambda i, f: (f, 0)),
                  pl.BlockSpec((1, d), lambda i, f: (0, 0))],
        out_specs=pl.BlockSpec((tm, d), lambda i, f: (i, 0)),
        out_shape=jax.ShapeDtypeStruct((t, d), F32),
        scratch_shapes=[pltpu.VMEM((tm, d), BF16), pltpu.VMEM((tm, d), F32)],
        compiler_params=_cparams("parallel", "arbitrary"),
        name="ffn",
    )(h, ln_w.reshape(1, d), w_in, w_in, w_out, fnw)


def _expand_heads(mat, g):
    rows = mat.shape[0]
    lane = lax.broadcasted_iota(jnp.int32, (rows, LANES), 1)
    pieces = []
    for p in range(2):
        h0 = GQA_R * g + 2 * p
        a = jnp.broadcast_to(mat[:, h0:h0 + 1], (rows, LANES))
        b = jnp.broadcast_to(mat[:, h0 + 1:h0 + 2], (rows, LANES))
        pieces.append(jnp.where(lane < HEAD_DIM, a, b))
    return jnp.concatenate(pieces, axis=1)


def _ssd_kernel(z_ref, x_ref, bc_ref, dt_ref, cw_ref, cb_ref, dtb_ref, alog_ref, dsk_ref, nw_ref,
                tril_ref, o_ref, state_ref, xbuf_ref):
    q = SSM_CHUNK
    c = pl.program_id(1)

    @pl.when(c == 0)
    def _():
        state_ref[...] = jnp.zeros(state_ref.shape, F32)
        xbuf_ref[0:8, :] = jnp.zeros((8, SSM_CONV_DIM), F32)

    xbuf_ref[8:8 + q, 0:SSM_D_INNER] = x_ref[0]
    xbuf_ref[8:8 + q, SSM_D_INNER:SSM_CONV_DIM] = bc_ref[0]

    def conv_silu(c0, width):
        acc = cb_ref[:, c0:c0 + width]
        for k in range(SSM_CONV):
            r0 = 8 - (SSM_CONV - 1) + k
            acc = acc + cw_ref[k:k + 1, c0:c0 + width] * xbuf_ref[r0:r0 + q, c0:c0 + width]
        return acc * _sigmoid(acc)

    dtr = dt_ref[0] + dtb_ref[...]
    dt = jnp.maximum(dtr, 0.0) + jnp.log1p(jnp.exp(-jnp.abs(dtr)))
    ad = dt * (-jnp.exp(alog_ref[...]))
    acum = jnp.dot(tril_ref[...], ad, precision=lax.Precision.HIGHEST,
                   preferred_element_type=F32)
    acum_row = acum.T
    a_last = acum[q - 1:q, :]
    exp_acum = jnp.exp(acum)
    decay_st = jnp.exp(a_last - acum)
    chunk_decay = jnp.exp(a_last)
    causal = (lax.broadcasted_iota(jnp.int32, (q, q), 0) >= lax.broadcasted_iota(jnp.int32, (q, q), 1))
    head_of_lane = lax.broadcasted_iota(jnp.int32, (q, 2 * LANES), 1) // HEAD_DIM

    for g in range(SSM_GROUPS):
        gw = GQA_R * HEAD_DIM
        cs = slice(gw * g, gw * (g + 1))
        xs = conv_silu(gw * g, gw)
        bm = conv_silu(SSM_D_INNER + SSM_STATE * g, SSM_STATE)
        cm = conv_silu(SSM_D_INNER + SSM_GROUPS * SSM_STATE + SSM_STATE * g, SSM_STATE)
        xd = xs * _expand_heads(dt, g)
        cbf = cm.astype(BF16)
        cb = _dot_nt(cbf, bm.astype(BF16))
        st = state_ref[g]
        y = _dot(cbf, st.astype(BF16)) * _expand_heads(exp_acum, g)
        for r in range(GQA_R):
            hh = GQA_R * g + r
            diff = acum[:, hh:hh + 1] - acum_row[hh:hh + 1, :]
            decay = jnp.exp(jnp.where(causal, diff, -jnp.inf))
            xdm = jnp.where(head_of_lane == r, xd, 0.0).astype(BF16)
            y = y + _dot((cb * decay).astype(BF16), xdm)
        state_ref[g] = (st * _expand_heads(chunk_decay, g)
                        + _dot(bm.T.astype(BF16), (xd * _expand_heads(decay_st, g)).astype(BF16)))
        y = y + dsk_ref[:, cs] * xs
        zz = z_ref[0, :, cs]
        gated = y * (zz * _sigmoid(zz))
        o_ref[0, :, cs] = _rms(gated, nw_ref[:, cs]).astype(o_ref.dtype)

    xbuf_ref[0:8, :] = xbuf_ref[q:q + 8, :]


def ssd_core(zx, conv_w, conv_b, dt_bias, a_log, d_skip, norm_w, b, s):
    q = SSM_CHUNK
    zx3 = zx.reshape(b, s, SSM_IN_PAD)
    pad = lambda v: jnp.pad(v.astype(F32), (0, LANES - SSM_HEADS)).reshape(1, LANES)
    tril = jnp.asarray(np.tril(np.ones((q, q), np.float32)))
    wide = SSM_D_INNER
    full = lambda shape: pl.BlockSpec(shape, lambda i, c: (0,) * len(shape))
    out = pl.pallas_call(
        _ssd_kernel,
        grid=(b, s // q),
        in_specs=[pl.BlockSpec((1, q, wide), lambda i, c: (i, c, 0)),
                  pl.BlockSpec((1, q, wide), lambda i, c: (i, c, 1)),
                  pl.BlockSpec((1, q, wide), lambda i, c: (i, c, 2)),
                  pl.BlockSpec((1, q, LANES), lambda i, c: (i, c, (SSM_D_INNER + SSM_CONV_DIM) // LANES)),
                  full((SSM_CONV, SSM_CONV_DIM)), full((1, SSM_CONV_DIM)),
                  full((1, LANES)), full((1, LANES)), full((1, wide)), full((1, wide)),
                  full((q, q))],
        out_specs=pl.BlockSpec((1, q, wide), lambda i, c: (i, c, 0)),
        out_shape=jax.ShapeDtypeStruct((b, s, wide), BF16),
        scratch_shapes=[pltpu.VMEM((SSM_GROUPS, SSM_STATE, GQA_R * HEAD_DIM), F32),
                        pltpu.VMEM((q + 8, SSM_CONV_DIM), F32)],
        compiler_params=_cparams("parallel", "arbitrary"),
        name="ssd_core",
    )(zx3, zx3, zx3, zx3, conv_w.astype(F32), conv_b.reshape(1, -1).astype(F32),
      pad(dt_bias), pad(a_log), jnp.repeat(d_skip.astype(F32), HEAD_DIM).reshape(1, wide),
      norm_w.reshape(1, wide).astype(F32), tril)
    return out.reshape(b * s, wide)


def _swa_kernel(q_ref, kc_ref, kp_ref, vc_ref, vp_ref, sink_ref, o_ref):
    i = pl.program_id(1)
    w = SWA_BLOCK
    row = lax.broadcasted_iota(jnp.int32, (w, 2 * w), 0)
    col = lax.broadcasted_iota(jnp.int32, (w, 2 * w), 1)
    valid = ((col < w) & (col > row) & (i > 0)) | ((col >= w) & ((col - w) <= row))
    for kv in range(KV_HEADS):
        ks = slice(HEAD_DIM * kv, HEAD_DIM * (kv + 1))
        k = jnp.concatenate([kp_ref[0, :, ks], kc_ref[0, :, ks]], axis=0)
        v = jnp.concatenate([vp_ref[0, :, ks], vc_ref[0, :, ks]], axis=0)
        for pair in range(GQA_R // 2):
            outs = []
            for r in (2 * pair, 2 * pair + 1):
                h = GQA_R * kv + r
                qh = q_ref[0, :, HEAD_DIM * h:HEAD_DIM * (h + 1)]
                sc = jnp.where(valid, _dot_nt(qh, k) * ATTN_SCALE, -jnp.inf)
                sink = sink_ref[:, h:h + 1]
                m = jnp.maximum(jnp.max(sc, axis=-1, keepdims=True), sink)
                e = jnp.exp(sc - m)
                den = jnp.sum(e, axis=-1, keepdims=True) + jnp.exp(sink - m)
                outs.append(_dot((e / den).astype(BF16), v))
            h0 = GQA_R * kv + 2 * pair
            o_ref[0, :, HEAD_DIM * h0:HEAD_DIM * (h0 + 2)] = jnp.concatenate(outs, axis=1).astype(o_ref.dtype)


def swa_core(qkv, sinks, b, s):
    w = SWA_BLOCK
    qkv3 = qkv.reshape(b, s, -1)
    qd = N_HEADS * HEAD_DIM
    kcol = qd // KV_DIM
    prev = lambda i, j: (i, jnp.maximum(j - 1, 0))
    out = pl.pallas_call(
        _swa_kernel,
        grid=(b, s // w),
        in_specs=[pl.BlockSpec((1, w, qd), lambda i, j: (i, j, 0)),
                  pl.BlockSpec((1, w, KV_DIM), lambda i, j: (i, j, kcol)),
                  pl.BlockSpec((1, w, KV_DIM), lambda i, j: prev(i, j) + (kcol,)),
                  pl.BlockSpec((1, w, KV_DIM), lambda i, j: (i, j, kcol + 1)),
                  pl.BlockSpec((1, w, KV_DIM), lambda i, j: prev(i, j) + (kcol + 1,)),
                  pl.BlockSpec((1, LANES), lambda i, j: (0, 0))],
        out_specs=pl.BlockSpec((1, w, qd), lambda i, j: (i, j, 0)),
        out_shape=jax.ShapeDtypeStruct((b, s, qd), BF16),
        compiler_params=_cparams("parallel", "arbitrary"),
        name="swa_core",
    )(qkv3, qkv3, qkv3, qkv3, qkv3, jnp.pad(sinks.astype(F32), (0, LANES - N_HEADS)).reshape(1, LANES))
    return out.reshape(b * s, qd)


def _compress_kernel(f_ref, pe_ref, w1_ref, w2_ref, o_ref, *, transposed):
    half = NSA_CMP_STRIDE * HEAD_DIM
    x = f_ref[0, 0]
    za = (x + pe_ref[:, 0:half]).astype(BF16)
    zb = (x + pe_ref[:, half:2 * half]).astype(BF16)
    hb = _dot(zb, w1_ref[half:2 * half, :])
    hid = _dot(za, w1_ref[0:half, :]) + pltpu.roll(hb, hb.shape[0] - 1, 0)
    act = (hid * _sigmoid(hid)).astype(BF16)
    if transposed:
        o_ref[0, 0] = _dot_nt(w2_ref[...], act).astype(o_ref.dtype)
    else:
        o_ref[0, 0] = _dot(act, w2_ref[...]).astype(o_ref.dtype)


def compress(tok, pe, w1, w2, b, s, transposed):
    nch = s // NSA_CMP_STRIDE
    half = NSA_CMP_STRIDE * HEAD_DIM
    f = tok.reshape(b, nch, NSA_CMP_STRIDE, KV_HEADS, HEAD_DIM).transpose(0, 3, 1, 2, 4).reshape(b, KV_HEADS, nch, half)
    w2 = (w2.T if transposed else w2).astype(BF16)
    oshape = (HEAD_DIM, nch) if transposed else (nch, HEAD_DIM)
    return pl.pallas_call(
        functools.partial(_compress_kernel, transposed=transposed),
        grid=(b, KV_HEADS),
        in_specs=[pl.BlockSpec((1, 1, nch, half), lambda i, j: (i, j, 0, 0)),
                  pl.BlockSpec((1, 2 * half), lambda i, j: (0, 0)),
                  pl.BlockSpec((2 * half, NSA_CMP_HIDDEN), lambda i, j: (0, 0)),
                  pl.BlockSpec(w2.shape, lambda i, j: (0, 0))],
        out_specs=pl.BlockSpec((1, 1) + oshape, lambda i, j: (i, j, 0, 0)),
        out_shape=jax.ShapeDtypeStruct((b, KV_HEADS) + oshape, BF16),
        compiler_params=_cparams("parallel", "parallel"),
        name="nsa_compress",
    )(f, pe.reshape(1, 2 * half).astype(F32), w1.astype(BF16), w2)


def _masked_softmax(logits, mask):
    l = jnp.where(mask, logits, MASK_NEG)
    e = jnp.exp(l - jnp.max(l, axis=-1, keepdims=True))
    return jnp.where(mask, e / jnp.sum(e, axis=-1, keepdims=True), 0.0)


def _nsa_kernel(q_ref, ks_ref, vs_ref, kw_ref, vw_ref, kcmp_ref, vcmp_ref, g_ref, ovt_ref, exp_ref,
                o_ref, *, seq):
    i = pl.program_id(1)
    qb = NSA_Q_BLOCK
    rows = GQA_R * qb
    n_cmp = seq // NSA_CMP_STRIDE
    n_sel = seq // NSA_SEL_BLOCK
    t0 = i * qb
    qpos = t0 + lax.broadcasted_iota(jnp.int32, (rows, 1), 0) % qb
    gates = _sigmoid(g_ref[0])

    cmp_end = lax.broadcasted_iota(jnp.int32, (rows, n_cmp), 1) * NSA_CMP_STRIDE + (NSA_CMP_LEN - 1)
    mask_c = cmp_end <= qpos
    blk = lax.broadcasted_iota(jnp.int32, (qb, n_sel), 1)
    blk_f = blk.astype(F32)
    forced = (blk == i) | (blk == 0)
    allowed = blk <= i
    span = NSA_WINDOW + qb
    w0 = pl.multiple_of(jnp.maximum(t0 - NSA_WINDOW, 0), qb)
    delta_w = qpos - (w0 + lax.broadcasted_iota(jnp.int32, (rows, span), 1))
    mask_w = (delta_w >= 0) & (delta_w < NSA_WINDOW)
    n_chunks = i // (NSA_SEL_CHUNK // NSA_SEL_BLOCK) + 1

    for kv in range(KV_HEADS):
        ks = slice(HEAD_DIM * kv, HEAD_DIM * (kv + 1))
        qs = jnp.concatenate(
            [q_ref[0, :, HEAD_DIM * (GQA_R * kv + r):HEAD_DIM * (GQA_R * kv + r + 1)] for r in range(GQA_R)],
            axis=0)

        pc = _masked_softmax(_dot_nt(qs, kcmp_ref[0, kv]) * ATTN_SCALE, mask_c)
        o_cmp = _dot(pc.astype(BF16), vcmp_ref[0, kv])
        psum = pc[0:qb] + pc[qb:2 * qb] + pc[2 * qb:3 * qb] + pc[3 * qb:4 * qb]
        imp = jnp.dot(psum, ovt_ref[...], precision=lax.Precision.HIGHEST, preferred_element_type=F32)
        imp = jnp.where(allowed, jnp.where(forced, jnp.inf, imp), -jnp.inf)

        sel = jnp.zeros((qb, n_sel), jnp.bool_)
        for _ in range(NSA_TOP_N):
            best = jnp.max(imp, axis=-1, keepdims=True)
            first = jnp.min(jnp.where(imp == best, blk_f, float(n_sel)), axis=-1, keepdims=True)
            pick = blk_f == first
            sel = sel | pick
            imp = jnp.where(pick, -jnp.inf, imp)
        sel_bf = jnp.where(sel, 1.0, 0.0).astype(BF16)

        def sel_step(c, carry):
            m_run, l_run, acc = carry
            k0 = pl.multiple_of(c * NSA_SEL_CHUNK, NSA_SEL_CHUNK)
            kc = ks_ref[0, pl.ds(k0, NSA_SEL_CHUNK), ks]
            vc = vs_ref[0, pl.ds(k0, NSA_SEL_CHUNK), ks]
            picked = _dot(sel_bf, exp_ref[c])
            picked = jnp.concatenate([picked] * GQA_R, axis=0)
            kpos = k0 + lax.broadcasted_iota(jnp.int32, (rows, NSA_SEL_CHUNK), 1)
            mask = (picked > 0.5) & (kpos <= qpos)
            sc = jnp.where(mask, _dot_nt(qs, kc) * ATTN_SCALE, MASK_NEG)
            m_new = jnp.maximum(m_run, jnp.max(sc, axis=-1, keepdims=True))
            p = jnp.where(mask, jnp.exp(sc - m_new), 0.0)
            alpha = jnp.exp(m_run - m_new)
            return (m_new, alpha * l_run + jnp.sum(p, axis=-1, keepdims=True),
                    alpha * acc + _dot(p.astype(BF16), vc))

        m_fin, l_fin, acc_fin = lax.fori_loop(
            0, n_chunks, sel_step,
            (jnp.full((rows, 1), MASK_NEG, F32), jnp.zeros((rows, 1), F32), jnp.zeros((rows, HEAD_DIM), F32)))
        o_sel = acc_fin / l_fin

        pw = _masked_softmax(_dot_nt(qs, kw_ref[0, pl.ds(w0, span), ks]) * ATTN_SCALE, mask_w)
        o_win = _dot(pw.astype(BF16), vw_ref[0, pl.ds(w0, span), ks])

        outs = []
        for r in range(GQA_R):
            g0 = 3 * (GQA_R * kv + r)
            rs = slice(qb * r, qb * (r + 1))
            outs.append(gates[:, g0:g0 + 1] * o_cmp[rs] + gates[:, g0 + 1:g0 + 2] * o_sel[rs]
                        + gates[:, g0 + 2:g0 + 3] * o_win[rs])
        o_ref[0, :, KV_DIM * kv:KV_DIM * (kv + 1)] = jnp.concatenate(outs, axis=1).astype(o_ref.dtype)


def nsa_core(pa, pb, k_cmp, v_cmp, b, s):
    qb = NSA_Q_BLOCK
    qd = N_HEADS * HEAD_DIM
    n_cmp = s // NSA_CMP_STRIDE
    n_sel = s // NSA_SEL_BLOCK
    pa3 = pa.reshape(b, s, -1)
    pb3 = pb.reshape(b, s, -1)
    sel_lo = np.arange(n_sel)[None, :] * NSA_SEL_BLOCK
    cmp_lo = np.arange(n_cmp)[:, None] * NSA_CMP_STRIDE
    ov = np.clip(np.minimum(sel_lo + NSA_SEL_BLOCK, cmp_lo + NSA_CMP_LEN) - np.maximum(sel_lo, cmp_lo), 0, None)
    ovt = jnp.asarray((ov / NSA_CMP_LEN).astype(np.float32))
    nck = s // NSA_SEL_CHUNK
    key_blk = np.arange(s) // NSA_SEL_BLOCK
    expand = (np.arange(n_sel)[:, None] == key_blk[None, :]).astype(np.float32)
    expand = jnp.asarray(expand.reshape(n_sel, nck, NSA_SEL_CHUNK).transpose(1, 0, 2), BF16)
    kcol = qd // KV_DIM
    seqspec = lambda col: pl.BlockSpec((1, s, KV_DIM), lambda i, j: (i, 0, col))
    out = pl.pallas_call(
        functools.partial(_nsa_kernel, seq=s),
        grid=(b, s // qb),
        in_specs=[pl.BlockSpec((1, qb, qd), lambda i, j: (i, j, 0)),
                  seqspec(kcol), seqspec(kcol + 1), seqspec(kcol + 2), seqspec(kcol + 3),
                  pl.BlockSpec((1, KV_HEADS, n_cmp, HEAD_DIM), lambda i, j: (i, 0, 0, 0)),
                  pl.BlockSpec((1, KV_HEADS, n_cmp, HEAD_DIM), lambda i, j: (i, 0, 0, 0)),
                  pl.BlockSpec((1, qb, KV_DIM), lambda i, j: (i, j, 2)),
                  pl.BlockSpec((n_cmp, n_sel), lambda i, j: (0, 0)),
                  pl.BlockSpec((nck, n_sel, NSA_SEL_CHUNK), lambda i, j: (0, 0, 0))],
        out_specs=pl.BlockSpec((1, qb, qd), lambda i, j: (i, j, 0)),
        out_shape=jax.ShapeDtypeStruct((b, s, qd), BF16),
        compiler_params=_cparams("parallel", "arbitrary"),
        name="nsa_core",
    )(pa3, pa3, pa3, pa3, pa3, k_cmp, v_cmp, pb3, ovt, expand)
    return out.reshape(b * s, qd)


NSA_VT_CHUNK = 128
NSA_WIN_SLABS = NSA_WINDOW // NSA_VT_CHUNK + 1


def _dot_split3(a_bf16, x):
    hi = x.astype(BF16)
    r1 = x - hi.astype(F32)
    mid = r1.astype(BF16)
    lo = (r1 - mid.astype(F32)).astype(BF16)
    return _dot(a_bf16, hi) + _dot(a_bf16, mid) + _dot(a_bf16, lo)


def _softmax_rows_masked(logits, mask):
    l = jnp.where(mask, logits, MASK_NEG)
    e = jnp.exp(l - jnp.max(l, axis=0, keepdims=True))
    return jnp.where(mask, e / jnp.sum(e, axis=0, keepdims=True), 0.0)


def _nsa_t_kernel(q_ref, ks_ref, vst_ref, kw_ref, vwt_ref, kcmp_ref, vcmpt_ref, g_ref, ov_ref,
                  o_ref, qs_ref, bias_ref, ocmp_ref, acc_ref, m_ref, l_ref, *, seq):
    i = pl.program_id(1)
    qb = NSA_Q_BLOCK
    rq = GQA_R * qb
    n_cmp = seq // NSA_CMP_STRIDE
    n_sel = seq // NSA_SEL_BLOCK
    blocks_per_chunk = NSA_SEL_CHUNK // NSA_SEL_BLOCK
    slabs_per_chunk = NSA_SEL_CHUNK // NSA_VT_CHUNK
    t0 = i * qb
    qpos = t0 + (lax.broadcasted_iota(jnp.int32, (1, rq), 1) & (qb - 1))
    gates_t = _sigmoid(g_ref[0]).T

    cmp_end = lax.broadcasted_iota(jnp.int32, (n_cmp, rq), 0) * NSA_CMP_STRIDE + (NSA_CMP_LEN - 1)
    mask_c = cmp_end <= qpos
    blk = lax.broadcasted_iota(jnp.int32, (n_sel, rq), 0)
    blk_f = blk.astype(F32)
    forced = (blk == i) | (blk == 0)
    allowed = blk <= i
    cw = jnp.maximum(t0 - NSA_WINDOW, 0) // NSA_VT_CHUNK
    wkeys = NSA_WIN_SLABS * NSA_VT_CHUNK
    wstart = pl.multiple_of(cw * NSA_VT_CHUNK, NSA_VT_CHUNK)
    delta_w = qpos - (wstart + lax.broadcasted_iota(jnp.int32, (wkeys, rq), 0))
    mask_w = (delta_w >= 0) & (delta_w < NSA_WINDOW)
    last_chunk = i // blocks_per_chunk
    key_in_blk = lax.broadcasted_iota(jnp.int32, (NSA_SEL_BLOCK, rq), 0)

    has_cmp = jnp.where(qpos >= NSA_CMP_LEN - 1, 1.0, 0.0)

    def one_ahead(items, start, finish):
        pending = start(items[0])
        for n, item in enumerate(items):
            nxt = start(items[n + 1]) if n + 1 < len(items) else None
            finish(item, pending)
            pending = nxt

    groups = list(range(KV_HEADS))

    def cmp_start(kv):
        qs = jnp.concatenate(
            [q_ref[0, :, HEAD_DIM * (GQA_R * kv + r):HEAD_DIM * (GQA_R * kv + r + 1)] for r in range(GQA_R)],
            axis=0)
        qs = qs * jnp.asarray(ATTN_SCALE, BF16)
        qs_ref[kv] = qs
        return _dot_nt(kcmp_ref[0, kv], qs)

    def cmp_finish(kv, logits):
        lc = jnp.where(mask_c, logits, MASK_NEG)
        ec = jnp.exp(lc - jnp.max(lc, axis=0, keepdims=True))
        inv = has_cmp / jnp.sum(ec, axis=0, keepdims=True)
        ocmp_ref[kv] = _dot(vcmpt_ref[0, kv], ec.astype(BF16)) * inv
        imp = _dot_split3(ov_ref[...], ec) * inv
        imp = imp + pltpu.roll(imp, qb, 1)
        imp = imp + pltpu.roll(imp, 2 * qb, 1)
        imp = jnp.where(allowed, jnp.where(forced, jnp.inf, imp), -jnp.inf)
        sel = blk < 0
        for _ in range(NSA_TOP_N):
            best = jnp.max(imp, axis=0, keepdims=True)
            first = jnp.min(jnp.where(imp == best, blk_f, float(n_sel)), axis=0, keepdims=True)
            pick = blk_f == first
            sel = sel | pick
            imp = jnp.where(pick, -jnp.inf, imp)
        bias_ref[kv] = jnp.where(sel & allowed, 0.0, MASK_NEG)
        m_ref[kv] = jnp.full((1, rq), MASK_NEG, F32)
        l_ref[kv] = jnp.zeros((1, rq), F32)
        acc_ref[kv] = jnp.zeros((HEAD_DIM, rq), F32)

    one_ahead(groups, cmp_start, cmp_finish)

    def sel_start(c, kv):
        k0 = pl.multiple_of(c * NSA_SEL_CHUNK, NSA_SEL_CHUNK)
        return _dot_nt(ks_ref[0, pl.ds(k0, NSA_SEL_CHUNK), HEAD_DIM * kv:HEAD_DIM * (kv + 1)], qs_ref[kv])

    def sel_finish(c, kv, st, causal):
        dsl = slice(HEAD_DIM * kv, HEAD_DIM * (kv + 1))
        k0 = c * NSA_SEL_CHUNK
        bias8 = bias_ref[kv, pl.ds(pl.multiple_of(c * blocks_per_chunk, blocks_per_chunk), blocks_per_chunk), :]
        pieces = []
        for jb in range(blocks_per_chunk):
            sj = st[NSA_SEL_BLOCK * jb:NSA_SEL_BLOCK * (jb + 1)] + bias8[jb:jb + 1, :]
            if causal:
                sj = jnp.where(k0 + NSA_SEL_BLOCK * jb + key_in_blk <= qpos, sj, MASK_NEG)
            pieces.append(sj)
        sc = jnp.concatenate(pieces, axis=0)
        m_run = m_ref[kv]
        m_new = jnp.maximum(m_run, jnp.max(sc, axis=0, keepdims=True))
        p = jnp.exp(sc - m_new)
        alpha = jnp.exp(m_run - m_new)
        vt = jnp.concatenate([vst_ref[0, slabs_per_chunk * c + u, dsl, :] for u in range(slabs_per_chunk)],
                             axis=1)
        m_ref[kv] = m_new
        l_ref[kv] = alpha * l_ref[kv] + jnp.sum(p, axis=0, keepdims=True)
        acc_ref[kv] = alpha * acc_ref[kv] + _dot(vt, p.astype(BF16))

    def sel_trip(c, carry):
        one_ahead(groups, functools.partial(sel_start, c),
                  lambda kv, st: sel_finish(c, kv, st, False))
        return carry

    lax.fori_loop(0, last_chunk, sel_trip, 0)

    def tail_start(item):
        kv, is_window = item
        if is_window:
            return _dot_nt(kw_ref[0, pl.ds(wstart, wkeys), HEAD_DIM * kv:HEAD_DIM * (kv + 1)], qs_ref[kv])
        return sel_start(last_chunk, kv)

    def tail_finish(item, logits):
        kv, is_window = item
        if not is_window:
            sel_finish(last_chunk, kv, logits, True)
            return
        dsl = slice(HEAD_DIM * kv, HEAD_DIM * (kv + 1))
        lw = jnp.where(mask_w, logits, MASK_NEG)
        ew = jnp.exp(lw - jnp.max(lw, axis=0, keepdims=True))
        vwt = jnp.concatenate([vwt_ref[0, cw + u, dsl, :] for u in range(NSA_WIN_SLABS)], axis=1)
        o_win = _dot(vwt, ew.astype(BF16)) * (1.0 / jnp.sum(ew, axis=0, keepdims=True))
        o_sel = acc_ref[kv] * (1.0 / l_ref[kv])

        def gate_row(branch):
            return jnp.concatenate(
                [gates_t[3 * (GQA_R * kv + r) + branch:3 * (GQA_R * kv + r) + branch + 1, :] for r in range(GQA_R)],
                axis=1)
        o_t = gate_row(0) * ocmp_ref[kv] + gate_row(1) * o_sel + gate_row(2) * o_win
        o_q = o_t.T
        o_ref[0, :, KV_DIM * kv:KV_DIM * (kv + 1)] = jnp.concatenate(
            [o_q[qb * r:qb * (r + 1)] for r in range(GQA_R)], axis=1).astype(o_ref.dtype)

    one_ahead([(kv, w) for kv in groups for w in (False, True)], tail_start, tail_finish)


def _slab_transpose(x3, col0):
    b, s, _ = x3.shape
    v = x3[:, :, col0:col0 + KV_DIM].reshape(b, s // NSA_VT_CHUNK, NSA_VT_CHUNK, KV_DIM)
    return v.transpose(0, 1, 3, 2)


def nsa_core_t(pa, pb, k_cmp, v_cmp_t, b, s):
    qb = NSA_Q_BLOCK
    qd = N_HEADS * HEAD_DIM
    n_cmp = s // NSA_CMP_STRIDE
    n_sel = s // NSA_SEL_BLOCK
    n_slab = s // NSA_VT_CHUNK
    pa3 = pa.reshape(b, s, -1)
    pb3 = pb.reshape(b, s, -1)
    vst = _slab_transpose(pa3, qd + KV_DIM)
    vwt = _slab_transpose(pa3, qd + 3 * KV_DIM)
    sel_lo = np.arange(n_sel)[:, None] * NSA_SEL_BLOCK
    cmp_lo = np.arange(n_cmp)[None, :] * NSA_CMP_STRIDE
    ov = np.clip(np.minimum(sel_lo + NSA_SEL_BLOCK, cmp_lo + NSA_CMP_LEN) - np.maximum(sel_lo, cmp_lo), 0, None)
    ov = jnp.asarray((ov / NSA_CMP_LEN).astype(np.float32), BF16)
    kcol = qd // KV_DIM
    seqspec = lambda col: pl.BlockSpec((1, s, KV_DIM), lambda i, j: (i, 0, col))
    slabspec = pl.BlockSpec((1, n_slab, KV_DIM, NSA_VT_CHUNK), lambda i, j: (i, 0, 0, 0))
    out = pl.pallas_call(
        functools.partial(_nsa_t_kernel, seq=s),
        grid=(b, s // qb),
        in_specs=[pl.BlockSpec((1, qb, qd), lambda i, j: (i, j, 0)),
                  seqspec(kcol), slabspec, seqspec(kcol + 2), slabspec,
                  pl.BlockSpec((1, KV_HEADS, n_cmp, HEAD_DIM), lambda i, j: (i, 0, 0, 0)),
                  pl.BlockSpec((1, KV_HEADS, HEAD_DIM, n_cmp), lambda i, j: (i, 0, 0, 0)),
                  pl.BlockSpec((1, qb, KV_DIM), lambda i, j: (i, j, 2)),
                  pl.BlockSpec((n_sel, n_cmp), lambda i, j: (0, 0))],
        out_specs=pl.BlockSpec((1, qb, qd), lambda i, j: (i, j, 0)),
        out_shape=jax.ShapeDtypeStruct((b, s, qd), BF16),
        scratch_shapes=[pltpu.VMEM((KV_HEADS, GQA_R * qb, HEAD_DIM), BF16),
                        pltpu.VMEM((KV_HEADS, n_sel, GQA_R * qb), F32),
                        pltpu.VMEM((KV_HEADS, HEAD_DIM, GQA_R * qb), F32),
                        pltpu.VMEM((KV_HEADS, HEAD_DIM, GQA_R * qb), F32),
                        pltpu.VMEM((KV_HEADS, 1, GQA_R * qb), F32),
                        pltpu.VMEM((KV_HEADS, 1, GQA_R * qb), F32)],
        compiler_params=_cparams("parallel", "arbitrary"),
        name="nsa_core",
    )(pa3, pa3, vst, pa3, vwt, k_cmp, v_cmp_t, pb3, ov)
    return out.reshape(b * s, qd)


def _rope_tables(positions):
    half = ROPE_DIM // 2
    inv_freq = ROPE_THETA ** (-jnp.arange(0, ROPE_DIM, 2, dtype=F32) / ROPE_DIM)
    ang = positions.astype(F32).reshape(-1)[:, None] * inv_freq
    cos, sin = jnp.cos(ang), jnp.sin(ang)
    t = cos.shape[0]
    ones = jnp.ones((t, HEAD_DIM - ROPE_DIM), F32)
    zeros = jnp.zeros((t, HEAD_DIM - ROPE_DIM), F32)
    zh = jnp.zeros((t, half), F32)
    c = jnp.concatenate([cos, cos, ones], axis=1)
    sa = jnp.concatenate([-sin, zh, zeros], axis=1)
    sb = jnp.concatenate([zh, sin, zeros], axis=1)
    return tuple(jnp.tile(v, (1, LANES // HEAD_DIM)) for v in (c, sa, sb))


def _pad_cols(w, n):
    return jnp.pad(w, ((0, 0), (0, n - w.shape[1])))


def mamba2_mixer(h, ln_w, w_in, conv_w, conv_b, dt_bias, a_log, d_skip, norm_w, w_out, b, s):
    w = _pad_cols(w_in, SSM_IN_PAD).astype(BF16)
    zx = norm_proj(h, ln_w, w, jnp.zeros((SSM_IN_PAD,), F32), F32)
    y = ssd_core(zx, conv_w, conv_b, dt_bias, a_log, d_skip, norm_w, b, s)
    return out_proj(y, w_out.astype(BF16), jnp.zeros((D_MODEL,), F32), h)


def swa_mixer(h, ln_w, w_qkv, b_qkv, sinks, w_o, b_o, rope, b, s):
    n_rope = (N_HEADS + KV_HEADS) * HEAD_DIM // PROJ_TILE
    qkv = norm_proj(h, ln_w, w_qkv.astype(BF16), b_qkv.astype(F32), BF16, rope, tuple(range(n_rope)))
    o = swa_core(qkv, sinks, b, s)
    return out_proj(o, w_o.astype(BF16), b_o.astype(F32), h)


def nsa_mixer(h, ln_w, w_in, pe_k, k_w1, k_w2, pe_v, v_w1, v_w2, w_o, rope, b, s):
    qd = N_HEADS * HEAD_DIM
    cols = lambda k: w_in[:, qd + KV_DIM * k:qd + KV_DIM * (k + 1)]
    wa = jnp.concatenate([w_in[:, :qd], cols(2), cols(3), cols(4), cols(5)], axis=1).astype(BF16)
    wb = _pad_cols(jnp.concatenate([cols(0), cols(1), w_in[:, qd + 6 * KV_DIM:]], axis=1), 3 * KV_DIM).astype(BF16)
    q_tiles = qd // PROJ_TILE
    pa = norm_proj(h, ln_w, wa, jnp.zeros((wa.shape[1],), F32), BF16, rope,
                   tuple(range(q_tiles)) + (q_tiles, q_tiles + 2))
    pb = norm_proj(h, ln_w, wb, jnp.zeros((wb.shape[1],), F32), F32, rope, (0,))
    pb3 = pb.reshape(b, s, -1)
    k_cmp = compress(pb3[..., 0:KV_DIM], pe_k, k_w1, k_w2, b, s, False)
    v_cmp_t = compress(pb3[..., KV_DIM:2 * KV_DIM], pe_v, v_w1, v_w2, b, s, True)
    o = nsa_core_t(pa, pb, k_cmp, v_cmp_t, b, s)
    return out_proj(o, w_o.astype(BF16), jnp.zeros((D_MODEL,), F32), h)


def kernel(x, positions, ln_ffn1, ffn1_w_in, ffn1_w_out, ln_mix, ln_ffn2, ffn2_w_in, ffn2_w_out, ssm_w_in, ssm_conv_w, ssm_conv_b, ssm_dt_bias, ssm_a_log, ssm_d, ssm_norm_w, ssm_w_out, swa_w_qkv, swa_b_qkv, swa_sinks, swa_w_o, swa_b_o, nsa_w_in, nsa_pe_k, nsa_k_w1, nsa_k_w2, nsa_pe_v, nsa_v_w1, nsa_v_w2, nsa_w_o, final_norm):
    b, s, d = x.shape
    depth = ln_ffn1.shape[0]
    rope = _rope_tables(positions)
    h = x.reshape(b * s, d)
    for i in range(depth):
        kind, inst = i % 3, i // 3
        h = ffn(h, ln_ffn1[i], ffn1_w_in[i].astype(BF16), ffn1_w_out[i].astype(BF16))
        if kind == 0:
            h = mamba2_mixer(h, ln_mix[i], ssm_w_in[inst], ssm_conv_w[inst], ssm_conv_b[inst],
                             ssm_dt_bias[inst], ssm_a_log[inst], ssm_d[inst], ssm_norm_w[inst],
                             ssm_w_out[inst], b, s)
        elif kind == 1:
            h = swa_mixer(h, ln_mix[i], swa_w_qkv[inst], swa_b_qkv[inst], swa_sinks[inst],
                          swa_w_o[inst], swa_b_o[inst], rope, b, s)
        else:
            h = nsa_mixer(h, ln_mix[i], nsa_w_in[inst], nsa_pe_k[inst], nsa_k_w1[inst], nsa_k_w2[inst],
                          nsa_pe_v[inst], nsa_v_w1[inst], nsa_v_w2[inst], nsa_w_o[inst], rope, b, s)
        h = ffn(h, ln_ffn2[i], ffn2_w_in[i].astype(BF16), ffn2_w_out[i].astype(BF16),
                final_norm if i == depth - 1 else None)
    return h.reshape(b, s, d)
```

```python
import functools

import numpy as np
import jax
import jax.numpy as jnp
from jax import lax
from jax.experimental import pallas as pl
from jax.experimental.pallas import tpu as pltpu

F32 = jnp.float32
BF16 = jnp.bfloat16

D_MODEL = 1024
RMS_EPS = 1e-6
D_FF = 2816
HEAD_DIM = 64
ROPE_DIM = HEAD_DIM // 4
ROPE_THETA = 500000.0
ATTN_SCALE = HEAD_DIM ** -0.5
MASK_NEG = -1e30

SSM_D_INNER = 2 * D_MODEL
SSM_HEADS = 32
SSM_GROUPS = 8
SSM_STATE = 128
SSM_CONV = 4
SSM_CHUNK = 128
SSM_CONV_DIM = SSM_D_INNER + 2 * SSM_GROUPS * SSM_STATE
SSM_IN_TILE = 1280
SSM_IN_PAD = 5 * SSM_IN_TILE

N_HEADS = 16
KV_HEADS = 4
GQA_R = N_HEADS // KV_HEADS
KV_DIM = KV_HEADS * HEAD_DIM

SWA_BLOCK = 128

NSA_CMP_LEN = 32
NSA_CMP_STRIDE = 16
NSA_CMP_HIDDEN = 256
NSA_SEL_BLOCK = 64
NSA_TOP_N = 8
NSA_WINDOW = 512
NSA_Q_BLOCK = 64
NSA_SEL_CHUNK = 512
VT_SLAB = 128
NSA_WIN_SLABS = NSA_WINDOW // VT_SLAB + 1

LANES = 128
VMEM_LIMIT = 56 * 1024 * 1024
TOKEN_TILE = 1024
ROW_SPLITS = 4
FF_TILE = 256


def _cparams(*sem):
    return pltpu.CompilerParams(dimension_semantics=sem, vmem_limit_bytes=VMEM_LIMIT)


def _dot(a, b):
    return jnp.dot(a, b, preferred_element_type=F32)


def _dot_nt(a, b):
    return lax.dot_general(a, b, (((1,), (1,)), ((), ())), preferred_element_type=F32)


def _sigmoid(x):
    return 1.0 / (1.0 + jnp.exp(-x))


def _rms(x, w):
    return x * lax.rsqrt(jnp.mean(x * x, axis=-1, keepdims=True) + RMS_EPS) * w


def _one_ahead(items, start, finish):
    items = list(items)
    pending = start(items[0])
    for n, item in enumerate(items):
        nxt = start(items[n + 1]) if n + 1 < len(items) else None
        finish(item, pending)
        pending = nxt


def _slab_transpose(x3, col0):
    b, s, _ = x3.shape
    v = x3[:, :, col0:col0 + KV_DIM].reshape(b, s // VT_SLAB, VT_SLAB, KV_DIM)
    return v.transpose(0, 1, 3, 2)


def _rope_store(o_ref, rows, acc, rope_segs, c, sa, sb):
    for k, roped in enumerate(rope_segs):
        seg = acc[:, LANES * k:LANES * (k + 1)]
        if roped:
            seg = (seg * c + pltpu.roll(seg, LANES - ROPE_DIM // 2, 1) * sa
                   + pltpu.roll(seg, ROPE_DIM // 2, 1) * sb)
        o_ref[rows, LANES * k:LANES * (k + 1)] = seg.astype(o_ref.dtype)


def _norm_proj_kernel(*refs, rope_segs, col_tiles):
    if rope_segs is not None:
        x_ref, lnw_ref, w_ref, b_ref, c_ref, sa_ref, sb_ref, o_ref = refs
    elif col_tiles == 1:
        x_ref, lnw_ref, w_ref, b_ref, o_ref = refs
    else:
        x_ref, lnw_ref, w_ref, b_ref, o_ref, xn_ref = refs
    tm = x_ref.shape[0]
    rt = tm // ROW_SPLITS
    row_tiles = [slice(rt * h, rt * (h + 1)) for h in range(ROW_SPLITS)]

    if col_tiles > 1:
        @pl.when(pl.program_id(1) == 0)
        def _():
            xn_ref[...] = _rms(x_ref[...], lnw_ref[...]).astype(BF16)

    def start(rows):
        xn = xn_ref[rows, :] if col_tiles > 1 else _rms(x_ref[rows, :], lnw_ref[...]).astype(BF16)
        return _dot(xn, w_ref[...]) + b_ref[...]

    def finish(rows, acc):
        if rope_segs is None:
            o_ref[rows, :] = acc.astype(o_ref.dtype)
        else:
            _rope_store(o_ref, rows, acc, rope_segs, c_ref[rows, :], sa_ref[rows, :], sb_ref[rows, :])

    _one_ahead(row_tiles, start, finish)


def norm_proj(h, ln_w, w, bias, out_dtype, tn=None, rope=None, rope_cols=()):
    t, d = h.shape
    n = w.shape[1]
    tm = min(TOKEN_TILE, t)
    tn = n if tn is None else tn
    col_tiles = n // tn
    assert n == col_tiles * tn and t % tm == 0 and tn % LANES == 0
    rope_segs = None
    if rope_cols:
        assert col_tiles == 1
        rope_segs = tuple(any(lo <= LANES * k < hi for lo, hi in rope_cols) for k in range(n // LANES))
    in_specs = [pl.BlockSpec((tm, d), lambda i, j: (i, 0)),
                pl.BlockSpec((1, d), lambda i, j: (0, 0)),
                pl.BlockSpec((d, tn), lambda i, j: (0, j)),
                pl.BlockSpec((1, tn), lambda i, j: (0, j))]
    args = [h, ln_w.reshape(1, d), w, bias.reshape(1, n)]
    if rope_segs is not None:
        in_specs += [pl.BlockSpec((tm, LANES), lambda i, j: (i, 0))] * 3
        args += list(rope)
    return pl.pallas_call(
        functools.partial(_norm_proj_kernel, rope_segs=rope_segs, col_tiles=col_tiles),
        grid=(t // tm, col_tiles),
        in_specs=in_specs,
        out_specs=pl.BlockSpec((tm, tn), lambda i, j: (i, j)),
        out_shape=jax.ShapeDtypeStruct((t, n), out_dtype),
        scratch_shapes=[pltpu.VMEM((tm, d), BF16)] if col_tiles > 1 else [],
        compiler_params=_cparams("parallel", "arbitrary"),
        name="norm_proj",
    )(*args)


def _out_proj_kernel(y_ref, w_ref, b_ref, r_ref, o_ref):
    o_ref[...] = r_ref[...] + _dot(y_ref[...], w_ref[...]) + b_ref[...]


def out_proj(y, w, bias, res):
    t, k = y.shape
    d = w.shape[1]
    tm = min(TOKEN_TILE, t)
    return pl.pallas_call(
        _out_proj_kernel,
        grid=(t // tm,),
        in_specs=[pl.BlockSpec((tm, k), lambda i: (i, 0)),
                  pl.BlockSpec((k, d), lambda i: (0, 0)),
                  pl.BlockSpec((1, d), lambda i: (0, 0)),
                  pl.BlockSpec((tm, d), lambda i: (i, 0))],
        out_specs=pl.BlockSpec((tm, d), lambda i: (i, 0)),
        out_shape=jax.ShapeDtypeStruct((t, d), F32),
        compiler_params=_cparams("parallel"),
        name="out_proj",
    )(y, w, bias.reshape(1, d), res)


def _ffn_kernel(x_ref, lnw_ref, wg_ref, wu_ref, wo_ref, fnw_ref, o_ref, xn_ref, acc_ref, *, final_norm):
    f = pl.program_id(1)
    tm = x_ref.shape[0]
    rt = tm // ROW_SPLITS

    @pl.when(f == 0)
    def _():
        xn_ref[...] = _rms(x_ref[...], lnw_ref[...]).astype(BF16)
        acc_ref[...] = jnp.zeros(acc_ref.shape, F32)

    def start(rows):
        xn = xn_ref[rows, :]
        return _dot(xn, wg_ref[0]), _dot(xn, wu_ref[0])

    def finish(rows, gu):
        g, u = gu
        acc_ref[rows, :] += _dot((g * _sigmoid(g) * u).astype(BF16), wo_ref[...])

    _one_ahead([slice(rt * h, rt * (h + 1)) for h in range(ROW_SPLITS)], start, finish)

    @pl.when(f == pl.num_programs(1) - 1)
    def _():
        y = x_ref[...] + 0.5 * acc_ref[...]
        if final_norm:
            y = _rms(y, fnw_ref[...])
        o_ref[...] = y


def ffn(h, ln_w, w_in, w_out, final_w=None):
    t, d = h.shape
    tm = min(TOKEN_TILE, t)
    tf = FF_TILE
    nf = D_FF // tf
    final_norm = final_w is not None
    fnw = (final_w if final_norm else ln_w).reshape(1, d)
    w_tiles = w_in.astype(BF16).reshape(d, 2 * nf, tf).transpose(1, 0, 2)
    return pl.pallas_call(
        functools.partial(_ffn_kernel, final_norm=final_norm),
        grid=(t // tm, nf),
        in_specs=[pl.BlockSpec((tm, d), lambda i, f: (i, 0)),
                  pl.BlockSpec((1, d), lambda i, f: (0, 0)),
                  pl.BlockSpec((1, d, tf), lambda i, f: (f, 0, 0)),
                  pl.BlockSpec((1, d, tf), lambda i, f: (nf + f, 0, 0)),
                  pl.BlockSpec((tf, d), lambda i, f: (f, 0)),
                  pl.BlockSpec((1, d), lambda i, f: (0, 0))],
        out_specs=pl.BlockSpec((tm, d), lambda i, f: (i, 0)),
        out_shape=jax.ShapeDtypeStruct((t, d), F32),
        scratch_shapes=[pltpu.VMEM((tm, d), BF16), pltpu.VMEM((tm, d), F32)],
        compiler_params=_cparams("parallel", "arbitrary"),
        name="ffn",
    )(h, ln_w.reshape(1, d), w_tiles, w_tiles, w_out.astype(BF16), fnw)


def _expand_heads(mat, g):
    rows = mat.shape[0]
    lane = lax.broadcasted_iota(jnp.int32, (rows, LANES), 1)
    pieces = []
    for p in range(2):
        h0 = GQA_R * g + 2 * p
        a = jnp.broadcast_to(mat[:, h0:h0 + 1], (rows, LANES))
        b = jnp.broadcast_to(mat[:, h0 + 1:h0 + 2], (rows, LANES))
        pieces.append(jnp.where(lane < HEAD_DIM, a, b))
    return jnp.concatenate(pieces, axis=1)


def _ssd_kernel(z_ref, x_ref, bc_ref, dt_ref, cw_ref, cb_ref, dtb_ref, alog_ref, dsk_ref, nw_ref,
                tril_ref, o_ref, state_ref, xbuf_ref):
    q = SSM_CHUNK
    gw = GQA_R * HEAD_DIM
    c = pl.program_id(1)

    @pl.when(c == 0)
    def _():
        state_ref[...] = jnp.zeros(state_ref.shape, F32)
        xbuf_ref[0:8, :] = jnp.zeros((8, SSM_CONV_DIM), F32)

    xbuf_ref[8:8 + q, 0:SSM_D_INNER] = x_ref[0]
    xbuf_ref[8:8 + q, SSM_D_INNER:SSM_CONV_DIM] = bc_ref[0]

    def conv_silu(c0, width):
        acc = cb_ref[:, c0:c0 + width]
        for k in range(SSM_CONV):
            r0 = 8 - (SSM_CONV - 1) + k
            acc = acc + cw_ref[k:k + 1, c0:c0 + width] * xbuf_ref[r0:r0 + q, c0:c0 + width]
        return acc * _sigmoid(acc)

    dtr = dt_ref[0] + dtb_ref[...]
    dt = jnp.maximum(dtr, 0.0) + jnp.log1p(jnp.exp(-jnp.abs(dtr)))
    ad = dt * (-jnp.exp(alog_ref[...]))
    acum = jnp.dot(tril_ref[...], ad, precision=lax.Precision.HIGHEST,
                   preferred_element_type=F32)
    acum_row = acum.T
    a_last = acum[q - 1:q, :]
    exp_acum = jnp.exp(acum)
    decay_st = jnp.exp(a_last - acum)
    chunk_decay = jnp.exp(a_last)
    causal = (lax.broadcasted_iota(jnp.int32, (q, q), 0) >= lax.broadcasted_iota(jnp.int32, (q, q), 1))
    head_of_lane = lax.broadcasted_iota(jnp.int32, (q, 2 * LANES), 1) // HEAD_DIM

    def start(g):
        xs = conv_silu(gw * g, gw)
        bm = conv_silu(SSM_D_INNER + SSM_STATE * g, SSM_STATE)
        cm = conv_silu(SSM_D_INNER + SSM_GROUPS * SSM_STATE + SSM_STATE * g, SSM_STATE)
        xd = xs * _expand_heads(dt, g)
        cbf = cm.astype(BF16)
        st = state_ref[g]
        return xs, bm, xd, st, _dot_nt(cbf, bm.astype(BF16)), _dot(cbf, st.astype(BF16))

    def finish(g, vals):
        xs, bm, xd, st, cb, y_off = vals
        cs = slice(gw * g, gw * (g + 1))
        y = y_off * _expand_heads(exp_acum, g)
        for r in range(GQA_R):
            hh = GQA_R * g + r
            diff = acum[:, hh:hh + 1] - acum_row[hh:hh + 1, :]
            decay = jnp.exp(jnp.where(causal, diff, -jnp.inf))
            xdm = jnp.where(head_of_lane == r, xd, 0.0).astype(BF16)
            y = y + _dot((cb * decay).astype(BF16), xdm)
        state_ref[g] = (st * _expand_heads(chunk_decay, g)
                        + _dot(bm.T.astype(BF16), (xd * _expand_heads(decay_st, g)).astype(BF16)))
        y = y + dsk_ref[:, cs] * xs
        zz = z_ref[0, :, cs]
        gated = y * (zz * _sigmoid(zz))
        o_ref[0, :, cs] = _rms(gated, nw_ref[:, cs]).astype(o_ref.dtype)

    _one_ahead(range(SSM_GROUPS), start, finish)
    xbuf_ref[0:8, :] = xbuf_ref[q:q + 8, :]


def ssd_core(zx, conv_w, conv_b, dt_bias, a_log, d_skip, norm_w, b, s):
    q = SSM_CHUNK
    zx3 = zx.reshape(b, s, SSM_IN_PAD)
    pad = lambda v: jnp.pad(v.astype(F32), (0, LANES - SSM_HEADS)).reshape(1, LANES)
    tril = jnp.asarray(np.tril(np.ones((q, q), np.float32)))
    wide = SSM_D_INNER
    full = lambda shape: pl.BlockSpec(shape, lambda i, c: (0,) * len(shape))
    out = pl.pallas_call(
        _ssd_kernel,
        grid=(b, s // q),
        in_specs=[pl.BlockSpec((1, q, wide), lambda i, c: (i, c, 0)),
                  pl.BlockSpec((1, q, wide), lambda i, c: (i, c, 1)),
                  pl.BlockSpec((1, q, wide), lambda i, c: (i, c, 2)),
                  pl.BlockSpec((1, q, LANES), lambda i, c: (i, c, (SSM_D_INNER + SSM_CONV_DIM) // LANES)),
                  full((SSM_CONV, SSM_CONV_DIM)), full((1, SSM_CONV_DIM)),
                  full((1, LANES)), full((1, LANES)), full((1, wide)), full((1, wide)),
                  full((q, q))],
        out_specs=pl.BlockSpec((1, q, wide), lambda i, c: (i, c, 0)),
        out_shape=jax.ShapeDtypeStruct((b, s, wide), BF16),
        scratch_shapes=[pltpu.VMEM((SSM_GROUPS, SSM_STATE, GQA_R * HEAD_DIM), F32),
                        pltpu.VMEM((q + 8, SSM_CONV_DIM), F32)],
        compiler_params=_cparams("parallel", "arbitrary"),
        name="ssd_core",
    )(zx3, zx3, zx3, zx3, conv_w.astype(F32), conv_b.reshape(1, -1).astype(F32),
      pad(dt_bias), pad(a_log), jnp.repeat(d_skip.astype(F32), HEAD_DIM).reshape(1, wide),
      norm_w.reshape(1, wide).astype(F32), tril)
    return out.reshape(b * s, wide)


def _swa_kernel(q_ref, kc_ref, kp_ref, vtc_ref, vtp_ref, sink_ref, o_ref):
    i = pl.program_id(1)
    w = SWA_BLOCK
    rq = GQA_R * w
    lane = lax.broadcasted_iota(jnp.int32, (1, rq), 1)
    qloc = lane & (w - 1)
    head_of_lane = lane // w
    kk = lax.broadcasted_iota(jnp.int32, (2 * w, rq), 0)
    valid = ((kk < w) & (kk > qloc) & (i > 0)) | ((kk >= w) & ((kk - w) <= qloc))
    scale = jnp.asarray(ATTN_SCALE, BF16)

    def start(kv):
        dsl = slice(HEAD_DIM * kv, HEAD_DIM * (kv + 1))
        qs = jnp.concatenate(
            [q_ref[0, :, HEAD_DIM * (GQA_R * kv + r):HEAD_DIM * (GQA_R * kv + r + 1)] for r in range(GQA_R)],
            axis=0) * scale
        k = jnp.concatenate([kp_ref[0, :, dsl], kc_ref[0, :, dsl]], axis=0)
        return _dot_nt(k, qs)

    def finish(kv, st):
        dsl = slice(HEAD_DIM * kv, HEAD_DIM * (kv + 1))
        sink = jnp.zeros((1, rq), F32)
        for r in range(GQA_R):
            h = GQA_R * kv + r
            sink = jnp.where(head_of_lane == r, sink_ref[:, h:h + 1], sink)
        sc = jnp.where(valid, st, -jnp.inf)
        m = jnp.maximum(jnp.max(sc, axis=0, keepdims=True), sink)
        e = jnp.exp(sc - m)
        den = jnp.sum(e, axis=0, keepdims=True) + jnp.exp(sink - m)
        vt = jnp.concatenate([vtp_ref[0, 0, dsl, :], vtc_ref[0, 0, dsl, :]], axis=1)
        o_q = (_dot(vt, e.astype(BF16)) * (1.0 / den)).T
        o_ref[0, :, KV_DIM * kv:KV_DIM * (kv + 1)] = jnp.concatenate(
            [o_q[w * r:w * (r + 1)] for r in range(GQA_R)], axis=1).astype(o_ref.dtype)

    _one_ahead(range(KV_HEADS), start, finish)


def swa_core(qkv, sinks, b, s):
    w = SWA_BLOCK
    qkv3 = qkv.reshape(b, s, -1)
    qd = N_HEADS * HEAD_DIM
    kcol = qd // KV_DIM
    vt = _slab_transpose(qkv3, qd + KV_DIM)
    prev = lambda i, j: (i, jnp.maximum(j - 1, 0))
    out = pl.pallas_call(
        _swa_kernel,
        grid=(b, s // w),
        in_specs=[pl.BlockSpec((1, w, qd), lambda i, j: (i, j, 0)),
                  pl.BlockSpec((1, w, KV_DIM), lambda i, j: (i, j, kcol)),
                  pl.BlockSpec((1, w, KV_DIM), lambda i, j: prev(i, j) + (kcol,)),
                  pl.BlockSpec((1, 1, KV_DIM, VT_SLAB), lambda i, j: (i, j, 0, 0)),
                  pl.BlockSpec((1, 1, KV_DIM, VT_SLAB), lambda i, j: prev(i, j) + (0, 0)),
                  pl.BlockSpec((1, LANES), lambda i, j: (0, 0))],
        out_specs=pl.BlockSpec((1, w, qd), lambda i, j: (i, j, 0)),
        out_shape=jax.ShapeDtypeStruct((b, s, qd), BF16),
        compiler_params=_cparams("parallel", "arbitrary"),
        name="swa_core",
    )(qkv3, qkv3, qkv3, vt, vt, jnp.pad(sinks.astype(F32), (0, LANES - N_HEADS)).reshape(1, LANES))
    return out.reshape(b * s, qd)


def _compress_kernel(f_ref, pe_ref, w1_ref, w2_ref, o_ref, *, transposed):
    half = NSA_CMP_STRIDE * HEAD_DIM
    x = f_ref[0, 0]
    za = (x + pe_ref[:, 0:half]).astype(BF16)
    zb = (x + pe_ref[:, half:2 * half]).astype(BF16)
    hb = _dot(zb, w1_ref[half:2 * half, :])
    hid = _dot(za, w1_ref[0:half, :]) + pltpu.roll(hb, hb.shape[0] - 1, 0)
    act = (hid * _sigmoid(hid)).astype(BF16)
    if transposed:
        o_ref[0, 0] = _dot_nt(w2_ref[...], act).astype(o_ref.dtype)
    else:
        o_ref[0, 0] = _dot(act, w2_ref[...]).astype(o_ref.dtype)


def compress(tok, pe, w1, w2, b, s, transposed):
    nch = s // NSA_CMP_STRIDE
    half = NSA_CMP_STRIDE * HEAD_DIM
    f = tok.reshape(b, nch, NSA_CMP_STRIDE, KV_HEADS, HEAD_DIM).transpose(0, 3, 1, 2, 4).reshape(b, KV_HEADS, nch, half)
    w2 = (w2.T if transposed else w2).astype(BF16)
    oshape = (HEAD_DIM, nch) if transposed else (nch, HEAD_DIM)
    return pl.pallas_call(
        functools.partial(_compress_kernel, transposed=transposed),
        grid=(b, KV_HEADS),
        in_specs=[pl.BlockSpec((1, 1, nch, half), lambda i, j: (i, j, 0, 0)),
                  pl.BlockSpec((1, 2 * half), lambda i, j: (0, 0)),
                  pl.BlockSpec((2 * half, NSA_CMP_HIDDEN), lambda i, j: (0, 0)),
                  pl.BlockSpec(w2.shape, lambda i, j: (0, 0))],
        out_specs=pl.BlockSpec((1, 1) + oshape, lambda i, j: (i, j, 0, 0)),
        out_shape=jax.ShapeDtypeStruct((b, KV_HEADS) + oshape, BF16),
        compiler_params=_cparams("parallel", "parallel"),
        name="nsa_compress",
    )(f, pe.reshape(1, 2 * half).astype(F32), w1.astype(BF16), w2)


def _dot_split3(a_bf16, x):
    hi = x.astype(BF16)
    r1 = x - hi.astype(F32)
    mid = r1.astype(BF16)
    lo = (r1 - mid.astype(F32)).astype(BF16)
    return _dot(a_bf16, hi) + _dot(a_bf16, mid) + _dot(a_bf16, lo)


def _nsa_kernel(q_ref, ks_ref, vst_ref, kw_ref, vwt_ref, kcmp_ref, vcmpt_ref, g_ref, ov_ref,
                o_ref, qs_ref, bias_ref, ocmp_ref, acc_ref, m_ref, l_ref, *, seq):
    i = pl.program_id(1)
    qb = NSA_Q_BLOCK
    rq = GQA_R * qb
    n_cmp = seq // NSA_CMP_STRIDE
    n_sel = seq // NSA_SEL_BLOCK
    blocks_per_chunk = NSA_SEL_CHUNK // NSA_SEL_BLOCK
    slabs_per_chunk = NSA_SEL_CHUNK // VT_SLAB
    t0 = i * qb
    qpos = t0 + (lax.broadcasted_iota(jnp.int32, (1, rq), 1) & (qb - 1))
    gates_t = _sigmoid(g_ref[0]).T

    cmp_end = lax.broadcasted_iota(jnp.int32, (n_cmp, rq), 0) * NSA_CMP_STRIDE + (NSA_CMP_LEN - 1)
    mask_c = cmp_end <= qpos
    has_cmp = jnp.where(qpos >= NSA_CMP_LEN - 1, 1.0, 0.0)
    blk = lax.broadcasted_iota(jnp.int32, (n_sel, rq), 0)
    blk_f = blk.astype(F32)
    forced = (blk == i) | (blk == 0)
    allowed = blk <= i
    cw = jnp.maximum(t0 - NSA_WINDOW, 0) // VT_SLAB
    wkeys = NSA_WIN_SLABS * VT_SLAB
    wstart = pl.multiple_of(cw * VT_SLAB, VT_SLAB)
    delta_w = qpos - (wstart + lax.broadcasted_iota(jnp.int32, (wkeys, rq), 0))
    mask_w = (delta_w >= 0) & (delta_w < NSA_WINDOW)
    last_chunk = i // blocks_per_chunk
    key_in_blk = lax.broadcasted_iota(jnp.int32, (NSA_SEL_BLOCK, rq), 0)
    groups = range(KV_HEADS)

    def cmp_start(kv):
        qs = jnp.concatenate(
            [q_ref[0, :, HEAD_DIM * (GQA_R * kv + r):HEAD_DIM * (GQA_R * kv + r + 1)] for r in range(GQA_R)],
            axis=0)
        qs = qs * jnp.asarray(ATTN_SCALE, BF16)
        qs_ref[kv] = qs
        return _dot_nt(kcmp_ref[0, kv], qs)

    def cmp_finish(kv, logits):
        lc = jnp.where(mask_c, logits, MASK_NEG)
        ec = jnp.exp(lc - jnp.max(lc, axis=0, keepdims=True))
        inv = has_cmp / jnp.sum(ec, axis=0, keepdims=True)
        ocmp_ref[kv] = _dot(vcmpt_ref[0, kv], ec.astype(BF16)) * inv
        imp = _dot_split3(ov_ref[...], ec) * inv
        imp = imp + pltpu.roll(imp, qb, 1)
        imp = imp + pltpu.roll(imp, 2 * qb, 1)
        imp = jnp.where(allowed, jnp.where(forced, jnp.inf, imp), -jnp.inf)
        sel = blk < 0
        for _ in range(NSA_TOP_N):
            best = jnp.max(imp, axis=0, keepdims=True)
            first = jnp.min(jnp.where(imp == best, blk_f, float(n_sel)), axis=0, keepdims=True)
            pick = blk_f == first
            sel = sel | pick
            imp = jnp.where(pick, -jnp.inf, imp)
        bias_ref[kv] = jnp.where(sel & allowed, 0.0, MASK_NEG)
        m_ref[kv] = jnp.full((1, rq), MASK_NEG, F32)
        l_ref[kv] = jnp.zeros((1, rq), F32)
        acc_ref[kv] = jnp.zeros((HEAD_DIM, rq), F32)

    _one_ahead(groups, cmp_start, cmp_finish)

    def sel_start(c, kv):
        k0 = pl.multiple_of(c * NSA_SEL_CHUNK, NSA_SEL_CHUNK)
        return _dot_nt(ks_ref[0, pl.ds(k0, NSA_SEL_CHUNK), HEAD_DIM * kv:HEAD_DIM * (kv + 1)], qs_ref[kv])

    def sel_finish(c, kv, st, causal):
        dsl = slice(HEAD_DIM * kv, HEAD_DIM * (kv + 1))
        k0 = c * NSA_SEL_CHUNK
        bias8 = bias_ref[kv, pl.ds(pl.multiple_of(c * blocks_per_chunk, blocks_per_chunk), blocks_per_chunk), :]
        pieces = []
        for jb in range(blocks_per_chunk):
            sj = st[NSA_SEL_BLOCK * jb:NSA_SEL_BLOCK * (jb + 1)] + bias8[jb:jb + 1, :]
            if causal:
                sj = jnp.where(k0 + NSA_SEL_BLOCK * jb + key_in_blk <= qpos, sj, MASK_NEG)
            pieces.append(sj)
        sc = jnp.concatenate(pieces, axis=0)
        m_run = m_ref[kv]
        m_new = jnp.maximum(m_run, jnp.max(sc, axis=0, keepdims=True))
        p = jnp.exp(sc - m_new)
        alpha = jnp.exp(m_run - m_new)
        vt = jnp.concatenate([vst_ref[0, slabs_per_chunk * c + u, dsl, :] for u in range(slabs_per_chunk)],
                             axis=1)
        m_ref[kv] = m_new
        l_ref[kv] = alpha * l_ref[kv] + jnp.sum(p, axis=0, keepdims=True)
        acc_ref[kv] = alpha * acc_ref[kv] + _dot(vt, p.astype(BF16))

    def sel_trip(c, carry):
        _one_ahead(groups, functools.partial(sel_start, c), lambda kv, st: sel_finish(c, kv, st, False))
        return carry

    lax.fori_loop(0, last_chunk, sel_trip, 0)

    def tail_start(item):
        kv, is_window = item
        if is_window:
            return _dot_nt(kw_ref[0, pl.ds(wstart, wkeys), HEAD_DIM * kv:HEAD_DIM * (kv + 1)], qs_ref[kv])
        return sel_start(last_chunk, kv)

    def tail_finish(item, logits):
        kv, is_window = item
        if not is_window:
            sel_finish(last_chunk, kv, logits, True)
            return
        dsl = slice(HEAD_DIM * kv, HEAD_DIM * (kv + 1))
        lw = jnp.where(mask_w, logits, MASK_NEG)
        ew = jnp.exp(lw - jnp.max(lw, axis=0, keepdims=True))
        vwt = jnp.concatenate([vwt_ref[0, cw + u, dsl, :] for u in range(NSA_WIN_SLABS)], axis=1)
        o_win = _dot(vwt, ew.astype(BF16)) * (1.0 / jnp.sum(ew, axis=0, keepdims=True))
        o_sel = acc_ref[kv] * (1.0 / l_ref[kv])

        def gate_row(branch):
            return jnp.concatenate(
                [gates_t[3 * (GQA_R * kv + r) + branch:3 * (GQA_R * kv + r) + branch + 1, :] for r in range(GQA_R)],
                axis=1)
        o_t = gate_row(0) * ocmp_ref[kv] + gate_row(1) * o_sel + gate_row(2) * o_win
        o_q = o_t.T
        o_ref[0, :, KV_DIM * kv:KV_DIM * (kv + 1)] = jnp.concatenate(
            [o_q[qb * r:qb * (r + 1)] for r in range(GQA_R)], axis=1).astype(o_ref.dtype)

    _one_ahead([(kv, w) for kv in groups for w in (False, True)], tail_start, tail_finish)


def nsa_core(pa, pb, k_cmp, v_cmp_t, b, s):
    qb = NSA_Q_BLOCK
    qd = N_HEADS * HEAD_DIM
    n_cmp = s // NSA_CMP_STRIDE
    n_sel = s // NSA_SEL_BLOCK
    n_slab = s // VT_SLAB
    pa3 = pa.reshape(b, s, -1)
    pb3 = pb.reshape(b, s, -1)
    vst = _slab_transpose(pa3, qd + 2 * KV_DIM)
    vwt = _slab_transpose(pa3, qd + 3 * KV_DIM)
    sel_lo = np.arange(n_sel)[:, None] * NSA_SEL_BLOCK
    cmp_lo = np.arange(n_cmp)[None, :] * NSA_CMP_STRIDE
    ov = np.clip(np.minimum(sel_lo + NSA_SEL_BLOCK, cmp_lo + NSA_CMP_LEN) - np.maximum(sel_lo, cmp_lo), 0, None)
    ov = jnp.asarray((ov / NSA_CMP_LEN).astype(np.float32), BF16)
    kcol = qd // KV_DIM
    seqspec = lambda col: pl.BlockSpec((1, s, KV_DIM), lambda i, j: (i, 0, col))
    slabspec = pl.BlockSpec((1, n_slab, KV_DIM, VT_SLAB), lambda i, j: (i, 0, 0, 0))
    rq = GQA_R * qb
    out = pl.pallas_call(
        functools.partial(_nsa_kernel, seq=s),
        grid=(b, s // qb),
        in_specs=[pl.BlockSpec((1, qb, qd), lambda i, j: (i, j, 0)),
                  seqspec(kcol), slabspec, seqspec(kcol + 1), slabspec,
                  pl.BlockSpec((1, KV_HEADS, n_cmp, HEAD_DIM), lambda i, j: (i, 0, 0, 0)),
                  pl.BlockSpec((1, KV_HEADS, HEAD_DIM, n_cmp), lambda i, j: (i, 0, 0, 0)),
                  pl.BlockSpec((1, qb, KV_DIM), lambda i, j: (i, j, 2)),
                  pl.BlockSpec((n_sel, n_cmp), lambda i, j: (0, 0))],
        out_specs=pl.BlockSpec((1, qb, qd), lambda i, j: (i, j, 0)),
        out_shape=jax.ShapeDtypeStruct((b, s, qd), BF16),
        scratch_shapes=[pltpu.VMEM((KV_HEADS, rq, HEAD_DIM), BF16),
                        pltpu.VMEM((KV_HEADS, n_sel, rq), F32),
                        pltpu.VMEM((KV_HEADS, HEAD_DIM, rq), F32),
                        pltpu.VMEM((KV_HEADS, HEAD_DIM, rq), F32),
                        pltpu.VMEM((KV_HEADS, 1, rq), F32),
                        pltpu.VMEM((KV_HEADS, 1, rq), F32)],
        compiler_params=_cparams("parallel", "arbitrary"),
        name="nsa_core",
    )(pa3, pa3, vst, pa3, vwt, k_cmp, v_cmp_t, pb3, ov)
    return out.reshape(b * s, qd)


def _rope_tables(positions):
    half = ROPE_DIM // 2
    inv_freq = ROPE_THETA ** (-jnp.arange(0, ROPE_DIM, 2, dtype=F32) / ROPE_DIM)
    ang = positions.astype(F32).reshape(-1)[:, None] * inv_freq
    cos, sin = jnp.cos(ang), jnp.sin(ang)
    t = cos.shape[0]
    ones = jnp.ones((t, HEAD_DIM - ROPE_DIM), F32)
    zeros = jnp.zeros((t, HEAD_DIM - ROPE_DIM), F32)
    zh = jnp.zeros((t, half), F32)
    c = jnp.concatenate([cos, cos, ones], axis=1)
    sa = jnp.concatenate([-sin, zh, zeros], axis=1)
    sb = jnp.concatenate([zh, sin, zeros], axis=1)
    return tuple(jnp.tile(v, (1, LANES // HEAD_DIM)) for v in (c, sa, sb))


def _pad_cols(w, n):
    return jnp.pad(w, ((0, 0), (0, n - w.shape[1])))


def mamba2_mixer(h, ln_w, w_in, conv_w, conv_b, dt_bias, a_log, d_skip, norm_w, w_out, b, s):
    w = _pad_cols(w_in, SSM_IN_PAD).astype(BF16)
    zx = norm_proj(h, ln_w, w, jnp.zeros((SSM_IN_PAD,), F32), F32, tn=SSM_IN_TILE)
    y = ssd_core(zx, conv_w, conv_b, dt_bias, a_log, d_skip, norm_w, b, s)
    return out_proj(y, w_out.astype(BF16), jnp.zeros((D_MODEL,), F32), h)


def swa_mixer(h, ln_w, w_qkv, b_qkv, sinks, w_o, b_o, rope, b, s):
    qk = (N_HEADS + KV_HEADS) * HEAD_DIM
    qkv = norm_proj(h, ln_w, w_qkv.astype(BF16), b_qkv.astype(F32), BF16, rope=rope, rope_cols=((0, qk),))
    o = swa_core(qkv, sinks, b, s)
    return out_proj(o, w_o.astype(BF16), b_o.astype(F32), h)


def nsa_mixer(h, ln_w, w_in, pe_k, k_w1, k_w2, pe_v, v_w1, v_w2, w_o, rope, b, s):
    qd = N_HEADS * HEAD_DIM
    cols = lambda k: w_in[:, qd + KV_DIM * k:qd + KV_DIM * (k + 1)]
    wa = jnp.concatenate([w_in[:, :qd], cols(2), cols(4), cols(3), cols(5)], axis=1).astype(BF16)
    wb = _pad_cols(jnp.concatenate([cols(0), cols(1), w_in[:, qd + 6 * KV_DIM:]], axis=1), 3 * KV_DIM).astype(BF16)
    pa = norm_proj(h, ln_w, wa, jnp.zeros((wa.shape[1],), F32), BF16, rope=rope,
                   rope_cols=((0, qd + 2 * KV_DIM),))
    pb = norm_proj(h, ln_w, wb, jnp.zeros((wb.shape[1],), F32), F32, rope=rope, rope_cols=((0, KV_DIM),))
    pb3 = pb.reshape(b, s, -1)
    k_cmp = compress(pb3[..., 0:KV_DIM], pe_k, k_w1, k_w2, b, s, False)
    v_cmp_t = compress(pb3[..., KV_DIM:2 * KV_DIM], pe_v, v_w1, v_w2, b, s, True)
    o = nsa_core(pa, pb, k_cmp, v_cmp_t, b, s)
    return out_proj(o, w_o.astype(BF16), jnp.zeros((D_MODEL,), F32), h)


def kernel(x, positions, ln_ffn1, ffn1_w_in, ffn1_w_out, ln_mix, ln_ffn2, ffn2_w_in, ffn2_w_out, ssm_w_in, ssm_conv_w, ssm_conv_b, ssm_dt_bias, ssm_a_log, ssm_d, ssm_norm_w, ssm_w_out, swa_w_qkv, swa_b_qkv, swa_sinks, swa_w_o, swa_b_o, nsa_w_in, nsa_pe_k, nsa_k_w1, nsa_k_w2, nsa_pe_v, nsa_v_w1, nsa_v_w2, nsa_w_o, final_norm):
    b, s, d = x.shape
    depth = ln_ffn1.shape[0]
    rope = _rope_tables(positions)
    h = x.reshape(b * s, d)
    for i in range(depth):
        kind, inst = i % 3, i // 3
        h = ffn(h, ln_ffn1[i], ffn1_w_in[i], ffn1_w_out[i])
        if kind == 0:
            h = mamba2_mixer(h, ln_mix[i], ssm_w_in[inst], ssm_conv_w[inst], ssm_conv_b[inst],
                             ssm_dt_bias[inst], ssm_a_log[inst], ssm_d[inst], ssm_norm_w[inst],
                             ssm_w_out[inst], b, s)
        elif kind == 1:
            h = swa_mixer(h, ln_mix[i], swa_w_qkv[inst], swa_b_qkv[inst], swa_sinks[inst],
                          swa_w_o[inst], swa_b_o[inst], rope, b, s)
        else:
            h = nsa_mixer(h, ln_mix[i], nsa_w_in[inst], nsa_pe_k[inst], nsa_k_w1[inst], nsa_k_w2[inst],
                          nsa_pe_v[inst], nsa_v_w1[inst], nsa_v_w2[inst], nsa_w_o[inst], rope, b, s)
        h = ffn(h, ln_ffn2[i], ffn2_w_in[i], ffn2_w_out[i], final_norm if i == depth - 1 else None)
    return h.reshape(b, s, d)
```

```python
import functools
import math

import numpy as np
import jax
import jax.numpy as jnp
from jax import lax
from jax.experimental import pallas as pl
from jax.experimental.pallas import tpu as pltpu

F32 = jnp.float32
BF16 = jnp.bfloat16

D_MODEL = 1024
RMS_EPS = 1e-6
D_FF = 2816
HEAD_DIM = 64
ROPE_DIM = HEAD_DIM // 4
ROPE_THETA = 500000.0
ATTN_SCALE = HEAD_DIM ** -0.5
MASK_NEG = -1e30

SSM_D_INNER = 2 * D_MODEL
SSM_HEADS = 32
SSM_GROUPS = 8
SSM_STATE = 128
SSM_CONV = 4
SSM_CHUNK = 128
SSM_CONV_DIM = SSM_D_INNER + 2 * SSM_GROUPS * SSM_STATE
SSM_IN_TILE = 1280
SSM_IN_PAD = 5 * SSM_IN_TILE

N_HEADS = 16
KV_HEADS = 4
GQA_R = N_HEADS // KV_HEADS
KV_DIM = KV_HEADS * HEAD_DIM

SWA_BLOCK = 128

NSA_CMP_LEN = 32
NSA_CMP_STRIDE = 16
NSA_CMP_HIDDEN = 256
NSA_SEL_BLOCK = 64
NSA_TOP_N = 8
NSA_WINDOW = 512
NSA_Q_BLOCK = 64
NSA_SEL_CHUNK = 512
VT_SLAB = 128
NSA_WIN_SLABS = NSA_WINDOW // VT_SLAB + 1

LANES = 128
VMEM_LIMIT = 56 * 1024 * 1024
TOKEN_TILE = 1024
ROW_SPLITS = 4
FF_TILE = 256


def _cparams(*sem):
    return pltpu.CompilerParams(dimension_semantics=sem, vmem_limit_bytes=VMEM_LIMIT)


def _dot(a, b):
    return jnp.dot(a, b, preferred_element_type=F32)


def _dot_nt(a, b):
    return lax.dot_general(a, b, (((1,), (1,)), ((), ())), preferred_element_type=F32)


def _sigmoid(x):
    return 1.0 / (1.0 + jnp.exp(-x))


def _rms(x, w):
    return x * lax.rsqrt(jnp.mean(x * x, axis=-1, keepdims=True) + RMS_EPS) * w


def _one_ahead(items, start, finish):
    items = list(items)
    pending = start(items[0])
    for n, item in enumerate(items):
        nxt = start(items[n + 1]) if n + 1 < len(items) else None
        finish(item, pending)
        pending = nxt


def _slab_transpose(x3, col0):
    b, s, _ = x3.shape
    v = x3[:, :, col0:col0 + KV_DIM].reshape(b, s // VT_SLAB, VT_SLAB, KV_DIM)
    return v.transpose(0, 1, 3, 2)


def _rope_store(o_ref, rows, acc, rope_segs, c, sa, sb):
    for k, roped in enumerate(rope_segs):
        seg = acc[:, LANES * k:LANES * (k + 1)]
        if roped:
            seg = (seg * c + pltpu.roll(seg, LANES - ROPE_DIM // 2, 1) * sa
                   + pltpu.roll(seg, ROPE_DIM // 2, 1) * sb)
        o_ref[rows, LANES * k:LANES * (k + 1)] = seg.astype(o_ref.dtype)


def _norm_proj_kernel(*refs, rope_segs, col_tiles):
    if rope_segs is not None:
        x_ref, lnw_ref, w_ref, b_ref, c_ref, sa_ref, sb_ref, o_ref = refs
    elif col_tiles == 1:
        x_ref, lnw_ref, w_ref, b_ref, o_ref = refs
    else:
        x_ref, lnw_ref, w_ref, b_ref, o_ref, xn_ref = refs
    tm = x_ref.shape[0]
    rt = tm // ROW_SPLITS
    row_tiles = [slice(rt * h, rt * (h + 1)) for h in range(ROW_SPLITS)]

    if col_tiles > 1:
        @pl.when(pl.program_id(1) == 0)
        def _():
            xn_ref[...] = _rms(x_ref[...], lnw_ref[...]).astype(BF16)

    def start(rows):
        xn = xn_ref[rows, :] if col_tiles > 1 else _rms(x_ref[rows, :], lnw_ref[...]).astype(BF16)
        return _dot(xn, w_ref[...]) + b_ref[...]

    def finish(rows, acc):
        if rope_segs is None:
            o_ref[rows, :] = acc.astype(o_ref.dtype)
        else:
            _rope_store(o_ref, rows, acc, rope_segs, c_ref[rows, :], sa_ref[rows, :], sb_ref[rows, :])

    _one_ahead(row_tiles, start, finish)


def norm_proj(h, ln_w, w, bias, out_dtype, tn=None, rope=None, rope_cols=()):
    t, d = h.shape
    n = w.shape[1]
    tm = min(TOKEN_TILE, t)
    tn = n if tn is None else tn
    col_tiles = n // tn
    assert n == col_tiles * tn and t % tm == 0 and tn % LANES == 0
    rope_segs = None
    if rope_cols:
        assert col_tiles == 1
        rope_segs = tuple(any(lo <= LANES * k < hi for lo, hi in rope_cols) for k in range(n // LANES))
    in_specs = [pl.BlockSpec((tm, d), lambda i, j: (i, 0)),
                pl.BlockSpec((1, d), lambda i, j: (0, 0)),
                pl.BlockSpec((d, tn), lambda i, j: (0, j)),
                pl.BlockSpec((1, tn), lambda i, j: (0, j))]
    args = [h, ln_w.reshape(1, d), w, bias.reshape(1, n)]
    if rope_segs is not None:
        in_specs += [pl.BlockSpec((tm, LANES), lambda i, j: (i, 0))] * 3
        args += list(rope)
    return pl.pallas_call(
        functools.partial(_norm_proj_kernel, rope_segs=rope_segs, col_tiles=col_tiles),
        grid=(t // tm, col_tiles),
        in_specs=in_specs,
        out_specs=pl.BlockSpec((tm, tn), lambda i, j: (i, j)),
        out_shape=jax.ShapeDtypeStruct((t, n), out_dtype),
        scratch_shapes=[pltpu.VMEM((tm, d), BF16)] if col_tiles > 1 else [],
        compiler_params=_cparams("parallel", "arbitrary"),
        name="norm_proj",
    )(*args)


def _out_proj_kernel(y_ref, w_ref, b_ref, r_ref, o_ref):
    o_ref[...] = r_ref[...] + _dot(y_ref[...], w_ref[...]) + b_ref[...]


def out_proj(y, w, bias, res):
    t, k = y.shape
    d = w.shape[1]
    tm = min(TOKEN_TILE, t)
    return pl.pallas_call(
        _out_proj_kernel,
        grid=(t // tm,),
        in_specs=[pl.BlockSpec((tm, k), lambda i: (i, 0)),
                  pl.BlockSpec((k, d), lambda i: (0, 0)),
                  pl.BlockSpec((1, d), lambda i: (0, 0)),
                  pl.BlockSpec((tm, d), lambda i: (i, 0))],
        out_specs=pl.BlockSpec((tm, d), lambda i: (i, 0)),
        out_shape=jax.ShapeDtypeStruct((t, d), F32),
        compiler_params=_cparams("parallel"),
        name="out_proj",
    )(y, w, bias.reshape(1, d), res)


def _ffn_kernel(x_ref, lnw_ref, wg_ref, wu_ref, wo_ref, fnw_ref, o_ref, xn_ref, acc_ref, *, final_norm):
    f = pl.program_id(1)
    tm = x_ref.shape[0]
    rt = tm // ROW_SPLITS

    @pl.when(f == 0)
    def _():
        xn_ref[...] = _rms(x_ref[...], lnw_ref[...]).astype(BF16)
        acc_ref[...] = jnp.zeros(acc_ref.shape, F32)

    def start(rows):
        xn = xn_ref[rows, :]
        return _dot(xn, wg_ref[0]), _dot(xn, wu_ref[0])

    def finish(rows, gu):
        g, u = gu
        acc_ref[rows, :] += _dot((g * _sigmoid(g) * u).astype(BF16), wo_ref[...])

    _one_ahead([slice(rt * h, rt * (h + 1)) for h in range(ROW_SPLITS)], start, finish)

    @pl.when(f == pl.num_programs(1) - 1)
    def _():
        y = x_ref[...] + 0.5 * acc_ref[...]
        if final_norm:
            y = _rms(y, fnw_ref[...])
        o_ref[...] = y


def ffn(h, ln_w, w_in, w_out, final_w=None):
    t, d = h.shape
    tm = min(TOKEN_TILE, t)
    tf = FF_TILE
    nf = D_FF // tf
    final_norm = final_w is not None
    fnw = (final_w if final_norm else ln_w).reshape(1, d)
    w_tiles = w_in.astype(BF16).reshape(d, 2 * nf, tf).transpose(1, 0, 2)
    return pl.pallas_call(
        functools.partial(_ffn_kernel, final_norm=final_norm),
        grid=(t // tm, nf),
        in_specs=[pl.BlockSpec((tm, d), lambda i, f: (i, 0)),
                  pl.BlockSpec((1, d), lambda i, f: (0, 0)),
                  pl.BlockSpec((1, d, tf), lambda i, f: (f, 0, 0)),
                  pl.BlockSpec((1, d, tf), lambda i, f: (nf + f, 0, 0)),
                  pl.BlockSpec((tf, d), lambda i, f: (f, 0)),
                  pl.BlockSpec((1, d), lambda i, f: (0, 0))],
        out_specs=pl.BlockSpec((tm, d), lambda i, f: (i, 0)),
        out_shape=jax.ShapeDtypeStruct((t, d), F32),
        scratch_shapes=[pltpu.VMEM((tm, d), BF16), pltpu.VMEM((tm, d), F32)],
        compiler_params=_cparams("parallel", "arbitrary"),
        name="ffn",
    )(h, ln_w.reshape(1, d), w_tiles, w_tiles, w_out.astype(BF16), fnw)


def _expand_heads(mat, g):
    rows = mat.shape[0]
    lane = lax.broadcasted_iota(jnp.int32, (rows, LANES), 1)
    pieces = []
    for p in range(2):
        h0 = GQA_R * g + 2 * p
        a = jnp.broadcast_to(mat[:, h0:h0 + 1], (rows, LANES))
        b = jnp.broadcast_to(mat[:, h0 + 1:h0 + 2], (rows, LANES))
        pieces.append(jnp.where(lane < HEAD_DIM, a, b))
    return jnp.concatenate(pieces, axis=1)


def _ssd_kernel(z_ref, x_ref, bc_ref, dt_ref, cw_ref, cb_ref, dtb_ref, alog_ref, dsk_ref, nw_ref,
                tril_ref, o_ref, state_ref, xbuf_ref):
    q = SSM_CHUNK
    gw = GQA_R * HEAD_DIM
    c = pl.program_id(1)

    @pl.when(c == 0)
    def _():
        state_ref[...] = jnp.zeros(state_ref.shape, F32)
        xbuf_ref[0:8, :] = jnp.zeros((8, SSM_CONV_DIM), F32)

    xbuf_ref[8:8 + q, 0:SSM_D_INNER] = x_ref[0]
    xbuf_ref[8:8 + q, SSM_D_INNER:SSM_CONV_DIM] = bc_ref[0]

    def conv_silu(c0, width):
        acc = cb_ref[:, c0:c0 + width]
        for k in reversed(range(SSM_CONV)):
            r0 = 8 - (SSM_CONV - 1) + k
            acc = acc + cw_ref[k:k + 1, c0:c0 + width] * xbuf_ref[r0:r0 + q, c0:c0 + width]
        return acc * _sigmoid(acc)

    dtr = dt_ref[0] + dtb_ref[...]
    e_neg = jnp.exp(-jnp.abs(dtr))
    u = 1.0 + e_neg
    log_u = jnp.log2(u) * math.log(2.0)
    dt = jnp.maximum(dtr, 0.0) + jnp.where(u == 1.0, e_neg, log_u * (e_neg / (u - 1.0)))
    ad = dt * (-jnp.exp(alog_ref[...]))
    acum = jnp.dot(tril_ref[...], ad, precision=lax.Precision.HIGHEST,
                   preferred_element_type=F32)
    acum_row = acum.T
    causal = (lax.broadcasted_iota(jnp.int32, (q, q), 0) >= lax.broadcasted_iota(jnp.int32, (q, q), 1))
    head_of_lane = lax.broadcasted_iota(jnp.int32, (q, 2 * LANES), 1) // HEAD_DIM

    def start(g):
        xs = conv_silu(gw * g, gw)
        bm = conv_silu(SSM_D_INNER + SSM_STATE * g, SSM_STATE)
        cm = conv_silu(SSM_D_INNER + SSM_GROUPS * SSM_STATE + SSM_STATE * g, SSM_STATE)
        xd = xs * _expand_heads(dt, g)
        cbf = cm.astype(BF16)
        st = state_ref[g]
        return xs, bm, xd, st, _dot_nt(cbf, bm.astype(BF16)), _dot(cbf, st.astype(BF16))

    def finish(g, vals):
        xs, bm, xd, st, cb, y_off = vals
        cs = slice(gw * g, gw * (g + 1))
        acum_ch = _expand_heads(acum, g)
        a_last = acum_ch[q - 1:q, :]
        xdb = xd.astype(BF16)
        y_diag = None
        for r in range(GQA_R):
            hh = GQA_R * g + r
            diff = acum[:, hh:hh + 1] - acum_row[hh:hh + 1, :]
            decay = jnp.exp(jnp.where(causal, diff, -jnp.inf))
            y_r = _dot((cb * decay).astype(BF16), xdb)
            y_diag = y_r if r == 0 else jnp.where(head_of_lane == r, y_r, y_diag)
        state_ref[g] = (st * jnp.exp(a_last)
                        + _dot(bm.T.astype(BF16), (xd * jnp.exp(a_last - acum_ch)).astype(BF16)))
        y = y_off * jnp.exp(acum_ch) + y_diag + dsk_ref[:, cs] * xs
        zz = z_ref[0, :, cs]
        gated = y * (zz * _sigmoid(zz))
        o_ref[0, :, cs] = _rms(gated, nw_ref[:, cs]).astype(o_ref.dtype)

    _one_ahead(range(SSM_GROUPS), start, finish)
    xbuf_ref[0:8, :] = xbuf_ref[q:q + 8, :]


def ssd_core(zx, conv_w, conv_b, dt_bias, a_log, d_skip, norm_w, b, s):
    q = SSM_CHUNK
    zx3 = zx.reshape(b, s, SSM_IN_PAD)
    pad = lambda v: jnp.pad(v.astype(F32), (0, LANES - SSM_HEADS)).reshape(1, LANES)
    tril = jnp.asarray(np.tril(np.ones((q, q), np.float32)))
    wide = SSM_D_INNER
    full = lambda shape: pl.BlockSpec(shape, lambda i, c: (0,) * len(shape))
    out = pl.pallas_call(
        _ssd_kernel,
        grid=(b, s // q),
        in_specs=[pl.BlockSpec((1, q, wide), lambda i, c: (i, c, 0)),
                  pl.BlockSpec((1, q, wide), lambda i, c: (i, c, 1)),
                  pl.BlockSpec((1, q, wide), lambda i, c: (i, c, 2)),
                  pl.BlockSpec((1, q, LANES), lambda i, c: (i, c, (SSM_D_INNER + SSM_CONV_DIM) // LANES)),
                  full((SSM_CONV, SSM_CONV_DIM)), full((1, SSM_CONV_DIM)),
                  full((1, LANES)), full((1, LANES)), full((1, wide)), full((1, wide)),
                  full((q, q))],
        out_specs=pl.BlockSpec((1, q, wide), lambda i, c: (i, c, 0)),
        out_shape=jax.ShapeDtypeStruct((b, s, wide), BF16),
        scratch_shapes=[pltpu.VMEM((SSM_GROUPS, SSM_STATE, GQA_R * HEAD_DIM), F32),
                        pltpu.VMEM((q + 8, SSM_CONV_DIM), F32)],
        compiler_params=_cparams("parallel", "arbitrary"),
        name="ssd_core",
    )(zx3, zx3, zx3, zx3, conv_w.astype(F32), conv_b.reshape(1, -1).astype(F32),
      pad(dt_bias), pad(a_log), jnp.repeat(d_skip.astype(F32), HEAD_DIM).reshape(1, wide),
      norm_w.reshape(1, wide).astype(F32), tril)
    return out.reshape(b * s, wide)


def _swa_kernel(q_ref, kc_ref, kp_ref, vtc_ref, vtp_ref, sink_ref, o_ref):
    i = pl.program_id(1)
    w = SWA_BLOCK
    rq = GQA_R * w
    lane = lax.broadcasted_iota(jnp.int32, (1, rq), 1)
    qloc = lane & (w - 1)
    head_of_lane = lane // w
    kk = lax.broadcasted_iota(jnp.int32, (2 * w, rq), 0)
    valid = ((kk < w) & (kk > qloc) & (i > 0)) | ((kk >= w) & ((kk - w) <= qloc))
    scale = jnp.asarray(ATTN_SCALE, BF16)

    def start(kv):
        dsl = slice(HEAD_DIM * kv, HEAD_DIM * (kv + 1))
        qs = jnp.concatenate(
            [q_ref[0, :, HEAD_DIM * (GQA_R * kv + r):HEAD_DIM * (GQA_R * kv + r + 1)] for r in range(GQA_R)],
            axis=0) * scale
        k = jnp.concatenate([kp_ref[0, :, dsl], kc_ref[0, :, dsl]], axis=0)
        return _dot_nt(k, qs)

    def finish(kv, st):
        dsl = slice(HEAD_DIM * kv, HEAD_DIM * (kv + 1))
        sink = jnp.zeros((1, rq), F32)
        for r in range(GQA_R):
            h = GQA_R * kv + r
            sink = jnp.where(head_of_lane == r, sink_ref[:, h:h + 1], sink)
        sc = jnp.where(valid, st, -jnp.inf)
        m = jnp.maximum(jnp.max(sc, axis=0, keepdims=True), sink)
        e = jnp.exp(sc - m)
        den = jnp.sum(e, axis=0, keepdims=True) + jnp.exp(sink - m)
        vt = jnp.concatenate([vtp_ref[0, 0, dsl, :], vtc_ref[0, 0, dsl, :]], axis=1)
        o_q = (_dot(vt, e.astype(BF16)) * (1.0 / den)).T
        o_ref[0, :, KV_DIM * kv:KV_DIM * (kv + 1)] = jnp.concatenate(
            [o_q[w * r:w * (r + 1)] for r in range(GQA_R)], axis=1).astype(o_ref.dtype)

    _one_ahead(range(KV_HEADS), start, finish)


def swa_core(qkv, sinks, b, s):
    w = SWA_BLOCK
    qkv3 = qkv.reshape(b, s, -1)
    qd = N_HEADS * HEAD_DIM
    kcol = qd // KV_DIM
    vt = _slab_transpose(qkv3, qd + KV_DIM)
    prev = lambda i, j: (i, jnp.maximum(j - 1, 0))
    out = pl.pallas_call(
        _swa_kernel,
        grid=(b, s // w),
        in_specs=[pl.BlockSpec((1, w, qd), lambda i, j: (i, j, 0)),
                  pl.BlockSpec((1, w, KV_DIM), lambda i, j: (i, j, kcol)),
                  pl.BlockSpec((1, w, KV_DIM), lambda i, j: prev(i, j) + (kcol,)),
                  pl.BlockSpec((1, 1, KV_DIM, VT_SLAB), lambda i, j: (i, j, 0, 0)),
                  pl.BlockSpec((1, 1, KV_DIM, VT_SLAB), lambda i, j: prev(i, j) + (0, 0)),
                  pl.BlockSpec((1, LANES), lambda i, j: (0, 0))],
        out_specs=pl.BlockSpec((1, w, qd), lambda i, j: (i, j, 0)),
        out_shape=jax.ShapeDtypeStruct((b, s, qd), BF16),
        compiler_params=_cparams("parallel", "arbitrary"),
        name="swa_core",
    )(qkv3, qkv3, qkv3, vt, vt, jnp.pad(sinks.astype(F32), (0, LANES - N_HEADS)).reshape(1, LANES))
    return out.reshape(b * s, qd)


def _compress_kernel(f_ref, pe_ref, w1_ref, w2_ref, o_ref, *, transposed):
    half = NSA_CMP_STRIDE * HEAD_DIM
    x = f_ref[0, 0]
    za = (x + pe_ref[:, 0:half]).astype(BF16)
    zb = (x + pe_ref[:, half:2 * half]).astype(BF16)
    hb = _dot(zb, w1_ref[half:2 * half, :])
    hid = _dot(za, w1_ref[0:half, :]) + pltpu.roll(hb, hb.shape[0] - 1, 0)
    act = (hid * _sigmoid(hid)).astype(BF16)
    if transposed:
        o_ref[0, 0] = _dot_nt(w2_ref[...], act).astype(o_ref.dtype)
    else:
        o_ref[0, 0] = _dot(act, w2_ref[...]).astype(o_ref.dtype)


def compress(tok, pe, w1, w2, b, s, transposed):
    nch = s // NSA_CMP_STRIDE
    half = NSA_CMP_STRIDE * HEAD_DIM
    f = tok.reshape(b, nch, NSA_CMP_STRIDE, KV_HEADS, HEAD_DIM).transpose(0, 3, 1, 2, 4).reshape(b, KV_HEADS, nch, half)
    w2 = (w2.T if transposed else w2).astype(BF16)
    oshape = (HEAD_DIM, nch) if transposed else (nch, HEAD_DIM)
    return pl.pallas_call(
        functools.partial(_compress_kernel, transposed=transposed),
        grid=(b, KV_HEADS),
        in_specs=[pl.BlockSpec((1, 1, nch, half), lambda i, j: (i, j, 0, 0)),
                  pl.BlockSpec((1, 2 * half), lambda i, j: (0, 0)),
                  pl.BlockSpec((2 * half, NSA_CMP_HIDDEN), lambda i, j: (0, 0)),
                  pl.BlockSpec(w2.shape, lambda i, j: (0, 0))],
        out_specs=pl.BlockSpec((1, 1) + oshape, lambda i, j: (i, j, 0, 0)),
        out_shape=jax.ShapeDtypeStruct((b, KV_HEADS) + oshape, BF16),
        compiler_params=_cparams("parallel", "parallel"),
        name="nsa_compress",
    )(f, pe.reshape(1, 2 * half).astype(F32), w1.astype(BF16), w2)


def _dot_split3(a_bf16, x):
    hi = x.astype(BF16)
    r1 = x - hi.astype(F32)
    mid = r1.astype(BF16)
    lo = (r1 - mid.astype(F32)).astype(BF16)
    return _dot(a_bf16, hi) + _dot(a_bf16, mid) + _dot(a_bf16, lo)


def _nsa_kernel(q_ref, ks_ref, vst_ref, kw_ref, vwt_ref, kcmp_ref, vcmpt_ref, g_ref, ov_ref,
                o_ref, qs_ref, bias_ref, ocmp_ref, acc_ref, m_ref, l_ref, st_ref, *, seq):
    i = pl.program_id(1)
    qb = NSA_Q_BLOCK
    rq = GQA_R * qb
    n_cmp = seq // NSA_CMP_STRIDE
    n_sel = seq // NSA_SEL_BLOCK
    blocks_per_chunk = NSA_SEL_CHUNK // NSA_SEL_BLOCK
    slabs_per_chunk = NSA_SEL_CHUNK // VT_SLAB
    t0 = i * qb
    qpos = t0 + (lax.broadcasted_iota(jnp.int32, (1, rq), 1) & (qb - 1))
    gates_t = _sigmoid(g_ref[0]).T

    cmp_end = lax.broadcasted_iota(jnp.int32, (n_cmp, rq), 0) * NSA_CMP_STRIDE + (NSA_CMP_LEN - 1)
    mask_c = cmp_end <= qpos
    has_cmp = jnp.where(qpos >= NSA_CMP_LEN - 1, 1.0, 0.0)
    blk = lax.broadcasted_iota(jnp.int32, (n_sel, LANES), 0)
    blk_f = blk.astype(F32)
    forced = (blk == i) | (blk == 0)
    allowed = blk <= i
    lane_lo = lax.broadcasted_iota(jnp.int32, (n_sel, LANES), 1) < qb
    importance = {}
    cw = jnp.maximum(t0 - NSA_WINDOW, 0) // VT_SLAB
    wkeys = NSA_WIN_SLABS * VT_SLAB
    wstart = pl.multiple_of(cw * VT_SLAB, VT_SLAB)
    delta_w = qpos - (wstart + lax.broadcasted_iota(jnp.int32, (wkeys, rq), 0))
    mask_w = (delta_w >= 0) & (delta_w < NSA_WINDOW)
    last_chunk = i // blocks_per_chunk
    key_in_blk = lax.broadcasted_iota(jnp.int32, (NSA_SEL_BLOCK, rq), 0)
    groups = range(KV_HEADS)

    def cmp_start(kv):
        qs = jnp.concatenate(
            [q_ref[0, :, HEAD_DIM * (GQA_R * kv + r):HEAD_DIM * (GQA_R * kv + r + 1)] for r in range(GQA_R)],
            axis=0)
        qs = qs * jnp.asarray(ATTN_SCALE, BF16)
        qs_ref[kv] = qs
        return _dot_nt(kcmp_ref[0, kv], qs)

    def cmp_finish(kv, logits):
        lc = jnp.where(mask_c, logits, MASK_NEG)
        ec = jnp.exp(lc - jnp.max(lc, axis=0, keepdims=True))
        inv = has_cmp / jnp.sum(ec, axis=0, keepdims=True)
        ocmp_ref[kv] = _dot(vcmpt_ref[0, kv], ec.astype(BF16)) * inv
        imp = _dot_split3(ov_ref[...], ec) * inv
        imp = imp + pltpu.roll(imp, qb, 1)
        imp = imp + pltpu.roll(imp, 2 * qb, 1)
        m_ref[kv] = jnp.full((1, rq), MASK_NEG, F32)
        l_ref[kv] = jnp.zeros((1, rq), F32)
        acc_ref[kv] = jnp.zeros((HEAD_DIM, rq), F32)
        importance[kv] = imp[:, 0:LANES]
        if kv % 2 == 1:
            select_pair(kv - 1, kv)

    def select_pair(ka, kb):
        v = jnp.where(lane_lo, importance[ka], importance[kb])
        v = jnp.where(allowed, jnp.where(forced, jnp.inf, v), -jnp.inf)
        sel = blk < 0
        for _ in range(NSA_TOP_N):
            best = jnp.max(v, axis=0, keepdims=True)
            first = jnp.min(jnp.where(v == best, blk_f, float(n_sel)), axis=0, keepdims=True)
            pick = blk_f == first
            sel = sel | pick
            v = jnp.where(pick, -jnp.inf, v)
        bias = jnp.where(sel & allowed, 0.0, MASK_NEG)
        swapped = pltpu.roll(bias, qb, 1)
        for kv, own_lo in ((ka, True), (kb, False)):
            half = jnp.where(lane_lo, bias, swapped) if own_lo else jnp.where(lane_lo, swapped, bias)
            bias_ref[kv] = jnp.concatenate([half, half], axis=1)

    def cmp_phase_start(item):
        return sel_start(0, 0) if item == "first_chunk" else cmp_start(item)

    def cmp_phase_finish(item, res):
        if item == "first_chunk":
            st_ref[...] = res
        else:
            cmp_finish(item, res)

    def sel_start(c, kv):
        k0 = pl.multiple_of(c * NSA_SEL_CHUNK, NSA_SEL_CHUNK)
        return _dot_nt(ks_ref[0, pl.ds(k0, NSA_SEL_CHUNK), HEAD_DIM * kv:HEAD_DIM * (kv + 1)], qs_ref[kv])

    def sel_finish(c, kv, st, causal):
        dsl = slice(HEAD_DIM * kv, HEAD_DIM * (kv + 1))
        k0 = c * NSA_SEL_CHUNK
        bias8 = bias_ref[kv, pl.ds(pl.multiple_of(c * blocks_per_chunk, blocks_per_chunk), blocks_per_chunk), :]
        pieces = []
        for jb in range(blocks_per_chunk):
            sj = st[NSA_SEL_BLOCK * jb:NSA_SEL_BLOCK * (jb + 1)] + bias8[jb:jb + 1, :]
            if causal:
                sj = jnp.where(k0 + NSA_SEL_BLOCK * jb + key_in_blk <= qpos, sj, MASK_NEG)
            pieces.append(sj)
        sc = jnp.concatenate(pieces, axis=0)
        m_run = m_ref[kv]
        m_new = jnp.maximum(m_run, jnp.max(sc, axis=0, keepdims=True))
        p = jnp.exp(sc - m_new)
        alpha = jnp.exp(m_run - m_new)
        vt = jnp.concatenate([vst_ref[0, slabs_per_chunk * c + u, dsl, :] for u in range(slabs_per_chunk)],
                             axis=1)
        m_ref[kv] = m_new
        l_ref[kv] = alpha * l_ref[kv] + jnp.sum(p, axis=0, keepdims=True)
        acc_ref[kv] = alpha * acc_ref[kv] + _dot(vt, p.astype(BF16))

    _one_ahead(list(groups) + ["first_chunk"], cmp_phase_start, cmp_phase_finish)

    def sel_trip(c, carry):
        pending = st_ref[...]
        for kv in groups:
            nxt = sel_start(c, kv + 1) if kv + 1 < KV_HEADS else sel_start(c + 1, 0)
            sel_finish(c, kv, pending, False)
            pending = nxt
        st_ref[...] = pending
        return carry

    lax.fori_loop(0, last_chunk, sel_trip, 0)

    def tail_start(item):
        kv, is_window = item
        if is_window:
            return _dot_nt(kw_ref[0, pl.ds(wstart, wkeys), HEAD_DIM * kv:HEAD_DIM * (kv + 1)], qs_ref[kv])
        return st_ref[...] if kv == 0 else sel_start(last_chunk, kv)

    def tail_finish(item, logits):
        kv, is_window = item
        if not is_window:
            sel_finish(last_chunk, kv, logits, True)
            return
        dsl = slice(HEAD_DIM * kv, HEAD_DIM * (kv + 1))
        lw = jnp.where(mask_w, logits, MASK_NEG)
        ew = jnp.exp(lw - jnp.max(lw, axis=0, keepdims=True))
        vwt = jnp.concatenate([vwt_ref[0, cw + u, dsl, :] for u in range(NSA_WIN_SLABS)], axis=1)
        o_win = _dot(vwt, ew.astype(BF16)) * (1.0 / jnp.sum(ew, axis=0, keepdims=True))
        o_sel = acc_ref[kv] * (1.0 / l_ref[kv])

        def gate_row(branch):
            return jnp.concatenate(
                [gates_t[3 * (GQA_R * kv + r) + branch:3 * (GQA_R * kv + r) + branch + 1, :] for r in range(GQA_R)],
                axis=1)
        o_t = gate_row(0) * ocmp_ref[kv] + gate_row(1) * o_sel + gate_row(2) * o_win
        o_q = o_t.T
        o_ref[0, :, KV_DIM * kv:KV_DIM * (kv + 1)] = jnp.concatenate(
            [o_q[qb * r:qb * (r + 1)] for r in range(GQA_R)], axis=1).astype(o_ref.dtype)

    _one_ahead([(kv, w) for kv in groups for w in (False, True)], tail_start, tail_finish)


def nsa_core(pa, pb, k_cmp, v_cmp_t, b, s):
    qb = NSA_Q_BLOCK
    qd = N_HEADS * HEAD_DIM
    n_cmp = s // NSA_CMP_STRIDE
    n_sel = s // NSA_SEL_BLOCK
    n_slab = s // VT_SLAB
    pa3 = pa.reshape(b, s, -1)
    pb3 = pb.reshape(b, s, -1)
    vst = _slab_transpose(pa3, qd + 2 * KV_DIM)
    vwt = _slab_transpose(pa3, qd + 3 * KV_DIM)
    sel_lo = np.arange(n_sel)[:, None] * NSA_SEL_BLOCK
    cmp_lo = np.arange(n_cmp)[None, :] * NSA_CMP_STRIDE
    ov = np.clip(np.minimum(sel_lo + NSA_SEL_BLOCK, cmp_lo + NSA_CMP_LEN) - np.maximum(sel_lo, cmp_lo), 0, None)
    ov = jnp.asarray((ov / NSA_CMP_LEN).astype(np.float32), BF16)
    kcol = qd // KV_DIM
    seqspec = lambda col: pl.BlockSpec((1, s, KV_DIM), lambda i, j: (i, 0, col))
    slabspec = pl.BlockSpec((1, n_slab, KV_DIM, VT_SLAB), lambda i, j: (i, 0, 0, 0))
    rq = GQA_R * qb
    out = pl.pallas_call(
        functools.partial(_nsa_kernel, seq=s),
        grid=(b, s // qb),
        in_specs=[pl.BlockSpec((1, qb, qd), lambda i, j: (i, j, 0)),
                  seqspec(kcol), slabspec, seqspec(kcol + 1), slabspec,
                  pl.BlockSpec((1, KV_HEADS, n_cmp, HEAD_DIM), lambda i, j: (i, 0, 0, 0)),
                  pl.BlockSpec((1, KV_HEADS, HEAD_DIM, n_cmp), lambda i, j: (i, 0, 0, 0)),
                  pl.BlockSpec((1, qb, KV_DIM), lambda i, j: (i, j, 2)),
                  pl.BlockSpec((n_sel, n_cmp), lambda i, j: (0, 0))],
        out_specs=pl.BlockSpec((1, qb, qd), lambda i, j: (i, j, 0)),
        out_shape=jax.ShapeDtypeStruct((b, s, qd), BF16),
        scratch_shapes=[pltpu.VMEM((KV_HEADS, rq, HEAD_DIM), BF16),
                        pltpu.VMEM((KV_HEADS, n_sel, rq), F32),
                        pltpu.VMEM((KV_HEADS, HEAD_DIM, rq), F32),
                        pltpu.VMEM((KV_HEADS, HEAD_DIM, rq), F32),
                        pltpu.VMEM((KV_HEADS, 1, rq), F32),
                        pltpu.VMEM((KV_HEADS, 1, rq), F32),
                        pltpu.VMEM((NSA_SEL_CHUNK, rq), F32)],
        compiler_params=_cparams("parallel", "arbitrary"),
        name="nsa_core",
    )(pa3, pa3, vst, pa3, vwt, k_cmp, v_cmp_t, pb3, ov)
    return out.reshape(b * s, qd)


def _rope_tables(positions):
    half = ROPE_DIM // 2
    inv_freq = ROPE_THETA ** (-jnp.arange(0, ROPE_DIM, 2, dtype=F32) / ROPE_DIM)
    ang = positions.astype(F32).reshape(-1)[:, None] * inv_freq
    cos, sin = jnp.cos(ang), jnp.sin(ang)
    t = cos.shape[0]
    ones = jnp.ones((t, HEAD_DIM - ROPE_DIM), F32)
    zeros = jnp.zeros((t, HEAD_DIM - ROPE_DIM), F32)
    zh = jnp.zeros((t, half), F32)
    c = jnp.concatenate([cos, cos, ones], axis=1)
    sa = jnp.concatenate([-sin, zh, zeros], axis=1)
    sb = jnp.concatenate([zh, sin, zeros], axis=1)
    return tuple(jnp.tile(v, (1, LANES // HEAD_DIM)) for v in (c, sa, sb))


def _pad_cols(w, n):
    return jnp.pad(w, ((0, 0), (0, n - w.shape[1])))


def mamba2_mixer(h, ln_w, w_in, conv_w, conv_b, dt_bias, a_log, d_skip, norm_w, w_out, b, s):
    w = _pad_cols(w_in, SSM_IN_PAD).astype(BF16)
    zx = norm_proj(h, ln_w, w, jnp.zeros((SSM_IN_PAD,), F32), F32, tn=SSM_IN_TILE)
    y = ssd_core(zx, conv_w, conv_b, dt_bias, a_log, d_skip, norm_w, b, s)
    return out_proj(y, w_out.astype(BF16), jnp.zeros((D_MODEL,), F32), h)


def swa_mixer(h, ln_w, w_qkv, b_qkv, sinks, w_o, b_o, rope, b, s):
    qk = (N_HEADS + KV_HEADS) * HEAD_DIM
    qkv = norm_proj(h, ln_w, w_qkv.astype(BF16), b_qkv.astype(F32), BF16, rope=rope, rope_cols=((0, qk),))
    o = swa_core(qkv, sinks, b, s)
    return out_proj(o, w_o.astype(BF16), b_o.astype(F32), h)


def nsa_mixer(h, ln_w, w_in, pe_k, k_w1, k_w2, pe_v, v_w1, v_w2, w_o, rope, b, s):
    qd = N_HEADS * HEAD_DIM
    cols = lambda k: w_in[:, qd + KV_DIM * k:qd + KV_DIM * (k + 1)]
    wa = jnp.concatenate([w_in[:, :qd], cols(2), cols(4), cols(3), cols(5)], axis=1).astype(BF16)
    wb = _pad_cols(jnp.concatenate([cols(0), cols(1), w_in[:, qd + 6 * KV_DIM:]], axis=1), 3 * KV_DIM).astype(BF16)
    pa = norm_proj(h, ln_w, wa, jnp.zeros((wa.shape[1],), F32), BF16, rope=rope,
                   rope_cols=((0, qd + 2 * KV_DIM),))
    pb = norm_proj(h, ln_w, wb, jnp.zeros((wb.shape[1],), F32), F32, rope=rope, rope_cols=((0, KV_DIM),))
    pb3 = pb.reshape(b, s, -1)
    k_cmp = compress(pb3[..., 0:KV_DIM], pe_k, k_w1, k_w2, b, s, False)
    v_cmp_t = compress(pb3[..., KV_DIM:2 * KV_DIM], pe_v, v_w1, v_w2, b, s, True)
    o = nsa_core(pa, pb, k_cmp, v_cmp_t, b, s)
    return out_proj(o, w_o.astype(BF16), jnp.zeros((D_MODEL,), F32), h)


def kernel(x, positions, ln_ffn1, ffn1_w_in, ffn1_w_out, ln_mix, ln_ffn2, ffn2_w_in, ffn2_w_out, ssm_w_in, ssm_conv_w, ssm_conv_b, ssm_dt_bias, ssm_a_log, ssm_d, ssm_norm_w, ssm_w_out, swa_w_qkv, swa_b_qkv, swa_sinks, swa_w_o, swa_b_o, nsa_w_in, nsa_pe_k, nsa_k_w1, nsa_k_w2, nsa_pe_v, nsa_v_w1, nsa_v_w2, nsa_w_o, final_norm):
    b, s, d = x.shape
    depth = ln_ffn1.shape[0]
    rope = _rope_tables(positions)
    h = x.reshape(b * s, d)
    for i in range(depth):
        kind, inst = i % 3, i // 3
        h = ffn(h, ln_ffn1[i], ffn1_w_in[i], ffn1_w_out[i])
        if kind == 0:
            h = mamba2_mixer(h, ln_mix[i], ssm_w_in[inst], ssm_conv_w[inst], ssm_conv_b[inst],
                             ssm_dt_bias[inst], ssm_a_log[inst], ssm_d[inst], ssm_norm_w[inst],
                             ssm_w_out[inst], b, s)
        elif kind == 1:
            h = swa_mixer(h, ln_mix[i], swa_w_qkv[inst], swa_b_qkv[inst], swa_sinks[inst],
                          swa_w_o[inst], swa_b_o[inst], rope, b, s)
        else:
            h = nsa_mixer(h, ln_mix[i], nsa_w_in[inst], nsa_pe_k[inst], nsa_k_w1[inst], nsa_k_w2[inst],
                          nsa_pe_v[inst], nsa_v_w1[inst], nsa_v_w2[inst], nsa_w_o[inst], rope, b, s)
        h = ffn(h, ln_ffn2[i], ffn2_w_in[i], ffn2_w_out[i], final_norm if i == depth - 1 else None)
    return h.reshape(b, s, d)
```

```python
import functools
import math

import numpy as np
import jax
import jax.numpy as jnp
from jax import lax
from jax.experimental import pallas as pl
from jax.experimental.pallas import tpu as pltpu

F32 = jnp.float32
BF16 = jnp.bfloat16

D_MODEL = 1024
RMS_EPS = 1e-6
D_FF = 2816
HEAD_DIM = 64
ROPE_DIM = HEAD_DIM // 4
ROPE_THETA = 500000.0
ATTN_SCALE = HEAD_DIM ** -0.5
MASK_NEG = -1e30

SSM_D_INNER = 2 * D_MODEL
SSM_HEADS = 32
SSM_GROUPS = 8
SSM_STATE = 128
SSM_CONV = 4
SSM_CHUNK = 128
SSM_CONV_DIM = SSM_D_INNER + 2 * SSM_GROUPS * SSM_STATE
SSM_IN_TILE = 1280
SSM_IN_PAD = 5 * SSM_IN_TILE

N_HEADS = 16
KV_HEADS = 4
GQA_R = N_HEADS // KV_HEADS
KV_DIM = KV_HEADS * HEAD_DIM

SWA_BLOCK = 128

NSA_CMP_LEN = 32
NSA_CMP_STRIDE = 16
NSA_CMP_HIDDEN = 256
NSA_SEL_BLOCK = 64
NSA_TOP_N = 8
NSA_WINDOW = 512
NSA_Q_BLOCK = 64
NSA_SEL_CHUNK = 512
NSA_AHEAD = 2
VT_SLAB = 128
NSA_WIN_SLABS = NSA_WINDOW // VT_SLAB + 1

LANES = 128
VMEM_LIMIT = 56 * 1024 * 1024
TOKEN_TILE = 1024
ROW_SPLITS = 4
FF_TILE = 256


def _cparams(*sem):
    return pltpu.CompilerParams(dimension_semantics=sem, vmem_limit_bytes=VMEM_LIMIT)


def _dot(a, b):
    return jnp.dot(a, b, preferred_element_type=F32)


def _dot_nt(a, b):
    return lax.dot_general(a, b, (((1,), (1,)), ((), ())), preferred_element_type=F32)


def _sigmoid(x):
    return 1.0 / (1.0 + jnp.exp(-x))


def _rms(x, w):
    return x * lax.rsqrt(jnp.mean(x * x, axis=-1, keepdims=True) + RMS_EPS) * w


def _one_ahead(items, start, finish, depth=1, primed=()):
    items = list(items)
    pending = list(primed)
    nxt = len(pending)
    while nxt < min(depth, len(items)):
        pending.append(start(items[nxt]))
        nxt += 1
    for item in items:
        if nxt < len(items):
            pending.append(start(items[nxt]))
            nxt += 1
        finish(item, pending.pop(0))


def _rope_store(o_ref, rows, acc, rope_segs, c, sa, sb):
    for k, roped in enumerate(rope_segs):
        seg = acc[:, LANES * k:LANES * (k + 1)]
        if roped:
            seg = (seg * c + pltpu.roll(seg, LANES - ROPE_DIM // 2, 1) * sa
                   + pltpu.roll(seg, ROPE_DIM // 2, 1) * sb)
        o_ref[rows, LANES * k:LANES * (k + 1)] = seg.astype(o_ref.dtype)


def _norm_proj_kernel(*refs, rope_segs, col_tiles, with_vt):
    refs = list(refs)
    x_ref, lnw_ref, w_ref, b_ref = refs[:4]
    del refs[:4]
    if rope_segs is not None:
        c_ref, sa_ref, sb_ref = refs[:3]
        del refs[:3]
    if with_vt:
        wt_ref, bt_ref = refs[:2]
        del refs[:2]
    o_ref = refs.pop(0)
    vt_ref = refs.pop(0) if with_vt else None
    xn_ref = refs.pop(0) if col_tiles > 1 else None
    tm = x_ref.shape[0]
    rt = tm // ROW_SPLITS
    row_tiles = [slice(rt * h, rt * (h + 1)) for h in range(ROW_SPLITS)]

    if col_tiles > 1:
        @pl.when(pl.program_id(1) == 0)
        def _():
            xn_ref[...] = _rms(x_ref[...], lnw_ref[...]).astype(BF16)

    def start(rows):
        xn = xn_ref[rows, :] if col_tiles > 1 else _rms(x_ref[rows, :], lnw_ref[...]).astype(BF16)
        acc = _dot(xn, w_ref[...]) + b_ref[...]
        return (acc, _dot_nt(wt_ref[...], xn) + bt_ref[...]) if with_vt else (acc, None)

    def finish(rows, res):
        acc, vt = res
        if rope_segs is None:
            o_ref[rows, :] = acc.astype(o_ref.dtype)
        else:
            _rope_store(o_ref, rows, acc, rope_segs, c_ref[rows, :], sa_ref[rows, :], sb_ref[rows, :])
        if with_vt:
            for u in range(rt // VT_SLAB):
                vt_ref[rows.start // VT_SLAB + u] = vt[:, VT_SLAB * u:VT_SLAB * (u + 1)].astype(vt_ref.dtype)

    _one_ahead(row_tiles, start, finish)


def norm_proj(h, ln_w, w, bias, out_dtype, tn=None, rope=None, rope_cols=(), w_t=None, bias_t=None):
    t, d = h.shape
    n = w.shape[1]
    tm = min(TOKEN_TILE, t)
    tn = n if tn is None else tn
    col_tiles = n // tn
    assert n == col_tiles * tn and t % tm == 0 and tn % LANES == 0
    with_vt = w_t is not None
    rope_segs = None
    if rope_cols:
        assert col_tiles == 1
        rope_segs = tuple(any(lo <= LANES * k < hi for lo, hi in rope_cols) for k in range(n // LANES))
    in_specs = [pl.BlockSpec((tm, d), lambda i, j: (i, 0)),
                pl.BlockSpec((1, d), lambda i, j: (0, 0)),
                pl.BlockSpec((d, tn), lambda i, j: (0, j)),
                pl.BlockSpec((1, tn), lambda i, j: (0, j))]
    args = [h, ln_w.reshape(1, d), w, bias.reshape(1, n)]
    if rope_segs is not None:
        in_specs += [pl.BlockSpec((tm, LANES), lambda i, j: (i, 0))] * 3
        args += list(rope)
    out_specs = [pl.BlockSpec((tm, tn), lambda i, j: (i, j))]
    out_shape = [jax.ShapeDtypeStruct((t, n), out_dtype)]
    if with_vt:
        assert col_tiles == 1
        nv = w_t.shape[0]
        in_specs += [pl.BlockSpec((nv, d), lambda i, j: (0, 0)), pl.BlockSpec((nv, 1), lambda i, j: (0, 0))]
        args += [w_t, bias_t.reshape(nv, 1)]
        out_specs.append(pl.BlockSpec((tm // VT_SLAB, nv, VT_SLAB), lambda i, j: (i, 0, 0)))
        out_shape.append(jax.ShapeDtypeStruct((t // VT_SLAB, nv, VT_SLAB), BF16))
    res = pl.pallas_call(
        functools.partial(_norm_proj_kernel, rope_segs=rope_segs, col_tiles=col_tiles, with_vt=with_vt),
        grid=(t // tm, col_tiles),
        in_specs=in_specs,
        out_specs=out_specs,
        out_shape=out_shape,
        scratch_shapes=[pltpu.VMEM((tm, d), BF16)] if col_tiles > 1 else [],
        compiler_params=_cparams("parallel", "arbitrary"),
        name="norm_proj",
    )(*args)
    return tuple(res) if with_vt else res[0]


def _out_proj_kernel(y_ref, w_ref, b_ref, r_ref, o_ref):
    o_ref[...] = r_ref[...] + _dot(y_ref[...], w_ref[...]) + b_ref[...]


def out_proj(y, w, bias, res):
    t, k = y.shape
    d = w.shape[1]
    tm = min(TOKEN_TILE, t)
    return pl.pallas_call(
        _out_proj_kernel,
        grid=(t // tm,),
        in_specs=[pl.BlockSpec((tm, k), lambda i: (i, 0)),
                  pl.BlockSpec((k, d), lambda i: (0, 0)),
                  pl.BlockSpec((1, d), lambda i: (0, 0)),
                  pl.BlockSpec((tm, d), lambda i: (i, 0))],
        out_specs=pl.BlockSpec((tm, d), lambda i: (i, 0)),
        out_shape=jax.ShapeDtypeStruct((t, d), F32),
        compiler_params=_cparams("parallel"),
        name="out_proj",
    )(y, w, bias.reshape(1, d), res)


def _ffn_kernel(x_ref, lnw_ref, wg_ref, wu_ref, wo_ref, fnw_ref, o_ref, xn_ref, acc_ref, *, final_norm):
    f = pl.program_id(1)
    tm = x_ref.shape[0]
    rt = tm // ROW_SPLITS

    @pl.when(f == 0)
    def _():
        xn_ref[...] = _rms(x_ref[...], lnw_ref[...]).astype(BF16)
        acc_ref[...] = jnp.zeros(acc_ref.shape, F32)

    def start(rows):
        xn = xn_ref[rows, :]
        return _dot(xn, wg_ref[0]), _dot(xn, wu_ref[0])

    def finish(rows, gu):
        g, u = gu
        acc_ref[rows, :] += _dot((g * _sigmoid(g) * u).astype(BF16), wo_ref[...])

    _one_ahead([slice(rt * h, rt * (h + 1)) for h in range(ROW_SPLITS)], start, finish)

    @pl.when(f == pl.num_programs(1) - 1)
    def _():
        y = x_ref[...] + 0.5 * acc_ref[...]
        if final_norm:
            y = _rms(y, fnw_ref[...])
        o_ref[...] = y


def ffn(h, ln_w, w_in, w_out, final_w=None):
    t, d = h.shape
    tm = min(TOKEN_TILE, t)
    tf = FF_TILE
    nf = D_FF // tf
    final_norm = final_w is not None
    fnw = (final_w if final_norm else ln_w).reshape(1, d)
    w_tiles = w_in.astype(BF16).reshape(d, 2 * nf, tf).transpose(1, 0, 2)
    return pl.pallas_call(
        functools.partial(_ffn_kernel, final_norm=final_norm),
        grid=(t // tm, nf),
        in_specs=[pl.BlockSpec((tm, d), lambda i, f: (i, 0)),
                  pl.BlockSpec((1, d), lambda i, f: (0, 0)),
                  pl.BlockSpec((1, d, tf), lambda i, f: (f, 0, 0)),
                  pl.BlockSpec((1, d, tf), lambda i, f: (nf + f, 0, 0)),
                  pl.BlockSpec((tf, d), lambda i, f: (f, 0)),
                  pl.BlockSpec((1, d), lambda i, f: (0, 0))],
        out_specs=pl.BlockSpec((tm, d), lambda i, f: (i, 0)),
        out_shape=jax.ShapeDtypeStruct((t, d), F32),
        scratch_shapes=[pltpu.VMEM((tm, d), BF16), pltpu.VMEM((tm, d), F32)],
        compiler_params=_cparams("parallel", "arbitrary"),
        name="ffn",
    )(h, ln_w.reshape(1, d), w_tiles, w_tiles, w_out.astype(BF16), fnw)


def _expand_heads(mat, g):
    rows = mat.shape[0]
    lane = lax.broadcasted_iota(jnp.int32, (rows, LANES), 1)
    pieces = []
    for p in range(2):
        h0 = GQA_R * g + 2 * p
        a = jnp.broadcast_to(mat[:, h0:h0 + 1], (rows, LANES))
        b = jnp.broadcast_to(mat[:, h0 + 1:h0 + 2], (rows, LANES))
        pieces.append(jnp.where(lane < HEAD_DIM, a, b))
    return jnp.concatenate(pieces, axis=1)


def _ssd_kernel(z_ref, x_ref, bc_ref, dt_ref, cw_ref, cb_ref, dtb_ref, alog_ref, dsk_ref, nw_ref,
                tril_ref, o_ref, state_ref, xbuf_ref):
    q = SSM_CHUNK
    gw = GQA_R * HEAD_DIM
    c = pl.program_id(1)

    @pl.when(c == 0)
    def _():
        state_ref[...] = jnp.zeros(state_ref.shape, F32)
        xbuf_ref[0:8, :] = jnp.zeros((8, SSM_CONV_DIM), F32)

    xbuf_ref[8:8 + q, 0:SSM_D_INNER] = x_ref[0]
    xbuf_ref[8:8 + q, SSM_D_INNER:SSM_CONV_DIM] = bc_ref[0]

    def conv_silu(c0, width):
        acc = cb_ref[:, c0:c0 + width]
        for k in reversed(range(SSM_CONV)):
            r0 = 8 - (SSM_CONV - 1) + k
            acc = acc + cw_ref[k:k + 1, c0:c0 + width] * xbuf_ref[r0:r0 + q, c0:c0 + width]
        return acc * _sigmoid(acc)

    dtr = dt_ref[0] + dtb_ref[...]
    e_neg = jnp.exp(-jnp.abs(dtr))
    u = 1.0 + e_neg
    log_u = jnp.log2(u) * math.log(2.0)
    dt = jnp.maximum(dtr, 0.0) + jnp.where(u == 1.0, e_neg, log_u * (e_neg / (u - 1.0)))
    ad = dt * (-jnp.exp(alog_ref[...]))
    acum = jnp.dot(tril_ref[...], ad, precision=lax.Precision.HIGHEST,
                   preferred_element_type=F32)
    acum_row = acum.T
    causal = (lax.broadcasted_iota(jnp.int32, (q, q), 0) >= lax.broadcasted_iota(jnp.int32, (q, q), 1))
    head_of_lane = lax.broadcasted_iota(jnp.int32, (q, 2 * LANES), 1) // HEAD_DIM

    def start(g):
        xs = conv_silu(gw * g, gw)
        bm = conv_silu(SSM_D_INNER + SSM_STATE * g, SSM_STATE)
        cm = conv_silu(SSM_D_INNER + SSM_GROUPS * SSM_STATE + SSM_STATE * g, SSM_STATE)
        xd = xs * _expand_heads(dt, g)
        cbf = cm.astype(BF16)
        st = state_ref[g]
        return xs, bm, xd, st, _dot_nt(cbf, bm.astype(BF16)), _dot(cbf, st.astype(BF16))

    def finish(g, vals):
        xs, bm, xd, st, cb, y_off = vals
        cs = slice(gw * g, gw * (g + 1))
        acum_ch = _expand_heads(acum, g)
        a_last = acum_ch[q - 1:q, :]
        xdb = xd.astype(BF16)
        y_diag = None
        for r in range(GQA_R):
            hh = GQA_R * g + r
            diff = acum[:, hh:hh + 1] - acum_row[hh:hh + 1, :]
            decay = jnp.exp(jnp.where(causal, diff, -jnp.inf))
            y_r = _dot((cb * decay).astype(BF16), xdb)
            y_diag = y_r if r == 0 else jnp.where(head_of_lane == r, y_r, y_diag)
        state_ref[g] = (st * jnp.exp(a_last)
                        + _dot(bm.T.astype(BF16), (xd * jnp.exp(a_last - acum_ch)).astype(BF16)))
        y = y_off * jnp.exp(acum_ch) + y_diag + dsk_ref[:, cs] * xs
        zz = z_ref[0, :, cs]
        gated = y * (zz * _sigmoid(zz))
        o_ref[0, :, cs] = _rms(gated, nw_ref[:, cs]).astype(o_ref.dtype)

    _one_ahead(range(SSM_GROUPS), start, finish)
    xbuf_ref[0:8, :] = xbuf_ref[q:q + 8, :]


def ssd_core(zx, conv_w, conv_b, dt_bias, a_log, d_skip, norm_w, b, s):
    q = SSM_CHUNK
    zx3 = zx.reshape(b, s, SSM_IN_PAD)
    pad = lambda v: jnp.pad(v.astype(F32), (0, LANES - SSM_HEADS)).reshape(1, LANES)
    tril = jnp.asarray(np.tril(np.ones((q, q), np.float32)))
    wide = SSM_D_INNER
    full = lambda shape: pl.BlockSpec(shape, lambda i, c: (0,) * len(shape))
    out = pl.pallas_call(
        _ssd_kernel,
        grid=(b, s // q),
        in_specs=[pl.BlockSpec((1, q, wide), lambda i, c: (i, c, 0)),
                  pl.BlockSpec((1, q, wide), lambda i, c: (i, c, 1)),
                  pl.BlockSpec((1, q, wide), lambda i, c: (i, c, 2)),
                  pl.BlockSpec((1, q, LANES), lambda i, c: (i, c, (SSM_D_INNER + SSM_CONV_DIM) // LANES)),
                  full((SSM_CONV, SSM_CONV_DIM)), full((1, SSM_CONV_DIM)),
                  full((1, LANES)), full((1, LANES)), full((1, wide)), full((1, wide)),
                  full((q, q))],
        out_specs=pl.BlockSpec((1, q, wide), lambda i, c: (i, c, 0)),
        out_shape=jax.ShapeDtypeStruct((b, s, wide), BF16),
        scratch_shapes=[pltpu.VMEM((SSM_GROUPS, SSM_STATE, GQA_R * HEAD_DIM), F32),
                        pltpu.VMEM((q + 8, SSM_CONV_DIM), F32)],
        compiler_params=_cparams("parallel", "arbitrary"),
        name="ssd_core",
    )(zx3, zx3, zx3, zx3, conv_w.astype(F32), conv_b.reshape(1, -1).astype(F32),
      pad(dt_bias), pad(a_log), jnp.repeat(d_skip.astype(F32), HEAD_DIM).reshape(1, wide),
      norm_w.reshape(1, wide).astype(F32), tril)
    return out.reshape(b * s, wide)


def _swa_kernel(q_ref, kc_ref, kp_ref, vtc_ref, vtp_ref, sink_ref, o_ref):
    i = pl.program_id(1)
    w = SWA_BLOCK
    rq = GQA_R * w
    lane = lax.broadcasted_iota(jnp.int32, (1, rq), 1)
    qloc = lane & (w - 1)
    head_of_lane = lane // w
    kk = lax.broadcasted_iota(jnp.int32, (2 * w, rq), 0)
    valid = ((kk < w) & (kk > qloc) & (i > 0)) | ((kk >= w) & ((kk - w) <= qloc))
    scale = jnp.asarray(ATTN_SCALE, BF16)

    def start(kv):
        dsl = slice(HEAD_DIM * kv, HEAD_DIM * (kv + 1))
        qs = jnp.concatenate(
            [q_ref[0, :, HEAD_DIM * (GQA_R * kv + r):HEAD_DIM * (GQA_R * kv + r + 1)] for r in range(GQA_R)],
            axis=0) * scale
        k = jnp.concatenate([kp_ref[0, :, dsl], kc_ref[0, :, dsl]], axis=0)
        return _dot_nt(k, qs)

    def finish(kv, st):
        dsl = slice(HEAD_DIM * kv, HEAD_DIM * (kv + 1))
        sink = jnp.zeros((1, rq), F32)
        for r in range(GQA_R):
            h = GQA_R * kv + r
            sink = jnp.where(head_of_lane == r, sink_ref[:, h:h + 1], sink)
        sc = jnp.where(valid, st, -jnp.inf)
        m = jnp.maximum(jnp.max(sc, axis=0, keepdims=True), sink)
        e = jnp.exp(sc - m)
        den = jnp.sum(e, axis=0, keepdims=True) + jnp.exp(sink - m)
        vt = jnp.concatenate([vtp_ref[0, 0, dsl, :], vtc_ref[0, 0, dsl, :]], axis=1)
        o_q = (_dot(vt, e.astype(BF16)) * (1.0 / den)).T
        o_ref[0, :, KV_DIM * kv:KV_DIM * (kv + 1)] = jnp.concatenate(
            [o_q[w * r:w * (r + 1)] for r in range(GQA_R)], axis=1).astype(o_ref.dtype)

    _one_ahead(range(KV_HEADS), start, finish, depth=2)


def swa_core(qk, vt, sinks, b, s):
    w = SWA_BLOCK
    qkv3 = qk.reshape(b, s, -1)
    qd = N_HEADS * HEAD_DIM
    kcol = qd // KV_DIM
    vt = vt.reshape(b, s // VT_SLAB, KV_DIM, VT_SLAB)
    prev = lambda i, j: (i, jnp.maximum(j - 1, 0))
    out = pl.pallas_call(
        _swa_kernel,
        grid=(b, s // w),
        in_specs=[pl.BlockSpec((1, w, qd), lambda i, j: (i, j, 0)),
                  pl.BlockSpec((1, w, KV_DIM), lambda i, j: (i, j, kcol)),
                  pl.BlockSpec((1, w, KV_DIM), lambda i, j: prev(i, j) + (kcol,)),
                  pl.BlockSpec((1, 1, KV_DIM, VT_SLAB), lambda i, j: (i, j, 0, 0)),
                  pl.BlockSpec((1, 1, KV_DIM, VT_SLAB), lambda i, j: prev(i, j) + (0, 0)),
                  pl.BlockSpec((1, LANES), lambda i, j: (0, 0))],
        out_specs=pl.BlockSpec((1, w, qd), lambda i, j: (i, j, 0)),
        out_shape=jax.ShapeDtypeStruct((b, s, qd), BF16),
        compiler_params=_cparams("parallel", "arbitrary"),
        name="swa_core",
    )(qkv3, qkv3, qkv3, vt, vt, jnp.pad(sinks.astype(F32), (0, LANES - N_HEADS)).reshape(1, LANES))
    return out.reshape(b * s, qd)


def _compress_kernel(f_ref, pe_ref, w1_ref, w2_ref, o_ref, *, transposed):
    half = NSA_CMP_STRIDE * HEAD_DIM
    x = f_ref[0, 0]
    za = (x + pe_ref[:, 0:half]).astype(BF16)
    zb = (x + pe_ref[:, half:2 * half]).astype(BF16)
    hb = _dot(zb, w1_ref[half:2 * half, :])
    hid = _dot(za, w1_ref[0:half, :]) + pltpu.roll(hb, hb.shape[0] - 1, 0)
    act = (hid * _sigmoid(hid)).astype(BF16)
    if transposed:
        o_ref[0, 0] = _dot_nt(w2_ref[...], act).astype(o_ref.dtype)
    else:
        o_ref[0, 0] = _dot(act, w2_ref[...]).astype(o_ref.dtype)


def compress(tok, pe, w1, w2, b, s, transposed):
    nch = s // NSA_CMP_STRIDE
    half = NSA_CMP_STRIDE * HEAD_DIM
    f = tok.reshape(b, nch, NSA_CMP_STRIDE, KV_HEADS, HEAD_DIM).transpose(0, 3, 1, 2, 4).reshape(b, KV_HEADS, nch, half)
    w2 = (w2.T if transposed else w2).astype(BF16)
    oshape = (HEAD_DIM, nch) if transposed else (nch, HEAD_DIM)
    return pl.pallas_call(
        functools.partial(_compress_kernel, transposed=transposed),
        grid=(b, KV_HEADS),
        in_specs=[pl.BlockSpec((1, 1, nch, half), lambda i, j: (i, j, 0, 0)),
                  pl.BlockSpec((1, 2 * half), lambda i, j: (0, 0)),
                  pl.BlockSpec((2 * half, NSA_CMP_HIDDEN), lambda i, j: (0, 0)),
                  pl.BlockSpec(w2.shape, lambda i, j: (0, 0))],
        out_specs=pl.BlockSpec((1, 1) + oshape, lambda i, j: (i, j, 0, 0)),
        out_shape=jax.ShapeDtypeStruct((b, KV_HEADS) + oshape, BF16),
        compiler_params=_cparams("parallel", "parallel"),
        name="nsa_compress",
    )(f, pe.reshape(1, 2 * half).astype(F32), w1.astype(BF16), w2)


def _dot_split3(a_bf16, x):
    hi = x.astype(BF16)
    r1 = x - hi.astype(F32)
    mid = r1.astype(BF16)
    lo = (r1 - mid.astype(F32)).astype(BF16)
    return _dot(a_bf16, hi) + _dot(a_bf16, mid) + _dot(a_bf16, lo)


def _nsa_kernel(q_ref, ks_ref, vst_ref, kw_ref, vwt_ref, kcmp_ref, vcmpt_ref, g_ref, ov_ref,
                o_ref, qt_ref, bias_ref, ocmp_ref, acc_ref, m_ref, l_ref, st_ref, *, seq):
    i = pl.program_id(1)
    qb = NSA_Q_BLOCK
    rq = GQA_R * qb
    n_cmp = seq // NSA_CMP_STRIDE
    n_sel = seq // NSA_SEL_BLOCK
    blocks_per_chunk = NSA_SEL_CHUNK // NSA_SEL_BLOCK
    slabs_per_chunk = NSA_SEL_CHUNK // VT_SLAB
    t0 = i * qb
    qpos = t0 + (lax.broadcasted_iota(jnp.int32, (1, rq), 1) & (qb - 1))
    gates_t = _sigmoid(g_ref[0]).T
    q_t = q_ref[0].astype(F32).T * ATTN_SCALE

    cmp_end = lax.broadcasted_iota(jnp.int32, (n_cmp, rq), 0) * NSA_CMP_STRIDE + (NSA_CMP_LEN - 1)
    mask_c = cmp_end <= qpos
    has_cmp = jnp.where(qpos >= NSA_CMP_LEN - 1, 1.0, 0.0)
    blk = lax.broadcasted_iota(jnp.int32, (n_sel, LANES), 0)
    blk_f = blk.astype(F32)
    forced = (blk == i) | (blk == 0)
    allowed = blk <= i
    lane_lo = lax.broadcasted_iota(jnp.int32, (n_sel, LANES), 1) < qb
    importance = {}
    cw = jnp.maximum(t0 - NSA_WINDOW, 0) // VT_SLAB
    wkeys = NSA_WIN_SLABS * VT_SLAB
    wstart = pl.multiple_of(cw * VT_SLAB, VT_SLAB)
    delta_w = qpos - (wstart + lax.broadcasted_iota(jnp.int32, (wkeys, rq), 0))
    mask_w = (delta_w >= 0) & (delta_w < NSA_WINDOW)
    last_chunk = i // blocks_per_chunk
    key_in_blk = lax.broadcasted_iota(jnp.int32, (NSA_SEL_BLOCK, rq), 0)
    groups = range(KV_HEADS)

    def cmp_start(kv):
        qt = jnp.concatenate(
            [q_t[HEAD_DIM * (GQA_R * kv + r):HEAD_DIM * (GQA_R * kv + r + 1), :] for r in range(GQA_R)],
            axis=1).astype(BF16)
        qt_ref[kv] = qt
        return _dot(kcmp_ref[0, kv], qt)

    def cmp_finish(kv, logits):
        lc = jnp.where(mask_c, logits, MASK_NEG)
        ec = jnp.exp(lc - jnp.max(lc, axis=0, keepdims=True))
        inv = has_cmp / jnp.sum(ec, axis=0, keepdims=True)
        ocmp_ref[kv] = _dot(vcmpt_ref[0, kv], ec.astype(BF16)) * inv
        imp = _dot_split3(ov_ref[...], ec) * inv
        imp = imp + pltpu.roll(imp, qb, 1)
        imp = imp + pltpu.roll(imp, 2 * qb, 1)
        m_ref[kv] = jnp.full((1, rq), MASK_NEG, F32)
        l_ref[kv] = jnp.zeros((1, rq), F32)
        acc_ref[kv] = jnp.zeros((HEAD_DIM, rq), F32)
        importance[kv] = imp[:, 0:LANES]
        if kv % 2 == 1:
            select_pair(kv - 1, kv)

    def select_pair(ka, kb):
        v = jnp.where(lane_lo, importance[ka], importance[kb])
        v = jnp.where(allowed, jnp.where(forced, jnp.inf, v), -jnp.inf)
        sel = blk < 0
        for _ in range(NSA_TOP_N):
            best = jnp.max(v, axis=0, keepdims=True)
            first = jnp.min(jnp.where(v == best, blk_f, float(n_sel)), axis=0, keepdims=True)
            pick = blk_f == first
            sel = sel | pick
            v = jnp.where(pick, -jnp.inf, v)
        bias = jnp.where(sel & allowed, 0.0, MASK_NEG)
        swapped = pltpu.roll(bias, qb, 1)
        for kv, own_lo in ((ka, True), (kb, False)):
            half = jnp.where(lane_lo, bias, swapped) if own_lo else jnp.where(lane_lo, swapped, bias)
            bias_ref[kv] = jnp.concatenate([half, half], axis=1)

    def cmp_phase_start(item):
        return sel_start(0, item[1]) if isinstance(item, tuple) else cmp_start(item)

    def cmp_phase_finish(item, res):
        if isinstance(item, tuple):
            st_ref[item[1]] = res
        else:
            cmp_finish(item, res)

    def sel_start(c, kv):
        k0 = pl.multiple_of(c * NSA_SEL_CHUNK, NSA_SEL_CHUNK)
        return _dot(ks_ref[0, pl.ds(k0, NSA_SEL_CHUNK), HEAD_DIM * kv:HEAD_DIM * (kv + 1)], qt_ref[kv])

    def sel_finish(c, kv, st, causal):
        dsl = slice(HEAD_DIM * kv, HEAD_DIM * (kv + 1))
        k0 = c * NSA_SEL_CHUNK
        bias8 = bias_ref[kv, pl.ds(pl.multiple_of(c * blocks_per_chunk, blocks_per_chunk), blocks_per_chunk), :]
        rows = []
        for jb in range(blocks_per_chunk):
            sj = st[NSA_SEL_BLOCK * jb:NSA_SEL_BLOCK * (jb + 1)] + bias8[jb:jb + 1, :]
            if causal:
                sj = jnp.where(k0 + NSA_SEL_BLOCK * jb + key_in_blk <= qpos, sj, MASK_NEG)
            rows.append(sj)
        sc = jnp.concatenate(rows, axis=0)
        m_run = m_ref[kv]
        m_new = jnp.maximum(m_run, jnp.max(sc, axis=0, keepdims=True))
        p = jnp.exp(sc - m_new)
        alpha = jnp.exp(m_run - m_new)
        vt = jnp.concatenate([vst_ref[0, slabs_per_chunk * c + u, dsl, :] for u in range(slabs_per_chunk)],
                             axis=1)
        m_ref[kv] = m_new
        l_ref[kv] = alpha * l_ref[kv] + jnp.sum(p, axis=0, keepdims=True)
        acc_ref[kv] = alpha * acc_ref[kv] + _dot(vt, p.astype(BF16))

    ahead = NSA_AHEAD
    _one_ahead(list(groups) + [("first_chunk", kv) for kv in range(ahead)], cmp_phase_start, cmp_phase_finish,
               depth=ahead)

    def sel_trip(c, carry):
        pending = [st_ref[k] for k in range(ahead)]
        for kv in groups:
            nk = kv + ahead
            pending.append(sel_start(c, nk) if nk < KV_HEADS else sel_start(c + 1, nk - KV_HEADS))
            sel_finish(c, kv, pending.pop(0), False)
        for k in range(ahead):
            st_ref[k] = pending[k]
        return carry

    lax.fori_loop(0, last_chunk, sel_trip, 0)

    def tail_start(item):
        kv, is_window = item
        if is_window:
            return _dot(kw_ref[0, pl.ds(wstart, wkeys), HEAD_DIM * kv:HEAD_DIM * (kv + 1)], qt_ref[kv])
        return sel_start(last_chunk, kv)

    def tail_finish(item, logits):
        kv, is_window = item
        if not is_window:
            sel_finish(last_chunk, kv, logits, True)
            return
        dsl = slice(HEAD_DIM * kv, HEAD_DIM * (kv + 1))
        lw = jnp.where(mask_w, logits, MASK_NEG)
        ew = jnp.exp(lw - jnp.max(lw, axis=0, keepdims=True))
        vwt = jnp.concatenate([vwt_ref[0, cw + u, dsl, :] for u in range(NSA_WIN_SLABS)], axis=1)
        o_win = _dot(vwt, ew.astype(BF16)) * (1.0 / jnp.sum(ew, axis=0, keepdims=True))
        o_sel = acc_ref[kv] * (1.0 / l_ref[kv])

        def gate_row(branch):
            return jnp.concatenate(
                [gates_t[3 * (GQA_R * kv + r) + branch:3 * (GQA_R * kv + r) + branch + 1, :] for r in range(GQA_R)],
                axis=1)
        o_t = gate_row(0) * ocmp_ref[kv] + gate_row(1) * o_sel + gate_row(2) * o_win
        o_q = o_t.T
        o_ref[0, :, KV_DIM * kv:KV_DIM * (kv + 1)] = jnp.concatenate(
            [o_q[qb * r:qb * (r + 1)] for r in range(GQA_R)], axis=1).astype(o_ref.dtype)

    _one_ahead([(kv, False) for kv in groups] + [(kv, True) for kv in groups], tail_start, tail_finish,
               depth=ahead, primed=[st_ref[k] for k in range(ahead)])


def nsa_core(pa, vt, pb, k_cmp, v_cmp_t, b, s):
    qb = NSA_Q_BLOCK
    qd = N_HEADS * HEAD_DIM
    n_cmp = s // NSA_CMP_STRIDE
    n_sel = s // NSA_SEL_BLOCK
    n_slab = s // VT_SLAB
    pa3 = pa.reshape(b, s, -1)
    pb3 = pb.reshape(b, s, -1)
    vt4 = vt.reshape(b, n_slab, 2 * KV_DIM, VT_SLAB)
    sel_lo = np.arange(n_sel)[:, None] * NSA_SEL_BLOCK
    cmp_lo = np.arange(n_cmp)[None, :] * NSA_CMP_STRIDE
    ov = np.clip(np.minimum(sel_lo + NSA_SEL_BLOCK, cmp_lo + NSA_CMP_LEN) - np.maximum(sel_lo, cmp_lo), 0, None)
    ov = jnp.asarray((ov / NSA_CMP_LEN).astype(np.float32), BF16)
    kcol = qd // KV_DIM
    seqspec = lambda col: pl.BlockSpec((1, s, KV_DIM), lambda i, j: (i, 0, col))
    slabspec = lambda k: pl.BlockSpec((1, n_slab, KV_DIM, VT_SLAB), lambda i, j: (i, 0, k, 0))
    rq = GQA_R * qb
    out = pl.pallas_call(
        functools.partial(_nsa_kernel, seq=s),
        grid=(b, s // qb),
        in_specs=[pl.BlockSpec((1, qb, qd), lambda i, j: (i, j, 0)),
                  seqspec(kcol), slabspec(0), seqspec(kcol + 1), slabspec(1),
                  pl.BlockSpec((1, KV_HEADS, n_cmp, HEAD_DIM), lambda i, j: (i, 0, 0, 0)),
                  pl.BlockSpec((1, KV_HEADS, HEAD_DIM, n_cmp), lambda i, j: (i, 0, 0, 0)),
                  pl.BlockSpec((1, qb, KV_DIM), lambda i, j: (i, j, 2)),
                  pl.BlockSpec((n_sel, n_cmp), lambda i, j: (0, 0))],
        out_specs=pl.BlockSpec((1, qb, qd), lambda i, j: (i, j, 0)),
        out_shape=jax.ShapeDtypeStruct((b, s, qd), BF16),
        scratch_shapes=[pltpu.VMEM((KV_HEADS, HEAD_DIM, rq), BF16),
                        pltpu.VMEM((KV_HEADS, n_sel, rq), F32),
                        pltpu.VMEM((KV_HEADS, HEAD_DIM, rq), F32),
                        pltpu.VMEM((KV_HEADS, HEAD_DIM, rq), F32),
                        pltpu.VMEM((KV_HEADS, 1, rq), F32),
                        pltpu.VMEM((KV_HEADS, 1, rq), F32),
                        pltpu.VMEM((NSA_AHEAD, NSA_SEL_CHUNK, rq), F32)],
        compiler_params=_cparams("parallel", "arbitrary"),
        name="nsa_core",
    )(pa3, pa3, vt4, pa3, vt4, k_cmp, v_cmp_t, pb3, ov)
    return out.reshape(b * s, qd)


def _rope_tables(positions):
    half = ROPE_DIM // 2
    inv_freq = ROPE_THETA ** (-jnp.arange(0, ROPE_DIM, 2, dtype=F32) / ROPE_DIM)
    ang = positions.astype(F32).reshape(-1)[:, None] * inv_freq
    cos, sin = jnp.cos(ang), jnp.sin(ang)
    t = cos.shape[0]
    ones = jnp.ones((t, HEAD_DIM - ROPE_DIM), F32)
    zeros = jnp.zeros((t, HEAD_DIM - ROPE_DIM), F32)
    zh = jnp.zeros((t, half), F32)
    c = jnp.concatenate([cos, cos, ones], axis=1)
    sa = jnp.concatenate([-sin, zh, zeros], axis=1)
    sb = jnp.concatenate([zh, sin, zeros], axis=1)
    return tuple(jnp.tile(v, (1, LANES // HEAD_DIM)) for v in (c, sa, sb))


def _pad_cols(w, n):
    return jnp.pad(w, ((0, 0), (0, n - w.shape[1])))


def mamba2_mixer(h, ln_w, w_in, conv_w, conv_b, dt_bias, a_log, d_skip, norm_w, w_out, b, s):
    w = _pad_cols(w_in, SSM_IN_PAD).astype(BF16)
    zx = norm_proj(h, ln_w, w, jnp.zeros((SSM_IN_PAD,), F32), F32, tn=SSM_IN_TILE)
    y = ssd_core(zx, conv_w, conv_b, dt_bias, a_log, d_skip, norm_w, b, s)
    return out_proj(y, w_out.astype(BF16), jnp.zeros((D_MODEL,), F32), h)


def swa_mixer(h, ln_w, w_qkv, b_qkv, sinks, w_o, b_o, rope, b, s):
    qk = (N_HEADS + KV_HEADS) * HEAD_DIM
    w = w_qkv.astype(BF16)
    bias = b_qkv.astype(F32)
    q_k, vt = norm_proj(h, ln_w, w[:, :qk], bias[:qk], BF16, rope=rope, rope_cols=((0, qk),),
                        w_t=w[:, qk:].T, bias_t=bias[qk:])
    o = swa_core(q_k, vt, sinks, b, s)
    return out_proj(o, w_o.astype(BF16), b_o.astype(F32), h)


def nsa_mixer(h, ln_w, w_in, pe_k, k_w1, k_w2, pe_v, v_w1, v_w2, w_o, rope, b, s):
    qd = N_HEADS * HEAD_DIM
    cols = lambda k: w_in[:, qd + KV_DIM * k:qd + KV_DIM * (k + 1)]
    wa = jnp.concatenate([w_in[:, :qd], cols(2), cols(4)], axis=1).astype(BF16)
    wv_t = jnp.concatenate([cols(3), cols(5)], axis=1).astype(BF16).T
    wb = _pad_cols(jnp.concatenate([cols(0), cols(1), w_in[:, qd + 6 * KV_DIM:]], axis=1), 3 * KV_DIM).astype(BF16)
    pa, vt = norm_proj(h, ln_w, wa, jnp.zeros((wa.shape[1],), F32), BF16, rope=rope,
                       rope_cols=((0, wa.shape[1]),), w_t=wv_t, bias_t=jnp.zeros((wv_t.shape[0],), F32))
    pb = norm_proj(h, ln_w, wb, jnp.zeros((wb.shape[1],), F32), F32, rope=rope, rope_cols=((0, KV_DIM),))
    pb3 = pb.reshape(b, s, -1)
    k_cmp = compress(pb3[..., 0:KV_DIM], pe_k, k_w1, k_w2, b, s, False)
    v_cmp_t = compress(pb3[..., KV_DIM:2 * KV_DIM], pe_v, v_w1, v_w2, b, s, True)
    o = nsa_core(pa, vt, pb, k_cmp, v_cmp_t, b, s)
    return out_proj(o, w_o.astype(BF16), jnp.zeros((D_MODEL,), F32), h)


def kernel(x, positions, ln_ffn1, ffn1_w_in, ffn1_w_out, ln_mix, ln_ffn2, ffn2_w_in, ffn2_w_out, ssm_w_in, ssm_conv_w, ssm_conv_b, ssm_dt_bias, ssm_a_log, ssm_d, ssm_norm_w, ssm_w_out, swa_w_qkv, swa_b_qkv, swa_sinks, swa_w_o, swa_b_o, nsa_w_in, nsa_pe_k, nsa_k_w1, nsa_k_w2, nsa_pe_v, nsa_v_w1, nsa_v_w2, nsa_w_o, final_norm):
    b, s, d = x.shape
    depth = ln_ffn1.shape[0]
    rope = _rope_tables(positions)
    h = x.reshape(b * s, d)
    for i in range(depth):
        kind, inst = i % 3, i // 3
        h = ffn(h, ln_ffn1[i], ffn1_w_in[i], ffn1_w_out[i])
        if kind == 0:
            h = mamba2_mixer(h, ln_mix[i], ssm_w_in[inst], ssm_conv_w[inst], ssm_conv_b[inst],
                             ssm_dt_bias[inst], ssm_a_log[inst], ssm_d[inst], ssm_norm_w[inst],
                             ssm_w_out[inst], b, s)
        elif kind == 1:
            h = swa_mixer(h, ln_mix[i], swa_w_qkv[inst], swa_b_qkv[inst], swa_sinks[inst],
                          swa_w_o[inst], swa_b_o[inst], rope, b, s)
        else:
            h = nsa_mixer(h, ln_mix[i], nsa_w_in[inst], nsa_pe_k[inst], nsa_k_w1[inst], nsa_k_w2[inst],
                          nsa_pe_v[inst], nsa_v_w1[inst], nsa_v_w2[inst], nsa_w_o[inst], rope, b, s)
        h = ffn(h, ln_ffn2[i], ffn2_w_in[i], ffn2_w_out[i], final_norm if i == depth - 1 else None)
    return h.reshape(b, s, d)
```

```python
import functools
import math

import numpy as np
import jax
import jax.numpy as jnp
from jax import lax
from jax.experimental import pallas as pl
from jax.experimental.pallas import tpu as pltpu

F32 = jnp.float32
BF16 = jnp.bfloat16

D_MODEL = 1024
RMS_EPS = 1e-6
D_FF = 2816
HEAD_DIM = 64
ROPE_DIM = HEAD_DIM // 4
ROPE_THETA = 500000.0
ATTN_SCALE = HEAD_DIM ** -0.5
MASK_NEG = -1e30

SSM_D_INNER = 2 * D_MODEL
SSM_HEADS = 32
SSM_GROUPS = 8
SSM_STATE = 128
SSM_CONV = 4
SSM_CHUNK = 128
SSM_CONV_DIM = SSM_D_INNER + 2 * SSM_GROUPS * SSM_STATE
SSM_IN_TILE = 1280
SSM_IN_PAD = 5 * SSM_IN_TILE

N_HEADS = 16
KV_HEADS = 4
GQA_R = N_HEADS // KV_HEADS
KV_DIM = KV_HEADS * HEAD_DIM

SWA_BLOCK = 128

NSA_CMP_LEN = 32
NSA_CMP_STRIDE = 16
NSA_CMP_HIDDEN = 256
NSA_SEL_BLOCK = 64
NSA_TOP_N = 8
NSA_WINDOW = 512
NSA_Q_BLOCK = 64
NSA_SEL_CHUNK = 512
NSA_AHEAD = 2
VT_SLAB = 128
NSA_WIN_SLABS = NSA_WINDOW // VT_SLAB + 1

LANES = 128
VMEM_LIMIT = 56 * 1024 * 1024
TOKEN_TILE = 1024
ROW_SPLITS = 4
FFN_TOKEN_TILE = 1024
FFN_ROW_SPLITS = 4


def _cparams(*sem):
    return pltpu.CompilerParams(dimension_semantics=sem, vmem_limit_bytes=VMEM_LIMIT)


def _dot(a, b):
    return jnp.dot(a, b, preferred_element_type=F32)


def _dot_nt(a, b):
    return lax.dot_general(a, b, (((1,), (1,)), ((), ())), preferred_element_type=F32)


def _sigmoid(x):
    return 1.0 / (1.0 + jnp.exp(-x))


def _rms(x, w):
    return x * lax.rsqrt(jnp.mean(x * x, axis=-1, keepdims=True) + RMS_EPS) * w


def _one_ahead(items, start, finish, depth=1, primed=()):
    items = list(items)
    pending = list(primed)
    nxt = len(pending)
    while nxt < min(depth, len(items)):
        pending.append(start(items[nxt]))
        nxt += 1
    for item in items:
        if nxt < len(items):
            pending.append(start(items[nxt]))
            nxt += 1
        finish(item, pending.pop(0))


def _rope_store(o_ref, rows, acc, rope_segs, c, sa, sb):
    for k, roped in enumerate(rope_segs):
        seg = acc[:, LANES * k:LANES * (k + 1)]
        if roped:
            seg = (seg * c + pltpu.roll(seg, LANES - ROPE_DIM // 2, 1) * sa
                   + pltpu.roll(seg, ROPE_DIM // 2, 1) * sb)
        o_ref[rows, LANES * k:LANES * (k + 1)] = seg.astype(o_ref.dtype)


def _norm_proj_kernel(*refs, rope_segs, col_tiles, with_vt):
    refs = list(refs)
    x_ref, lnw_ref, w_ref, b_ref = refs[:4]
    del refs[:4]
    if rope_segs is not None:
        c_ref, sa_ref, sb_ref = refs[:3]
        del refs[:3]
    if with_vt:
        wt_ref, bt_ref = refs[:2]
        del refs[:2]
    o_ref = refs.pop(0)
    vt_ref = refs.pop(0) if with_vt else None
    xn_ref = refs.pop(0) if col_tiles > 1 else None
    tm = x_ref.shape[0]
    rt = tm // ROW_SPLITS
    row_tiles = [slice(rt * h, rt * (h + 1)) for h in range(ROW_SPLITS)]

    if col_tiles > 1:
        @pl.when(pl.program_id(1) == 0)
        def _():
            xn_ref[...] = _rms(x_ref[...], lnw_ref[...]).astype(BF16)

    def start(rows):
        xn = xn_ref[rows, :] if col_tiles > 1 else _rms(x_ref[rows, :], lnw_ref[...]).astype(BF16)
        acc = _dot(xn, w_ref[pl.program_id(1) if col_tiles > 1 else 0]) + b_ref[...]
        return (acc, _dot_nt(wt_ref[...], xn) + bt_ref[...]) if with_vt else (acc, None)

    def finish(rows, res):
        acc, vt = res
        if rope_segs is None:
            o_ref[rows, :] = acc.astype(o_ref.dtype)
        else:
            _rope_store(o_ref, rows, acc, rope_segs, c_ref[rows, :], sa_ref[rows, :], sb_ref[rows, :])
        if with_vt:
            for u in range(rt // VT_SLAB):
                vt_ref[rows.start // VT_SLAB + u] = vt[:, VT_SLAB * u:VT_SLAB * (u + 1)].astype(vt_ref.dtype)

    _one_ahead(row_tiles, start, finish)


def norm_proj(h, ln_w, w, bias, out_dtype, tn=None, rope=None, rope_cols=(), w_t=None, bias_t=None):
    t, d = h.shape
    n = w.shape[1]
    tm = min(TOKEN_TILE, t)
    tn = n if tn is None else tn
    col_tiles = n // tn
    assert n == col_tiles * tn and t % tm == 0 and tn % LANES == 0
    with_vt = w_t is not None
    rope_segs = None
    if rope_cols:
        assert col_tiles == 1
        rope_segs = tuple(any(lo <= LANES * k < hi for lo, hi in rope_cols) for k in range(n // LANES))
    w_tiles = w.reshape(d, col_tiles, tn).transpose(1, 0, 2)
    in_specs = [pl.BlockSpec((tm, d), lambda i, j: (i, 0)),
                pl.BlockSpec((1, d), lambda i, j: (0, 0)),
                pl.BlockSpec((col_tiles, d, tn), lambda i, j: (0, 0, 0), pipeline_mode=pl.Buffered(1)),
                pl.BlockSpec((1, tn), lambda i, j: (0, j))]
    args = [h, ln_w.reshape(1, d), w_tiles, bias.reshape(1, n)]
    if rope_segs is not None:
        in_specs += [pl.BlockSpec((tm, LANES), lambda i, j: (i, 0))] * 3
        args += list(rope)
    out_specs = [pl.BlockSpec((tm, tn), lambda i, j: (i, j))]
    out_shape = [jax.ShapeDtypeStruct((t, n), out_dtype)]
    if with_vt:
        assert col_tiles == 1
        nv = w_t.shape[0]
        in_specs += [pl.BlockSpec((nv, d), lambda i, j: (0, 0)), pl.BlockSpec((nv, 1), lambda i, j: (0, 0))]
        args += [w_t, bias_t.reshape(nv, 1)]
        out_specs.append(pl.BlockSpec((tm // VT_SLAB, nv, VT_SLAB), lambda i, j: (i, 0, 0)))
        out_shape.append(jax.ShapeDtypeStruct((t // VT_SLAB, nv, VT_SLAB), BF16))
    res = pl.pallas_call(
        functools.partial(_norm_proj_kernel, rope_segs=rope_segs, col_tiles=col_tiles, with_vt=with_vt),
        grid=(t // tm, col_tiles),
        in_specs=in_specs,
        out_specs=out_specs,
        out_shape=out_shape,
        scratch_shapes=[pltpu.VMEM((tm, d), BF16)] if col_tiles > 1 else [],
        compiler_params=_cparams("parallel", "arbitrary"),
        name="norm_proj",
    )(*args)
    return tuple(res) if with_vt else res[0]


def _out_proj_kernel(y_ref, w_ref, b_ref, r_ref, o_ref):
    o_ref[...] = r_ref[...] + _dot(y_ref[...], w_ref[...]) + b_ref[...]


def out_proj(y, w, bias, res):
    t, k = y.shape
    d = w.shape[1]
    tm = min(TOKEN_TILE, t)
    return pl.pallas_call(
        _out_proj_kernel,
        grid=(t // tm,),
        in_specs=[pl.BlockSpec((tm, k), lambda i: (i, 0)),
                  pl.BlockSpec((k, d), lambda i: (0, 0)),
                  pl.BlockSpec((1, d), lambda i: (0, 0)),
                  pl.BlockSpec((tm, d), lambda i: (i, 0))],
        out_specs=pl.BlockSpec((tm, d), lambda i: (i, 0)),
        out_shape=jax.ShapeDtypeStruct((t, d), F32),
        compiler_params=_cparams("parallel"),
        name="out_proj",
    )(y, w, bias.reshape(1, d), res)


def _ffn_kernel(x_ref, lnw_ref, wi_ref, wo_ref, fnw_ref, o_ref, *, final_norm):
    tm = x_ref.shape[0]
    rt = tm // FFN_ROW_SPLITS
    dff = wo_ref.shape[0]

    def start(rows):
        xn = _rms(x_ref[rows, :], lnw_ref[...]).astype(BF16)
        return _dot(xn, wi_ref[:, 0:dff]), _dot(xn, wi_ref[:, dff:2 * dff])

    def finish(rows, gu):
        g, u = gu
        y = x_ref[rows, :] + 0.5 * _dot((g * _sigmoid(g) * u).astype(BF16), wo_ref[...])
        if final_norm:
            y = _rms(y, fnw_ref[...])
        o_ref[rows, :] = y

    _one_ahead([slice(rt * h, rt * (h + 1)) for h in range(FFN_ROW_SPLITS)], start, finish)


def ffn(h, ln_w, w_in, w_out, final_w=None):
    t, d = h.shape
    tm = min(FFN_TOKEN_TILE, t)
    final_norm = final_w is not None
    fnw = (final_w if final_norm else ln_w).reshape(1, d)
    resident = lambda shape: pl.BlockSpec(shape, lambda i: (0, 0), pipeline_mode=pl.Buffered(1))
    return pl.pallas_call(
        functools.partial(_ffn_kernel, final_norm=final_norm),
        grid=(t // tm,),
        in_specs=[pl.BlockSpec((tm, d), lambda i: (i, 0)),
                  pl.BlockSpec((1, d), lambda i: (0, 0)),
                  resident(w_in.shape),
                  resident(w_out.shape),
                  pl.BlockSpec((1, d), lambda i: (0, 0))],
        out_specs=pl.BlockSpec((tm, d), lambda i: (i, 0)),
        out_shape=jax.ShapeDtypeStruct((t, d), F32),
        compiler_params=_cparams("parallel"),
        name="ffn",
    )(h, ln_w.reshape(1, d), w_in.astype(BF16), w_out.astype(BF16), fnw)


def _expand_heads(mat, g):
    rows = mat.shape[0]
    lane = lax.broadcasted_iota(jnp.int32, (rows, LANES), 1)
    pieces = []
    for p in range(2):
        h0 = GQA_R * g + 2 * p
        a = jnp.broadcast_to(mat[:, h0:h0 + 1], (rows, LANES))
        b = jnp.broadcast_to(mat[:, h0 + 1:h0 + 2], (rows, LANES))
        pieces.append(jnp.where(lane < HEAD_DIM, a, b))
    return jnp.concatenate(pieces, axis=1)


def _ssd_kernel(z_ref, x_ref, bc_ref, dt_ref, cw_ref, cb_ref, dtb_ref, alog_ref, dsk_ref, nw_ref,
                tril_ref, o_ref, state_ref, xbuf_ref):
    q = SSM_CHUNK
    gw = GQA_R * HEAD_DIM
    c = pl.program_id(1)

    @pl.when(c == 0)
    def _():
        state_ref[...] = jnp.zeros(state_ref.shape, F32)
        xbuf_ref[0:8, :] = jnp.zeros((8, SSM_CONV_DIM), F32)

    xbuf_ref[8:8 + q, 0:SSM_D_INNER] = x_ref[0]
    xbuf_ref[8:8 + q, SSM_D_INNER:SSM_CONV_DIM] = bc_ref[0]

    def conv_silu(c0, width):
        acc = cb_ref[:, c0:c0 + width]
        for k in reversed(range(SSM_CONV)):
            r0 = 8 - (SSM_CONV - 1) + k
            acc = acc + cw_ref[k:k + 1, c0:c0 + width] * xbuf_ref[r0:r0 + q, c0:c0 + width]
        return acc * _sigmoid(acc)

    dtr = dt_ref[0] + dtb_ref[...]
    e_neg = jnp.exp(-jnp.abs(dtr))
    u = 1.0 + e_neg
    log_u = jnp.log2(u) * math.log(2.0)
    dt = jnp.maximum(dtr, 0.0) + jnp.where(u == 1.0, e_neg, log_u * (e_neg / (u - 1.0)))
    ad = dt * (-jnp.exp(alog_ref[...]))
    acum = jnp.dot(tril_ref[...], ad, precision=lax.Precision.HIGHEST,
                   preferred_element_type=F32)
    acum_row = acum.T
    causal = (lax.broadcasted_iota(jnp.int32, (q, q), 0) >= lax.broadcasted_iota(jnp.int32, (q, q), 1))
    head_of_lane = lax.broadcasted_iota(jnp.int32, (q, 2 * LANES), 1) // HEAD_DIM

    def start(g):
        xs = conv_silu(gw * g, gw)
        bm = conv_silu(SSM_D_INNER + SSM_STATE * g, SSM_STATE)
        cm = conv_silu(SSM_D_INNER + SSM_GROUPS * SSM_STATE + SSM_STATE * g, SSM_STATE)
        xd = xs * _expand_heads(dt, g)
        cbf = cm.astype(BF16)
        st = state_ref[g]
        return xs, bm, xd, st, _dot_nt(cbf, bm.astype(BF16)), _dot(cbf, st.astype(BF16))

    def finish(g, vals):
        xs, bm, xd, st, cb, y_off = vals
        cs = slice(gw * g, gw * (g + 1))
        acum_ch = _expand_heads(acum, g)
        a_last = acum_ch[q - 1:q, :]
        xdb = xd.astype(BF16)
        y_diag = None
        for r in range(GQA_R):
            hh = GQA_R * g + r
            diff = acum[:, hh:hh + 1] - acum_row[hh:hh + 1, :]
            decay = jnp.exp(jnp.where(causal, diff, -jnp.inf))
            y_r = _dot((cb * decay).astype(BF16), xdb)
            y_diag = y_r if r == 0 else jnp.where(head_of_lane == r, y_r, y_diag)
        state_ref[g] = (st * jnp.exp(a_last)
                        + _dot(bm.T.astype(BF16), (xd * jnp.exp(a_last - acum_ch)).astype(BF16)))
        y = y_off * jnp.exp(acum_ch) + y_diag + dsk_ref[:, cs] * xs
        zz = z_ref[0, :, cs]
        gated = y * (zz * _sigmoid(zz))
        o_ref[0, :, cs] = _rms(gated, nw_ref[:, cs]).astype(o_ref.dtype)

    _one_ahead(range(SSM_GROUPS), start, finish)
    xbuf_ref[0:8, :] = xbuf_ref[q:q + 8, :]


def ssd_core(zx, conv_w, conv_b, dt_bias, a_log, d_skip, norm_w, b, s):
    q = SSM_CHUNK
    zx3 = zx.reshape(b, s, SSM_IN_PAD)
    pad = lambda v: jnp.pad(v.astype(F32), (0, LANES - SSM_HEADS)).reshape(1, LANES)
    tril = jnp.asarray(np.tril(np.ones((q, q), np.float32)))
    wide = SSM_D_INNER
    full = lambda shape: pl.BlockSpec(shape, lambda i, c: (0,) * len(shape))
    out = pl.pallas_call(
        _ssd_kernel,
        grid=(b, s // q),
        in_specs=[pl.BlockSpec((1, q, wide), lambda i, c: (i, c, 0)),
                  pl.BlockSpec((1, q, wide), lambda i, c: (i, c, 1)),
                  pl.BlockSpec((1, q, wide), lambda i, c: (i, c, 2)),
                  pl.BlockSpec((1, q, LANES), lambda i, c: (i, c, (SSM_D_INNER + SSM_CONV_DIM) // LANES)),
                  full((SSM_CONV, SSM_CONV_DIM)), full((1, SSM_CONV_DIM)),
                  full((1, LANES)), full((1, LANES)), full((1, wide)), full((1, wide)),
                  full((q, q))],
        out_specs=pl.BlockSpec((1, q, wide), lambda i, c: (i, c, 0)),
        out_shape=jax.ShapeDtypeStruct((b, s, wide), BF16),
        scratch_shapes=[pltpu.VMEM((SSM_GROUPS, SSM_STATE, GQA_R * HEAD_DIM), F32),
                        pltpu.VMEM((q + 8, SSM_CONV_DIM), F32)],
        compiler_params=_cparams("parallel", "arbitrary"),
        name="ssd_core",
    )(zx3, zx3, zx3, zx3, conv_w.astype(F32), conv_b.reshape(1, -1).astype(F32),
      pad(dt_bias), pad(a_log), jnp.repeat(d_skip.astype(F32), HEAD_DIM).reshape(1, wide),
      norm_w.reshape(1, wide).astype(F32), tril)
    return out.reshape(b * s, wide)


def _swa_kernel(q_ref, kc_ref, kp_ref, vtc_ref, vtp_ref, sink_ref, o_ref):
    i = pl.program_id(1)
    w = SWA_BLOCK
    rq = GQA_R * w
    lane = lax.broadcasted_iota(jnp.int32, (1, rq), 1)
    qloc = lane & (w - 1)
    head_of_lane = lane // w
    kk = lax.broadcasted_iota(jnp.int32, (2 * w, rq), 0)
    valid = ((kk < w) & (kk > qloc) & (i > 0)) | ((kk >= w) & ((kk - w) <= qloc))
    scale = jnp.asarray(ATTN_SCALE, BF16)

    def start(kv):
        dsl = slice(HEAD_DIM * kv, HEAD_DIM * (kv + 1))
        qs = jnp.concatenate(
            [q_ref[0, :, HEAD_DIM * (GQA_R * kv + r):HEAD_DIM * (GQA_R * kv + r + 1)] for r in range(GQA_R)],
            axis=0) * scale
        k = jnp.concatenate([kp_ref[0, :, dsl], kc_ref[0, :, dsl]], axis=0)
        return _dot_nt(k, qs)

    def finish(kv, st):
        dsl = slice(HEAD_DIM * kv, HEAD_DIM * (kv + 1))
        sink = jnp.zeros((1, rq), F32)
        for r in range(GQA_R):
            h = GQA_R * kv + r
            sink = jnp.where(head_of_lane == r, sink_ref[:, h:h + 1], sink)
        sc = jnp.where(valid, st, -jnp.inf)
        m = jnp.maximum(jnp.max(sc, axis=0, keepdims=True), sink)
        e = jnp.exp(sc - m)
        den = jnp.sum(e, axis=0, keepdims=True) + jnp.exp(sink - m)
        vt = jnp.concatenate([vtp_ref[0, 0, dsl, :], vtc_ref[0, 0, dsl, :]], axis=1)
        o_q = (_dot(vt, e.astype(BF16)) * (1.0 / den)).T
        o_ref[0, :, KV_DIM * kv:KV_DIM * (kv + 1)] = jnp.concatenate(
            [o_q[w * r:w * (r + 1)] for r in range(GQA_R)], axis=1).astype(o_ref.dtype)

    _one_ahead(range(KV_HEADS), start, finish, depth=2)


def swa_core(qk, vt, sinks, b, s):
    w = SWA_BLOCK
    qkv3 = qk.reshape(b, s, -1)
    qd = N_HEADS * HEAD_DIM
    kcol = qd // KV_DIM
    vt = vt.reshape(b, s // VT_SLAB, KV_DIM, VT_SLAB)
    prev = lambda i, j: (i, jnp.maximum(j - 1, 0))
    out = pl.pallas_call(
        _swa_kernel,
        grid=(b, s // w),
        in_specs=[pl.BlockSpec((1, w, qd), lambda i, j: (i, j, 0)),
                  pl.BlockSpec((1, w, KV_DIM), lambda i, j: (i, j, kcol)),
                  pl.BlockSpec((1, w, KV_DIM), lambda i, j: prev(i, j) + (kcol,)),
                  pl.BlockSpec((1, 1, KV_DIM, VT_SLAB), lambda i, j: (i, j, 0, 0)),
                  pl.BlockSpec((1, 1, KV_DIM, VT_SLAB), lambda i, j: prev(i, j) + (0, 0)),
                  pl.BlockSpec((1, LANES), lambda i, j: (0, 0))],
        out_specs=pl.BlockSpec((1, w, qd), lambda i, j: (i, j, 0)),
        out_shape=jax.ShapeDtypeStruct((b, s, qd), BF16),
        compiler_params=_cparams("parallel", "arbitrary"),
        name="swa_core",
    )(qkv3, qkv3, qkv3, vt, vt, jnp.pad(sinks.astype(F32), (0, LANES - N_HEADS)).reshape(1, LANES))
    return out.reshape(b * s, qd)


def _compress_kernel(f_ref, pe_ref, w1_ref, w2_ref, o_ref, *, transposed):
    half = NSA_CMP_STRIDE * HEAD_DIM
    x = f_ref[0, 0]
    za = (x + pe_ref[:, 0:half]).astype(BF16)
    zb = (x + pe_ref[:, half:2 * half]).astype(BF16)
    hb = _dot(zb, w1_ref[half:2 * half, :])
    hid = _dot(za, w1_ref[0:half, :]) + pltpu.roll(hb, hb.shape[0] - 1, 0)
    act = (hid * _sigmoid(hid)).astype(BF16)
    if transposed:
        o_ref[0, 0] = _dot_nt(w2_ref[...], act).astype(o_ref.dtype)
    else:
        o_ref[0, 0] = _dot(act, w2_ref[...]).astype(o_ref.dtype)


def compress(tok, pe, w1, w2, b, s, transposed):
    nch = s // NSA_CMP_STRIDE
    half = NSA_CMP_STRIDE * HEAD_DIM
    f = tok.reshape(b, nch, NSA_CMP_STRIDE, KV_HEADS, HEAD_DIM).transpose(0, 3, 1, 2, 4).reshape(b, KV_HEADS, nch, half)
    w2 = (w2.T if transposed else w2).astype(BF16)
    oshape = (HEAD_DIM, nch) if transposed else (nch, HEAD_DIM)
    return pl.pallas_call(
        functools.partial(_compress_kernel, transposed=transposed),
        grid=(b, KV_HEADS),
        in_specs=[pl.BlockSpec((1, 1, nch, half), lambda i, j: (i, j, 0, 0)),
                  pl.BlockSpec((1, 2 * half), lambda i, j: (0, 0)),
                  pl.BlockSpec((2 * half, NSA_CMP_HIDDEN), lambda i, j: (0, 0)),
                  pl.BlockSpec(w2.shape, lambda i, j: (0, 0))],
        out_specs=pl.BlockSpec((1, 1) + oshape, lambda i, j: (i, j, 0, 0)),
        out_shape=jax.ShapeDtypeStruct((b, KV_HEADS) + oshape, BF16),
        compiler_params=_cparams("parallel", "parallel"),
        name="nsa_compress",
    )(f, pe.reshape(1, 2 * half).astype(F32), w1.astype(BF16), w2)


def _dot_split3(a_bf16, x):
    hi = x.astype(BF16)
    r1 = x - hi.astype(F32)
    mid = r1.astype(BF16)
    lo = (r1 - mid.astype(F32)).astype(BF16)
    return _dot(a_bf16, hi) + _dot(a_bf16, mid) + _dot(a_bf16, lo)


def _nsa_kernel(q_ref, ks_ref, vst_ref, kw_ref, vwt_ref, kcmp_ref, vcmpt_ref, g_ref, ov_ref,
                o_ref, qt_ref, bias_ref, ocmp_ref, acc_ref, m_ref, l_ref, st_ref, *, seq):
    i = pl.program_id(1)
    qb = NSA_Q_BLOCK
    rq = GQA_R * qb
    n_cmp = seq // NSA_CMP_STRIDE
    n_sel = seq // NSA_SEL_BLOCK
    blocks_per_chunk = NSA_SEL_CHUNK // NSA_SEL_BLOCK
    slabs_per_chunk = NSA_SEL_CHUNK // VT_SLAB
    t0 = i * qb
    qpos = t0 + (lax.broadcasted_iota(jnp.int32, (1, rq), 1) & (qb - 1))
    gates_t = _sigmoid(g_ref[0]).T
    q_t = q_ref[0].astype(F32).T * ATTN_SCALE

    cmp_end = lax.broadcasted_iota(jnp.int32, (n_cmp, rq), 0) * NSA_CMP_STRIDE + (NSA_CMP_LEN - 1)
    mask_c = cmp_end <= qpos
    has_cmp = jnp.where(qpos >= NSA_CMP_LEN - 1, 1.0, 0.0)
    blk = lax.broadcasted_iota(jnp.int32, (n_sel, LANES), 0)
    blk_f = blk.astype(F32)
    forced = (blk == i) | (blk == 0)
    allowed = blk <= i
    lane_lo = lax.broadcasted_iota(jnp.int32, (n_sel, LANES), 1) < qb
    importance = {}
    cw = jnp.maximum(t0 - NSA_WINDOW, 0) // VT_SLAB
    wkeys = NSA_WIN_SLABS * VT_SLAB
    wstart = pl.multiple_of(cw * VT_SLAB, VT_SLAB)
    delta_w = qpos - (wstart + lax.broadcasted_iota(jnp.int32, (wkeys, rq), 0))
    mask_w = (delta_w >= 0) & (delta_w < NSA_WINDOW)
    last_chunk = i // blocks_per_chunk
    key_in_blk = lax.broadcasted_iota(jnp.int32, (NSA_SEL_BLOCK, rq), 0)
    groups = range(KV_HEADS)

    def cmp_start(kv):
        qt = jnp.concatenate(
            [q_t[HEAD_DIM * (GQA_R * kv + r):HEAD_DIM * (GQA_R * kv + r + 1), :] for r in range(GQA_R)],
            axis=1).astype(BF16)
        qt_ref[kv] = qt
        return _dot(kcmp_ref[0, kv], qt)

    def cmp_finish(kv, logits):
        lc = jnp.where(mask_c, logits, MASK_NEG)
        ec = jnp.exp(lc - jnp.max(lc, axis=0, keepdims=True))
        inv = has_cmp / jnp.sum(ec, axis=0, keepdims=True)
        ocmp_ref[kv] = _dot(vcmpt_ref[0, kv], ec.astype(BF16)) * inv
        imp = _dot_split3(ov_ref[...], ec) * inv
        imp = imp + pltpu.roll(imp, qb, 1)
        imp = imp + pltpu.roll(imp, 2 * qb, 1)
        m_ref[kv] = jnp.full((1, rq), MASK_NEG, F32)
        l_ref[kv] = jnp.zeros((1, rq), F32)
        acc_ref[kv] = jnp.zeros((HEAD_DIM, rq), F32)
        importance[kv] = imp[:, 0:LANES]
        if kv % 2 == 1:
            select_pair(kv - 1, kv)

    def select_pair(ka, kb):
        v = jnp.where(lane_lo, importance[ka], importance[kb])
        v = jnp.where(allowed, jnp.where(forced, jnp.inf, v), -jnp.inf)
        sel = blk < 0
        for _ in range(NSA_TOP_N):
            best = jnp.max(v, axis=0, keepdims=True)
            first = jnp.min(jnp.where(v == best, blk_f, float(n_sel)), axis=0, keepdims=True)
            pick = blk_f == first
            sel = sel | pick
            v = jnp.where(pick, -jnp.inf, v)
        bias = jnp.where(sel & allowed, 0.0, MASK_NEG)
        swapped = pltpu.roll(bias, qb, 1)
        for kv, own_lo in ((ka, True), (kb, False)):
            half = jnp.where(lane_lo, bias, swapped) if own_lo else jnp.where(lane_lo, swapped, bias)
            bias_ref[kv] = jnp.concatenate([half, half], axis=1)

    def cmp_phase_start(item):
        return sel_start(0, item[1]) if isinstance(item, tuple) else cmp_start(item)

    def cmp_phase_finish(item, res):
        if isinstance(item, tuple):
            st_ref[item[1]] = res
        else:
            cmp_finish(item, res)

    def sel_start(c, kv):
        k0 = pl.multiple_of(c * NSA_SEL_CHUNK, NSA_SEL_CHUNK)
        return _dot(ks_ref[0, pl.ds(k0, NSA_SEL_CHUNK), HEAD_DIM * kv:HEAD_DIM * (kv + 1)], qt_ref[kv])

    def sel_finish(c, kv, st, causal):
        dsl = slice(HEAD_DIM * kv, HEAD_DIM * (kv + 1))
        k0 = c * NSA_SEL_CHUNK
        bias8 = bias_ref[kv, pl.ds(pl.multiple_of(c * blocks_per_chunk, blocks_per_chunk), blocks_per_chunk), :]
        rows = []
        for jb in range(blocks_per_chunk):
            sj = st[NSA_SEL_BLOCK * jb:NSA_SEL_BLOCK * (jb + 1)] + bias8[jb:jb + 1, :]
            if causal:
                sj = jnp.where(k0 + NSA_SEL_BLOCK * jb + key_in_blk <= qpos, sj, MASK_NEG)
            rows.append(sj)
        sc = jnp.concatenate(rows, axis=0)
        m_run = m_ref[kv]
        m_new = jnp.maximum(m_run, jnp.max(sc, axis=0, keepdims=True))
        p = jnp.exp(sc - m_new)
        alpha = jnp.exp(m_run - m_new)
        vt = jnp.concatenate([vst_ref[0, slabs_per_chunk * c + u, dsl, :] for u in range(slabs_per_chunk)],
                             axis=1)
        m_ref[kv] = m_new
        l_ref[kv] = alpha * l_ref[kv] + jnp.sum(p, axis=0, keepdims=True)
        acc_ref[kv] = alpha * acc_ref[kv] + _dot(vt, p.astype(BF16))

    ahead = NSA_AHEAD
    _one_ahead(list(groups) + [("first_chunk", kv) for kv in range(ahead)], cmp_phase_start, cmp_phase_finish,
               depth=ahead)

    def sel_trip(c, carry):
        pending = [st_ref[k] for k in range(ahead)]
        for kv in groups:
            nk = kv + ahead
            pending.append(sel_start(c, nk) if nk < KV_HEADS else sel_start(c + 1, nk - KV_HEADS))
            sel_finish(c, kv, pending.pop(0), False)
        for k in range(ahead):
            st_ref[k] = pending[k]
        return carry

    lax.fori_loop(0, last_chunk, sel_trip, 0)

    def tail_start(item):
        kv, is_window = item
        if is_window:
            return _dot(kw_ref[0, pl.ds(wstart, wkeys), HEAD_DIM * kv:HEAD_DIM * (kv + 1)], qt_ref[kv])
        return sel_start(last_chunk, kv)

    def tail_finish(item, logits):
        kv, is_window = item
        if not is_window:
            sel_finish(last_chunk, kv, logits, True)
            return
        dsl = slice(HEAD_DIM * kv, HEAD_DIM * (kv + 1))
        lw = jnp.where(mask_w, logits, MASK_NEG)
        ew = jnp.exp(lw - jnp.max(lw, axis=0, keepdims=True))
        vwt = jnp.concatenate([vwt_ref[0, cw + u, dsl, :] for u in range(NSA_WIN_SLABS)], axis=1)
        o_win = _dot(vwt, ew.astype(BF16)) * (1.0 / jnp.sum(ew, axis=0, keepdims=True))
        o_sel = acc_ref[kv] * (1.0 / l_ref[kv])

        def gate_row(branch):
            return jnp.concatenate(
                [gates_t[3 * (GQA_R * kv + r) + branch:3 * (GQA_R * kv + r) + branch + 1, :] for r in range(GQA_R)],
                axis=1)
        o_t = gate_row(0) * ocmp_ref[kv] + gate_row(1) * o_sel + gate_row(2) * o_win
        o_q = o_t.T
        o_ref[0, :, KV_DIM * kv:KV_DIM * (kv + 1)] = jnp.concatenate(
            [o_q[qb * r:qb * (r + 1)] for r in range(GQA_R)], axis=1).astype(o_ref.dtype)

    _one_ahead([(kv, False) for kv in groups] + [(kv, True) for kv in groups], tail_start, tail_finish,
               depth=ahead, primed=[st_ref[k] for k in range(ahead)])


def nsa_core(pa, vt, pb, k_cmp, v_cmp_t, b, s):
    qb = NSA_Q_BLOCK
    qd = N_HEADS * HEAD_DIM
    n_cmp = s // NSA_CMP_STRIDE
    n_sel = s // NSA_SEL_BLOCK
    n_slab = s // VT_SLAB
    pa3 = pa.reshape(b, s, -1)
    pb3 = pb.reshape(b, s, -1)
    vt4 = vt.reshape(b, n_slab, 2 * KV_DIM, VT_SLAB)
    sel_lo = np.arange(n_sel)[:, None] * NSA_SEL_BLOCK
    cmp_lo = np.arange(n_cmp)[None, :] * NSA_CMP_STRIDE
    ov = np.clip(np.minimum(sel_lo + NSA_SEL_BLOCK, cmp_lo + NSA_CMP_LEN) - np.maximum(sel_lo, cmp_lo), 0, None)
    ov = jnp.asarray((ov / NSA_CMP_LEN).astype(np.float32), BF16)
    kcol = qd // KV_DIM
    seqspec = lambda col: pl.BlockSpec((1, s, KV_DIM), lambda i, j: (i, 0, col))
    slabspec = lambda k: pl.BlockSpec((1, n_slab, KV_DIM, VT_SLAB), lambda i, j: (i, 0, k, 0))
    rq = GQA_R * qb
    out = pl.pallas_call(
        functools.partial(_nsa_kernel, seq=s),
        grid=(b, s // qb),
        in_specs=[pl.BlockSpec((1, qb, qd), lambda i, j: (i, j, 0)),
                  seqspec(kcol), slabspec(0), seqspec(kcol + 1), slabspec(1),
                  pl.BlockSpec((1, KV_HEADS, n_cmp, HEAD_DIM), lambda i, j: (i, 0, 0, 0)),
                  pl.BlockSpec((1, KV_HEADS, HEAD_DIM, n_cmp), lambda i, j: (i, 0, 0, 0)),
                  pl.BlockSpec((1, qb, KV_DIM), lambda i, j: (i, j, 2)),
                  pl.BlockSpec((n_sel, n_cmp), lambda i, j: (0, 0))],
        out_specs=pl.BlockSpec((1, qb, qd), lambda i, j: (i, j, 0)),
        out_shape=jax.ShapeDtypeStruct((b, s, qd), BF16),
        scratch_shapes=[pltpu.VMEM((KV_HEADS, HEAD_DIM, rq), BF16),
                        pltpu.VMEM((KV_HEADS, n_sel, rq), F32),
                        pltpu.VMEM((KV_HEADS, HEAD_DIM, rq), F32),
                        pltpu.VMEM((KV_HEADS, HEAD_DIM, rq), F32),
                        pltpu.VMEM((KV_HEADS, 1, rq), F32),
                        pltpu.VMEM((KV_HEADS, 1, rq), F32),
                        pltpu.VMEM((NSA_AHEAD, NSA_SEL_CHUNK, rq), F32)],
        compiler_params=_cparams("parallel", "arbitrary"),
        name="nsa_core",
    )(pa3, pa3, vt4, pa3, vt4, k_cmp, v_cmp_t, pb3, ov)
    return out.reshape(b * s, qd)


def _rope_tables(positions):
    half = ROPE_DIM // 2
    inv_freq = ROPE_THETA ** (-jnp.arange(0, ROPE_DIM, 2, dtype=F32) / ROPE_DIM)
    ang = positions.astype(F32).reshape(-1)[:, None] * inv_freq
    cos, sin = jnp.cos(ang), jnp.sin(ang)
    t = cos.shape[0]
    ones = jnp.ones((t, HEAD_DIM - ROPE_DIM), F32)
    zeros = jnp.zeros((t, HEAD_DIM - ROPE_DIM), F32)
    zh = jnp.zeros((t, half), F32)
    c = jnp.concatenate([cos, cos, ones], axis=1)
    sa = jnp.concatenate([-sin, zh, zeros], axis=1)
    sb = jnp.concatenate([zh, sin, zeros], axis=1)
    return tuple(jnp.tile(v, (1, LANES // HEAD_DIM)) for v in (c, sa, sb))


def _pad_cols(w, n):
    return jnp.pad(w, ((0, 0), (0, n - w.shape[1])))


def mamba2_mixer(h, ln_w, w_in, conv_w, conv_b, dt_bias, a_log, d_skip, norm_w, w_out, b, s):
    w = _pad_cols(w_in, SSM_IN_PAD).astype(BF16)
    zx = norm_proj(h, ln_w, w, jnp.zeros((SSM_IN_PAD,), F32), F32, tn=SSM_IN_TILE)
    y = ssd_core(zx, conv_w, conv_b, dt_bias, a_log, d_skip, norm_w, b, s)
    return out_proj(y, w_out.astype(BF16), jnp.zeros((D_MODEL,), F32), h)


def swa_mixer(h, ln_w, w_qkv, b_qkv, sinks, w_o, b_o, rope, b, s):
    qk = (N_HEADS + KV_HEADS) * HEAD_DIM
    w = w_qkv.astype(BF16)
    bias = b_qkv.astype(F32)
    q_k, vt = norm_proj(h, ln_w, w[:, :qk], bias[:qk], BF16, rope=rope, rope_cols=((0, qk),),
                        w_t=w[:, qk:].T, bias_t=bias[qk:])
    o = swa_core(q_k, vt, sinks, b, s)
    return out_proj(o, w_o.astype(BF16), b_o.astype(F32), h)


def nsa_mixer(h, ln_w, w_in, pe_k, k_w1, k_w2, pe_v, v_w1, v_w2, w_o, rope, b, s):
    qd = N_HEADS * HEAD_DIM
    cols = lambda k: w_in[:, qd + KV_DIM * k:qd + KV_DIM * (k + 1)]
    wa = jnp.concatenate([w_in[:, :qd], cols(2), cols(4)], axis=1).astype(BF16)
    wv_t = jnp.concatenate([cols(3), cols(5)], axis=1).astype(BF16).T
    wb = _pad_cols(jnp.concatenate([cols(0), cols(1), w_in[:, qd + 6 * KV_DIM:]], axis=1), 3 * KV_DIM).astype(BF16)
    pa, vt = norm_proj(h, ln_w, wa, jnp.zeros((wa.shape[1],), F32), BF16, rope=rope,
                       rope_cols=((0, wa.shape[1]),), w_t=wv_t, bias_t=jnp.zeros((wv_t.shape[0],), F32))
    pb = norm_proj(h, ln_w, wb, jnp.zeros((wb.shape[1],), F32), F32, rope=rope, rope_cols=((0, KV_DIM),))
    pb3 = pb.reshape(b, s, -1)
    k_cmp = compress(pb3[..., 0:KV_DIM], pe_k, k_w1, k_w2, b, s, False)
    v_cmp_t = compress(pb3[..., KV_DIM:2 * KV_DIM], pe_v, v_w1, v_w2, b, s, True)
    o = nsa_core(pa, vt, pb, k_cmp, v_cmp_t, b, s)
    return out_proj(o, w_o.astype(BF16), jnp.zeros((D_MODEL,), F32), h)


def kernel(x, positions, ln_ffn1, ffn1_w_in, ffn1_w_out, ln_mix, ln_ffn2, ffn2_w_in, ffn2_w_out, ssm_w_in, ssm_conv_w, ssm_conv_b, ssm_dt_bias, ssm_a_log, ssm_d, ssm_norm_w, ssm_w_out, swa_w_qkv, swa_b_qkv, swa_sinks, swa_w_o, swa_b_o, nsa_w_in, nsa_pe_k, nsa_k_w1, nsa_k_w2, nsa_pe_v, nsa_v_w1, nsa_v_w2, nsa_w_o, final_norm):
    b, s, d = x.shape
    depth = ln_ffn1.shape[0]
    rope = _rope_tables(positions)
    h = x.reshape(b * s, d)
    for i in range(depth):
        kind, inst = i % 3, i // 3
        h = ffn(h, ln_ffn1[i], ffn1_w_in[i], ffn1_w_out[i])
        if kind == 0:
            h = mamba2_mixer(h, ln_mix[i], ssm_w_in[inst], ssm_conv_w[inst], ssm_conv_b[inst],
                             ssm_dt_bias[inst], ssm_a_log[inst], ssm_d[inst], ssm_norm_w[inst],
                             ssm_w_out[inst], b, s)
        elif kind == 1:
            h = swa_mixer(h, ln_mix[i], swa_w_qkv[inst], swa_b_qkv[inst], swa_sinks[inst],
                          swa_w_o[inst], swa_b_o[inst], rope, b, s)
        else:
            h = nsa_mixer(h, ln_mix[i], nsa_w_in[inst], nsa_pe_k[inst], nsa_k_w1[inst], nsa_k_w2[inst],
                          nsa_pe_v[inst], nsa_v_w1[inst], nsa_v_w2[inst], nsa_w_o[inst], rope, b, s)
        h = ffn(h, ln_ffn2[i], ffn2_w_in[i], ffn2_w_out[i], final_norm if i == depth - 1 else None)
    return h.reshape(b, s, d)
```

```python
import functools
import math

import numpy as np
import jax
import jax.numpy as jnp
from jax import lax
from jax.experimental import pallas as pl
from jax.experimental.pallas import tpu as pltpu

F32 = jnp.float32
BF16 = jnp.bfloat16

D_MODEL = 1024
RMS_EPS = 1e-6
D_FF = 2816
HEAD_DIM = 64
ROPE_DIM = HEAD_DIM // 4
ROPE_THETA = 500000.0
ATTN_SCALE = HEAD_DIM ** -0.5
MASK_NEG = -1e30

SSM_D_INNER = 2 * D_MODEL
SSM_HEADS = 32
SSM_GROUPS = 8
SSM_STATE = 128
SSM_CONV = 4
SSM_CHUNK = 128
SSM_CONV_DIM = SSM_D_INNER + 2 * SSM_GROUPS * SSM_STATE
SSM_IN_PAD = SSM_D_INNER + SSM_CONV_DIM + 256
SSM_IN_TOKEN_TILE = 512

N_HEADS = 16
KV_HEADS = 4
GQA_R = N_HEADS // KV_HEADS
KV_DIM = KV_HEADS * HEAD_DIM

SWA_BLOCK = 128

NSA_CMP_LEN = 32
NSA_CMP_STRIDE = 16
NSA_CMP_HIDDEN = 256
NSA_SEL_BLOCK = 64
NSA_TOP_N = 8
NSA_WINDOW = 512
NSA_Q_BLOCK = 64
NSA_SEL_CHUNK = 512
NSA_AHEAD = 2
VT_SLAB = 128
NSA_WIN_SLABS = NSA_WINDOW // VT_SLAB + 1

LANES = 128
VMEM_LIMIT = 56 * 1024 * 1024
TOKEN_TILE = 1024
ROW_SPLITS = 4
FFN_TOKEN_TILE = 1024
FFN_ROW_SPLITS = 4


def _cparams(*sem):
    return pltpu.CompilerParams(dimension_semantics=sem, vmem_limit_bytes=VMEM_LIMIT)


def _dot(a, b):
    return jnp.dot(a, b, preferred_element_type=F32)


def _dot_nt(a, b):
    return lax.dot_general(a, b, (((1,), (1,)), ((), ())), preferred_element_type=F32)


def _sigmoid(x):
    return 1.0 / (1.0 + jnp.exp(-x))


def _rms(x, w):
    return x * lax.rsqrt(jnp.mean(x * x, axis=-1, keepdims=True) + RMS_EPS) * w


def _one_ahead(items, start, finish, depth=1, primed=()):
    items = list(items)
    pending = list(primed)
    nxt = len(pending)
    while nxt < min(depth, len(items)):
        pending.append(start(items[nxt]))
        nxt += 1
    for item in items:
        if nxt < len(items):
            pending.append(start(items[nxt]))
            nxt += 1
        finish(item, pending.pop(0))


def _rope_store(o_ref, rows, acc, rope_segs, c, sa, sb):
    for k, roped in enumerate(rope_segs):
        seg = acc[:, LANES * k:LANES * (k + 1)]
        if roped:
            seg = (seg * c + pltpu.roll(seg, LANES - ROPE_DIM // 2, 1) * sa
                   + pltpu.roll(seg, ROPE_DIM // 2, 1) * sb)
        o_ref[rows, LANES * k:LANES * (k + 1)] = seg.astype(o_ref.dtype)


def _norm_proj_kernel(*refs, rope_segs, col_tiles, with_vt, row_splits):
    refs = list(refs)
    x_ref, lnw_ref, w_ref, b_ref = refs[:4]
    del refs[:4]
    if rope_segs is not None:
        c_ref, sa_ref, sb_ref = refs[:3]
        del refs[:3]
    if with_vt:
        wt_ref, bt_ref = refs[:2]
        del refs[:2]
    o_ref = refs.pop(0)
    vt_ref = refs.pop(0) if with_vt else None
    xn_ref = refs.pop(0) if col_tiles > 1 else None
    tm = x_ref.shape[0]
    rt = tm // row_splits
    row_tiles = [slice(rt * h, rt * (h + 1)) for h in range(row_splits)]

    if col_tiles > 1:
        @pl.when(pl.program_id(1) == 0)
        def _():
            xn_ref[...] = _rms(x_ref[...], lnw_ref[...]).astype(BF16)

    def start(rows):
        xn = xn_ref[rows, :] if col_tiles > 1 else _rms(x_ref[rows, :], lnw_ref[...]).astype(BF16)
        acc = _dot(xn, w_ref[pl.program_id(1) if col_tiles > 1 else 0]) + b_ref[...]
        return (acc, _dot_nt(wt_ref[...], xn) + bt_ref[...]) if with_vt else (acc, None)

    def finish(rows, res):
        acc, vt = res
        if rope_segs is None:
            o_ref[rows, :] = acc.astype(o_ref.dtype)
        else:
            _rope_store(o_ref, rows, acc, rope_segs, c_ref[rows, :], sa_ref[rows, :], sb_ref[rows, :])
        if with_vt:
            for u in range(rt // VT_SLAB):
                vt_ref[rows.start // VT_SLAB + u] = vt[:, VT_SLAB * u:VT_SLAB * (u + 1)].astype(vt_ref.dtype)

    _one_ahead(row_tiles, start, finish)


def norm_proj(h, ln_w, w, bias, out_dtype, tn=None, rope=None, rope_cols=(), w_t=None, bias_t=None,
              tm=TOKEN_TILE, row_splits=ROW_SPLITS):
    t, d = h.shape
    n = w.shape[1]
    tm = min(tm, t)
    tn = n if tn is None else tn
    col_tiles = n // tn
    assert n == col_tiles * tn and t % tm == 0 and tn % LANES == 0
    with_vt = w_t is not None
    rope_segs = None
    if rope_cols:
        assert col_tiles == 1
        rope_segs = tuple(any(lo <= LANES * k < hi for lo, hi in rope_cols) for k in range(n // LANES))
    w_tiles = w.reshape(d, col_tiles, tn).transpose(1, 0, 2)
    in_specs = [pl.BlockSpec((tm, d), lambda i, j: (i, 0)),
                pl.BlockSpec((1, d), lambda i, j: (0, 0)),
                pl.BlockSpec((col_tiles, d, tn), lambda i, j: (0, 0, 0), pipeline_mode=pl.Buffered(1)),
                pl.BlockSpec((1, tn), lambda i, j: (0, j))]
    args = [h, ln_w.reshape(1, d), w_tiles, bias.reshape(1, n)]
    if rope_segs is not None:
        in_specs += [pl.BlockSpec((tm, LANES), lambda i, j: (i, 0))] * 3
        args += list(rope)
    out_specs = [pl.BlockSpec((tm, tn), lambda i, j: (i, j))]
    out_shape = [jax.ShapeDtypeStruct((t, n), out_dtype)]
    if with_vt:
        assert col_tiles == 1
        nv = w_t.shape[0]
        in_specs += [pl.BlockSpec((nv, d), lambda i, j: (0, 0)), pl.BlockSpec((nv, 1), lambda i, j: (0, 0))]
        args += [w_t, bias_t.reshape(nv, 1)]
        out_specs.append(pl.BlockSpec((tm // VT_SLAB, nv, VT_SLAB), lambda i, j: (i, 0, 0)))
        out_shape.append(jax.ShapeDtypeStruct((t // VT_SLAB, nv, VT_SLAB), BF16))
    res = pl.pallas_call(
        functools.partial(_norm_proj_kernel, rope_segs=rope_segs, col_tiles=col_tiles, with_vt=with_vt,
                          row_splits=row_splits),
        grid=(t // tm, col_tiles),
        in_specs=in_specs,
        out_specs=out_specs,
        out_shape=out_shape,
        scratch_shapes=[pltpu.VMEM((tm, d), BF16)] if col_tiles > 1 else [],
        compiler_params=_cparams("parallel", "arbitrary"),
        name="norm_proj",
    )(*args)
    return tuple(res) if with_vt else res[0]


def _out_proj_kernel(y_ref, w_ref, b_ref, r_ref, o_ref):
    o_ref[...] = r_ref[...] + _dot(y_ref[...], w_ref[...]) + b_ref[...]


def out_proj(y, w, bias, res):
    t, k = y.shape
    d = w.shape[1]
    tm = min(TOKEN_TILE, t)
    return pl.pallas_call(
        _out_proj_kernel,
        grid=(t // tm,),
        in_specs=[pl.BlockSpec((tm, k), lambda i: (i, 0)),
                  pl.BlockSpec((k, d), lambda i: (0, 0)),
                  pl.BlockSpec((1, d), lambda i: (0, 0)),
                  pl.BlockSpec((tm, d), lambda i: (i, 0))],
        out_specs=pl.BlockSpec((tm, d), lambda i: (i, 0)),
        out_shape=jax.ShapeDtypeStruct((t, d), F32),
        compiler_params=_cparams("parallel"),
        name="out_proj",
    )(y, w, bias.reshape(1, d), res)


def _ffn_kernel(x_ref, lnw_ref, wi_ref, wo_ref, fnw_ref, o_ref, *, final_norm):
    tm = x_ref.shape[0]
    rt = tm // FFN_ROW_SPLITS
    dff = wo_ref.shape[0]

    def start(rows):
        xn = _rms(x_ref[rows, :], lnw_ref[...]).astype(BF16)
        return _dot(xn, wi_ref[:, 0:dff]), _dot(xn, wi_ref[:, dff:2 * dff])

    def finish(rows, gu):
        g, u = gu
        y = x_ref[rows, :] + 0.5 * _dot((g * _sigmoid(g) * u).astype(BF16), wo_ref[...])
        if final_norm:
            y = _rms(y, fnw_ref[...])
        o_ref[rows, :] = y

    _one_ahead([slice(rt * h, rt * (h + 1)) for h in range(FFN_ROW_SPLITS)], start, finish)


def ffn(h, ln_w, w_in, w_out, final_w=None):
    t, d = h.shape
    tm = min(FFN_TOKEN_TILE, t)
    final_norm = final_w is not None
    fnw = (final_w if final_norm else ln_w).reshape(1, d)
    resident = lambda shape: pl.BlockSpec(shape, lambda i: (0, 0), pipeline_mode=pl.Buffered(1))
    return pl.pallas_call(
        functools.partial(_ffn_kernel, final_norm=final_norm),
        grid=(t // tm,),
        in_specs=[pl.BlockSpec((tm, d), lambda i: (i, 0)),
                  pl.BlockSpec((1, d), lambda i: (0, 0)),
                  resident(w_in.shape),
                  resident(w_out.shape),
                  pl.BlockSpec((1, d), lambda i: (0, 0))],
        out_specs=pl.BlockSpec((tm, d), lambda i: (i, 0)),
        out_shape=jax.ShapeDtypeStruct((t, d), F32),
        compiler_params=_cparams("parallel"),
        name="ffn",
    )(h, ln_w.reshape(1, d), w_in.astype(BF16), w_out.astype(BF16), fnw)


def _expand_heads(mat, g):
    rows = mat.shape[0]
    lane = lax.broadcasted_iota(jnp.int32, (rows, LANES), 1)
    pieces = []
    for p in range(2):
        h0 = GQA_R * g + 2 * p
        a = jnp.broadcast_to(mat[:, h0:h0 + 1], (rows, LANES))
        b = jnp.broadcast_to(mat[:, h0 + 1:h0 + 2], (rows, LANES))
        pieces.append(jnp.where(lane < HEAD_DIM, a, b))
    return jnp.concatenate(pieces, axis=1)


def _ssd_kernel(z_ref, x_ref, bc_ref, dt_ref, cw_ref, cb_ref, dtb_ref, alog_ref, dsk_ref, nw_ref,
                tril_ref, o_ref, state_ref, xbuf_ref):
    q = SSM_CHUNK
    gw = GQA_R * HEAD_DIM
    c = pl.program_id(1)

    @pl.when(c == 0)
    def _():
        state_ref[...] = jnp.zeros(state_ref.shape, F32)
        xbuf_ref[0:8, :] = jnp.zeros((8, SSM_CONV_DIM), F32)

    xbuf_ref[8:8 + q, 0:SSM_D_INNER] = x_ref[0]
    xbuf_ref[8:8 + q, SSM_D_INNER:SSM_CONV_DIM] = bc_ref[0]

    def conv_silu(c0, width):
        acc = cb_ref[:, c0:c0 + width]
        for k in reversed(range(SSM_CONV)):
            r0 = 8 - (SSM_CONV - 1) + k
            acc = acc + cw_ref[k:k + 1, c0:c0 + width] * xbuf_ref[r0:r0 + q, c0:c0 + width]
        return acc * _sigmoid(acc)

    dtr = dt_ref[0] + dtb_ref[...]
    e_neg = jnp.exp(-jnp.abs(dtr))
    u = 1.0 + e_neg
    log_u = jnp.log2(u) * math.log(2.0)
    dt = jnp.maximum(dtr, 0.0) + jnp.where(u == 1.0, e_neg, log_u * (e_neg / (u - 1.0)))
    ad = dt * (-jnp.exp(alog_ref[...]))
    acum = jnp.dot(tril_ref[...], ad, precision=lax.Precision.HIGHEST,
                   preferred_element_type=F32)
    acum_row = acum.T
    causal = (lax.broadcasted_iota(jnp.int32, (q, q), 0) >= lax.broadcasted_iota(jnp.int32, (q, q), 1))
    head_of_lane = lax.broadcasted_iota(jnp.int32, (q, 2 * LANES), 1) // HEAD_DIM

    def start(g):
        xs = conv_silu(gw * g, gw)
        bm = conv_silu(SSM_D_INNER + SSM_STATE * g, SSM_STATE)
        cm = conv_silu(SSM_D_INNER + SSM_GROUPS * SSM_STATE + SSM_STATE * g, SSM_STATE)
        xd = xs * _expand_heads(dt, g)
        cbf = cm.astype(BF16)
        st = state_ref[g]
        return xs, bm, xd, st, _dot_nt(cbf, bm.astype(BF16)), _dot(cbf, st.astype(BF16))

    def finish(g, vals):
        xs, bm, xd, st, cb, y_off = vals
        cs = slice(gw * g, gw * (g + 1))
        acum_ch = _expand_heads(acum, g)
        a_last = acum_ch[q - 1:q, :]
        xdb = xd.astype(BF16)
        y_diag = None
        for r in range(GQA_R):
            hh = GQA_R * g + r
            diff = acum[:, hh:hh + 1] - acum_row[hh:hh + 1, :]
            decay = jnp.exp(jnp.where(causal, diff, -jnp.inf))
            y_r = _dot((cb * decay).astype(BF16), xdb)
            y_diag = y_r if r == 0 else jnp.where(head_of_lane == r, y_r, y_diag)
        state_ref[g] = (st * jnp.exp(a_last)
                        + _dot(bm.T.astype(BF16), (xd * jnp.exp(a_last - acum_ch)).astype(BF16)))
        y = y_off * jnp.exp(acum_ch) + y_diag + dsk_ref[:, cs] * xs
        zz = z_ref[0, :, cs]
        gated = y * (zz * _sigmoid(zz))
        o_ref[0, :, cs] = _rms(gated, nw_ref[:, cs]).astype(o_ref.dtype)

    _one_ahead(range(SSM_GROUPS), start, finish)
    xbuf_ref[0:8, :] = xbuf_ref[q:q + 8, :]


def ssd_core(zx, conv_w, conv_b, dt_bias, a_log, d_skip, norm_w, b, s):
    q = SSM_CHUNK
    zx3 = zx.reshape(b, s, SSM_IN_PAD)
    pad = lambda v: jnp.pad(v.astype(F32), (0, LANES - SSM_HEADS)).reshape(1, LANES)
    tril = jnp.asarray(np.tril(np.ones((q, q), np.float32)))
    wide = SSM_D_INNER
    full = lambda shape: pl.BlockSpec(shape, lambda i, c: (0,) * len(shape))
    out = pl.pallas_call(
        _ssd_kernel,
        grid=(b, s // q),
        in_specs=[pl.BlockSpec((1, q, wide), lambda i, c: (i, c, 0)),
                  pl.BlockSpec((1, q, wide), lambda i, c: (i, c, 1)),
                  pl.BlockSpec((1, q, wide), lambda i, c: (i, c, 2)),
                  pl.BlockSpec((1, q, LANES), lambda i, c: (i, c, (SSM_D_INNER + SSM_CONV_DIM) // LANES)),
                  full((SSM_CONV, SSM_CONV_DIM)), full((1, SSM_CONV_DIM)),
                  full((1, LANES)), full((1, LANES)), full((1, wide)), full((1, wide)),
                  full((q, q))],
        out_specs=pl.BlockSpec((1, q, wide), lambda i, c: (i, c, 0)),
        out_shape=jax.ShapeDtypeStruct((b, s, wide), BF16),
        scratch_shapes=[pltpu.VMEM((SSM_GROUPS, SSM_STATE, GQA_R * HEAD_DIM), F32),
                        pltpu.VMEM((q + 8, SSM_CONV_DIM), F32)],
        compiler_params=_cparams("parallel", "arbitrary"),
        name="ssd_core",
    )(zx3, zx3, zx3, zx3, conv_w.astype(F32), conv_b.reshape(1, -1).astype(F32),
      pad(dt_bias), pad(a_log), jnp.repeat(d_skip.astype(F32), HEAD_DIM).reshape(1, wide),
      norm_w.reshape(1, wide).astype(F32), tril)
    return out.reshape(b * s, wide)


def _swa_kernel(q_ref, kc_ref, kp_ref, vtc_ref, vtp_ref, sink_ref, o_ref):
    i = pl.program_id(1)
    w = SWA_BLOCK
    rq = GQA_R * w
    lane = lax.broadcasted_iota(jnp.int32, (1, rq), 1)
    qloc = lane & (w - 1)
    head_of_lane = lane // w
    kk = lax.broadcasted_iota(jnp.int32, (2 * w, rq), 0)
    valid = ((kk < w) & (kk > qloc) & (i > 0)) | ((kk >= w) & ((kk - w) <= qloc))
    scale = jnp.asarray(ATTN_SCALE, BF16)

    def start(kv):
        dsl = slice(HEAD_DIM * kv, HEAD_DIM * (kv + 1))
        qs = jnp.concatenate(
            [q_ref[0, :, HEAD_DIM * (GQA_R * kv + r):HEAD_DIM * (GQA_R * kv + r + 1)] for r in range(GQA_R)],
            axis=0) * scale
        k = jnp.concatenate([kp_ref[0, :, dsl], kc_ref[0, :, dsl]], axis=0)
        return _dot_nt(k, qs)

    def finish(kv, st):
        dsl = slice(HEAD_DIM * kv, HEAD_DIM * (kv + 1))
        sink = jnp.zeros((1, rq), F32)
        for r in range(GQA_R):
            h = GQA_R * kv + r
            sink = jnp.where(head_of_lane == r, sink_ref[:, h:h + 1], sink)
        sc = jnp.where(valid, st, -jnp.inf)
        m = jnp.maximum(jnp.max(sc, axis=0, keepdims=True), sink)
        e = jnp.exp(sc - m)
        den = jnp.sum(e, axis=0, keepdims=True) + jnp.exp(sink - m)
        vt = jnp.concatenate([vtp_ref[0, 0, dsl, :], vtc_ref[0, 0, dsl, :]], axis=1)
        o_q = (_dot(vt, e.astype(BF16)) * (1.0 / den)).T
        o_ref[0, :, KV_DIM * kv:KV_DIM * (kv + 1)] = jnp.concatenate(
            [o_q[w * r:w * (r + 1)] for r in range(GQA_R)], axis=1).astype(o_ref.dtype)

    _one_ahead(range(KV_HEADS), start, finish, depth=2)


def swa_core(qk, vt, sinks, b, s):
    w = SWA_BLOCK
    qkv3 = qk.reshape(b, s, -1)
    qd = N_HEADS * HEAD_DIM
    kcol = qd // KV_DIM
    vt = vt.reshape(b, s // VT_SLAB, KV_DIM, VT_SLAB)
    prev = lambda i, j: (i, jnp.maximum(j - 1, 0))
    out = pl.pallas_call(
        _swa_kernel,
        grid=(b, s // w),
        in_specs=[pl.BlockSpec((1, w, qd), lambda i, j: (i, j, 0)),
                  pl.BlockSpec((1, w, KV_DIM), lambda i, j: (i, j, kcol)),
                  pl.BlockSpec((1, w, KV_DIM), lambda i, j: prev(i, j) + (kcol,)),
                  pl.BlockSpec((1, 1, KV_DIM, VT_SLAB), lambda i, j: (i, j, 0, 0)),
                  pl.BlockSpec((1, 1, KV_DIM, VT_SLAB), lambda i, j: prev(i, j) + (0, 0)),
                  pl.BlockSpec((1, LANES), lambda i, j: (0, 0))],
        out_specs=pl.BlockSpec((1, w, qd), lambda i, j: (i, j, 0)),
        out_shape=jax.ShapeDtypeStruct((b, s, qd), BF16),
        compiler_params=_cparams("parallel", "arbitrary"),
        name="swa_core",
    )(qkv3, qkv3, qkv3, vt, vt, jnp.pad(sinks.astype(F32), (0, LANES - N_HEADS)).reshape(1, LANES))
    return out.reshape(b * s, qd)


def _compress_kernel(f_ref, pe_ref, w1_ref, w2_ref, o_ref, *, transposed):
    half = NSA_CMP_STRIDE * HEAD_DIM
    x = f_ref[0, 0]
    za = (x + pe_ref[:, 0:half]).astype(BF16)
    zb = (x + pe_ref[:, half:2 * half]).astype(BF16)
    hb = _dot(zb, w1_ref[half:2 * half, :])
    hid = _dot(za, w1_ref[0:half, :]) + pltpu.roll(hb, hb.shape[0] - 1, 0)
    act = (hid * _sigmoid(hid)).astype(BF16)
    if transposed:
        o_ref[0, 0] = _dot_nt(w2_ref[...], act).astype(o_ref.dtype)
    else:
        o_ref[0, 0] = _dot(act, w2_ref[...]).astype(o_ref.dtype)


def compress(tok, pe, w1, w2, b, s, transposed):
    nch = s // NSA_CMP_STRIDE
    half = NSA_CMP_STRIDE * HEAD_DIM
    f = tok.reshape(b, nch, NSA_CMP_STRIDE, KV_HEADS, HEAD_DIM).transpose(0, 3, 1, 2, 4).reshape(b, KV_HEADS, nch, half)
    w2 = (w2.T if transposed else w2).astype(BF16)
    oshape = (HEAD_DIM, nch) if transposed else (nch, HEAD_DIM)
    return pl.pallas_call(
        functools.partial(_compress_kernel, transposed=transposed),
        grid=(b, KV_HEADS),
        in_specs=[pl.BlockSpec((1, 1, nch, half), lambda i, j: (i, j, 0, 0)),
                  pl.BlockSpec((1, 2 * half), lambda i, j: (0, 0)),
                  pl.BlockSpec((2 * half, NSA_CMP_HIDDEN), lambda i, j: (0, 0)),
                  pl.BlockSpec(w2.shape, lambda i, j: (0, 0))],
        out_specs=pl.BlockSpec((1, 1) + oshape, lambda i, j: (i, j, 0, 0)),
        out_shape=jax.ShapeDtypeStruct((b, KV_HEADS) + oshape, BF16),
        compiler_params=_cparams("parallel", "parallel"),
        name="nsa_compress",
    )(f, pe.reshape(1, 2 * half).astype(F32), w1.astype(BF16), w2)


def _dot_split3(a_bf16, x):
    hi = x.astype(BF16)
    r1 = x - hi.astype(F32)
    mid = r1.astype(BF16)
    lo = (r1 - mid.astype(F32)).astype(BF16)
    return _dot(a_bf16, hi) + _dot(a_bf16, mid) + _dot(a_bf16, lo)


def _nsa_kernel(q_ref, ks_ref, vst_ref, kw_ref, vwt_ref, kcmp_ref, vcmpt_ref, g_ref, ov_ref,
                o_ref, qt_ref, bias_ref, ocmp_ref, acc_ref, m_ref, l_ref, st_ref, *, seq):
    i = pl.program_id(1)
    qb = NSA_Q_BLOCK
    rq = GQA_R * qb
    n_cmp = seq // NSA_CMP_STRIDE
    n_sel = seq // NSA_SEL_BLOCK
    blocks_per_chunk = NSA_SEL_CHUNK // NSA_SEL_BLOCK
    slabs_per_chunk = NSA_SEL_CHUNK // VT_SLAB
    t0 = i * qb
    qpos = t0 + (lax.broadcasted_iota(jnp.int32, (1, rq), 1) & (qb - 1))
    gates_t = _sigmoid(g_ref[0]).T
    q_t = q_ref[0].astype(F32).T * ATTN_SCALE

    cmp_end = lax.broadcasted_iota(jnp.int32, (n_cmp, rq), 0) * NSA_CMP_STRIDE + (NSA_CMP_LEN - 1)
    mask_c = cmp_end <= qpos
    has_cmp = jnp.where(qpos >= NSA_CMP_LEN - 1, 1.0, 0.0)
    blk = lax.broadcasted_iota(jnp.int32, (n_sel, LANES), 0)
    blk_f = blk.astype(F32)
    forced = (blk == i) | (blk == 0)
    allowed = blk <= i
    lane_lo = lax.broadcasted_iota(jnp.int32, (n_sel, LANES), 1) < qb
    importance = {}
    cw = jnp.maximum(t0 - NSA_WINDOW, 0) // VT_SLAB
    wkeys = NSA_WIN_SLABS * VT_SLAB
    wstart = pl.multiple_of(cw * VT_SLAB, VT_SLAB)
    delta_w = qpos - (wstart + lax.broadcasted_iota(jnp.int32, (wkeys, rq), 0))
    mask_w = (delta_w >= 0) & (delta_w < NSA_WINDOW)
    last_chunk = i // blocks_per_chunk
    key_in_blk = lax.broadcasted_iota(jnp.int32, (NSA_SEL_BLOCK, rq), 0)
    groups = range(KV_HEADS)

    def cmp_start(kv):
        qt = jnp.concatenate(
            [q_t[HEAD_DIM * (GQA_R * kv + r):HEAD_DIM * (GQA_R * kv + r + 1), :] for r in range(GQA_R)],
            axis=1).astype(BF16)
        qt_ref[kv] = qt
        return _dot(kcmp_ref[0, kv], qt)

    def cmp_finish(kv, logits):
        lc = jnp.where(mask_c, logits, MASK_NEG)
        ec = jnp.exp(lc - jnp.max(lc, axis=0, keepdims=True))
        inv = has_cmp / jnp.sum(ec, axis=0, keepdims=True)
        ocmp_ref[kv] = _dot(vcmpt_ref[0, kv], ec.astype(BF16)) * inv
        imp = _dot_split3(ov_ref[...], ec) * inv
        imp = imp + pltpu.roll(imp, qb, 1)
        imp = imp + pltpu.roll(imp, 2 * qb, 1)
        m_ref[kv] = jnp.full((1, rq), MASK_NEG, F32)
        l_ref[kv] = jnp.zeros((1, rq), F32)
        acc_ref[kv] = jnp.zeros((HEAD_DIM, rq), F32)
        importance[kv] = imp[:, 0:LANES]
        if kv % 2 == 1:
            select_pair(kv - 1, kv)

    def select_pair(ka, kb):
        v = jnp.where(lane_lo, importance[ka], importance[kb])
        v = jnp.where(allowed, jnp.where(forced, jnp.inf, v), -jnp.inf)
        sel = blk < 0
        for _ in range(NSA_TOP_N):
            best = jnp.max(v, axis=0, keepdims=True)
            first = jnp.min(jnp.where(v == best, blk_f, float(n_sel)), axis=0, keepdims=True)
            pick = blk_f == first
            sel = sel | pick
            v = jnp.where(pick, -jnp.inf, v)
        bias = jnp.where(sel & allowed, 0.0, MASK_NEG)
        swapped = pltpu.roll(bias, qb, 1)
        for kv, own_lo in ((ka, True), (kb, False)):
            half = jnp.where(lane_lo, bias, swapped) if own_lo else jnp.where(lane_lo, swapped, bias)
            bias_ref[kv] = jnp.concatenate([half, half], axis=1)

    def cmp_phase_start(item):
        return sel_start(0, item[1]) if isinstance(item, tuple) else cmp_start(item)

    def cmp_phase_finish(item, res):
        if isinstance(item, tuple):
            st_ref[item[1]] = res
        else:
            cmp_finish(item, res)

    def sel_start(c, kv):
        k0 = pl.multiple_of(c * NSA_SEL_CHUNK, NSA_SEL_CHUNK)
        return _dot(ks_ref[0, pl.ds(k0, NSA_SEL_CHUNK), HEAD_DIM * kv:HEAD_DIM * (kv + 1)], qt_ref[kv])

    def sel_finish(c, kv, st, causal):
        dsl = slice(HEAD_DIM * kv, HEAD_DIM * (kv + 1))
        k0 = c * NSA_SEL_CHUNK
        bias8 = bias_ref[kv, pl.ds(pl.multiple_of(c * blocks_per_chunk, blocks_per_chunk), blocks_per_chunk), :]
        rows = []
        for jb in range(blocks_per_chunk):
            sj = st[NSA_SEL_BLOCK * jb:NSA_SEL_BLOCK * (jb + 1)] + bias8[jb:jb + 1, :]
            if causal:
                sj = jnp.where(k0 + NSA_SEL_BLOCK * jb + key_in_blk <= qpos, sj, MASK_NEG)
            rows.append(sj)
        sc = jnp.concatenate(rows, axis=0)
        m_run = m_ref[kv]
        m_new = jnp.maximum(m_run, jnp.max(sc, axis=0, keepdims=True))
        p = jnp.exp(sc - m_new)
        alpha = jnp.exp(m_run - m_new)
        vt = jnp.concatenate([vst_ref[0, slabs_per_chunk * c + u, dsl, :] for u in range(slabs_per_chunk)],
                             axis=1)
        m_ref[kv] = m_new
        l_ref[kv] = alpha * l_ref[kv] + jnp.sum(p, axis=0, keepdims=True)
        acc_ref[kv] = alpha * acc_ref[kv] + _dot(vt, p.astype(BF16))

    ahead = NSA_AHEAD
    _one_ahead(list(groups) + [("first_chunk", kv) for kv in range(ahead)], cmp_phase_start, cmp_phase_finish,
               depth=ahead)

    def sel_trip(c, carry):
        pending = [st_ref[k] for k in range(ahead)]
        for kv in groups:
            nk = kv + ahead
            pending.append(sel_start(c, nk) if nk < KV_HEADS else sel_start(c + 1, nk - KV_HEADS))
            sel_finish(c, kv, pending.pop(0), False)
        for k in range(ahead):
            st_ref[k] = pending[k]
        return carry

    lax.fori_loop(0, last_chunk, sel_trip, 0)

    def tail_start(item):
        kv, is_window = item
        if is_window:
            return _dot(kw_ref[0, pl.ds(wstart, wkeys), HEAD_DIM * kv:HEAD_DIM * (kv + 1)], qt_ref[kv])
        return sel_start(last_chunk, kv)

    def tail_finish(item, logits):
        kv, is_window = item
        if not is_window:
            sel_finish(last_chunk, kv, logits, True)
            return
        dsl = slice(HEAD_DIM * kv, HEAD_DIM * (kv + 1))
        lw = jnp.where(mask_w, logits, MASK_NEG)
        ew = jnp.exp(lw - jnp.max(lw, axis=0, keepdims=True))
        vwt = jnp.concatenate([vwt_ref[0, cw + u, dsl, :] for u in range(NSA_WIN_SLABS)], axis=1)
        o_win = _dot(vwt, ew.astype(BF16)) * (1.0 / jnp.sum(ew, axis=0, keepdims=True))
        o_sel = acc_ref[kv] * (1.0 / l_ref[kv])

        def gate_row(branch):
            return jnp.concatenate(
                [gates_t[3 * (GQA_R * kv + r) + branch:3 * (GQA_R * kv + r) + branch + 1, :] for r in range(GQA_R)],
                axis=1)
        o_t = gate_row(0) * ocmp_ref[kv] + gate_row(1) * o_sel + gate_row(2) * o_win
        o_q = o_t.T
        o_ref[0, :, KV_DIM * kv:KV_DIM * (kv + 1)] = jnp.concatenate(
            [o_q[qb * r:qb * (r + 1)] for r in range(GQA_R)], axis=1).astype(o_ref.dtype)

    _one_ahead([(kv, False) for kv in groups] + [(kv, True) for kv in groups], tail_start, tail_finish,
               depth=ahead, primed=[st_ref[k] for k in range(ahead)])


def nsa_core(pa, vt, pb, k_cmp, v_cmp_t, b, s):
    qb = NSA_Q_BLOCK
    qd = N_HEADS * HEAD_DIM
    n_cmp = s // NSA_CMP_STRIDE
    n_sel = s // NSA_SEL_BLOCK
    n_slab = s // VT_SLAB
    pa3 = pa.reshape(b, s, -1)
    pb3 = pb.reshape(b, s, -1)
    vt4 = vt.reshape(b, n_slab, 2 * KV_DIM, VT_SLAB)
    sel_lo = np.arange(n_sel)[:, None] * NSA_SEL_BLOCK
    cmp_lo = np.arange(n_cmp)[None, :] * NSA_CMP_STRIDE
    ov = np.clip(np.minimum(sel_lo + NSA_SEL_BLOCK, cmp_lo + NSA_CMP_LEN) - np.maximum(sel_lo, cmp_lo), 0, None)
    ov = jnp.asarray((ov / NSA_CMP_LEN).astype(np.float32), BF16)
    kcol = qd // KV_DIM
    seqspec = lambda col: pl.BlockSpec((1, s, KV_DIM), lambda i, j: (i, 0, col))
    slabspec = lambda k: pl.BlockSpec((1, n_slab, KV_DIM, VT_SLAB), lambda i, j: (i, 0, k, 0))
    rq = GQA_R * qb
    out = pl.pallas_call(
        functools.partial(_nsa_kernel, seq=s),
        grid=(b, s // qb),
        in_specs=[pl.BlockSpec((1, qb, qd), lambda i, j: (i, j, 0)),
                  seqspec(kcol), slabspec(0), seqspec(kcol + 1), slabspec(1),
                  pl.BlockSpec((1, KV_HEADS, n_cmp, HEAD_DIM), lambda i, j: (i, 0, 0, 0)),
                  pl.BlockSpec((1, KV_HEADS, HEAD_DIM, n_cmp), lambda i, j: (i, 0, 0, 0)),
                  pl.BlockSpec((1, qb, KV_DIM), lambda i, j: (i, j, 2)),
                  pl.BlockSpec((n_sel, n_cmp), lambda i, j: (0, 0))],
        out_specs=pl.BlockSpec((1, qb, qd), lambda i, j: (i, j, 0)),
        out_shape=jax.ShapeDtypeStruct((b, s, qd), BF16),
        scratch_shapes=[pltpu.VMEM((KV_HEADS, HEAD_DIM, rq), BF16),
                        pltpu.VMEM((KV_HEADS, n_sel, rq), F32),
                        pltpu.VMEM((KV_HEADS, HEAD_DIM, rq), F32),
                        pltpu.VMEM((KV_HEADS, HEAD_DIM, rq), F32),
                        pltpu.VMEM((KV_HEADS, 1, rq), F32),
                        pltpu.VMEM((KV_HEADS, 1, rq), F32),
                        pltpu.VMEM((NSA_AHEAD, NSA_SEL_CHUNK, rq), F32)],
        compiler_params=_cparams("parallel", "arbitrary"),
        name="nsa_core",
    )(pa3, pa3, vt4, pa3, vt4, k_cmp, v_cmp_t, pb3, ov)
    return out.reshape(b * s, qd)


def _rope_tables(positions):
    half = ROPE_DIM // 2
    inv_freq = ROPE_THETA ** (-jnp.arange(0, ROPE_DIM, 2, dtype=F32) / ROPE_DIM)
    ang = positions.astype(F32).reshape(-1)[:, None] * inv_freq
    cos, sin = jnp.cos(ang), jnp.sin(ang)
    t = cos.shape[0]
    ones = jnp.ones((t, HEAD_DIM - ROPE_DIM), F32)
    zeros = jnp.zeros((t, HEAD_DIM - ROPE_DIM), F32)
    zh = jnp.zeros((t, half), F32)
    c = jnp.concatenate([cos, cos, ones], axis=1)
    sa = jnp.concatenate([-sin, zh, zeros], axis=1)
    sb = jnp.concatenate([zh, sin, zeros], axis=1)
    return tuple(jnp.tile(v, (1, LANES // HEAD_DIM)) for v in (c, sa, sb))


def _pad_cols(w, n):
    return jnp.pad(w, ((0, 0), (0, n - w.shape[1])))


def mamba2_mixer(h, ln_w, w_in, conv_w, conv_b, dt_bias, a_log, d_skip, norm_w, w_out, b, s):
    w = _pad_cols(w_in, SSM_IN_PAD).astype(BF16)
    zx = norm_proj(h, ln_w, w, jnp.zeros((SSM_IN_PAD,), F32), F32, tm=SSM_IN_TOKEN_TILE, row_splits=2)
    y = ssd_core(zx, conv_w, conv_b, dt_bias, a_log, d_skip, norm_w, b, s)
    return out_proj(y, w_out.astype(BF16), jnp.zeros((D_MODEL,), F32), h)


def swa_mixer(h, ln_w, w_qkv, b_qkv, sinks, w_o, b_o, rope, b, s):
    qk = (N_HEADS + KV_HEADS) * HEAD_DIM
    w = w_qkv.astype(BF16)
    bias = b_qkv.astype(F32)
    q_k, vt = norm_proj(h, ln_w, w[:, :qk], bias[:qk], BF16, rope=rope, rope_cols=((0, qk),),
                        w_t=w[:, qk:].T, bias_t=bias[qk:])
    o = swa_core(q_k, vt, sinks, b, s)
    return out_proj(o, w_o.astype(BF16), b_o.astype(F32), h)


def nsa_mixer(h, ln_w, w_in, pe_k, k_w1, k_w2, pe_v, v_w1, v_w2, w_o, rope, b, s):
    qd = N_HEADS * HEAD_DIM
    cols = lambda k: w_in[:, qd + KV_DIM * k:qd + KV_DIM * (k + 1)]
    wa = jnp.concatenate([w_in[:, :qd], cols(2), cols(4)], axis=1).astype(BF16)
    wv_t = jnp.concatenate([cols(3), cols(5)], axis=1).astype(BF16).T
    wb = _pad_cols(jnp.concatenate([cols(0), cols(1), w_in[:, qd + 6 * KV_DIM:]], axis=1), 3 * KV_DIM).astype(BF16)
    pa, vt = norm_proj(h, ln_w, wa, jnp.zeros((wa.shape[1],), F32), BF16, rope=rope,
                       rope_cols=((0, wa.shape[1]),), w_t=wv_t, bias_t=jnp.zeros((wv_t.shape[0],), F32))
    pb = norm_proj(h, ln_w, wb, jnp.zeros((wb.shape[1],), F32), F32, rope=rope, rope_cols=((0, KV_DIM),))
    pb3 = pb.reshape(b, s, -1)
    k_cmp = compress(pb3[..., 0:KV_DIM], pe_k, k_w1, k_w2, b, s, False)
    v_cmp_t = compress(pb3[..., KV_DIM:2 * KV_DIM], pe_v, v_w1, v_w2, b, s, True)
    o = nsa_core(pa, vt, pb, k_cmp, v_cmp_t, b, s)
    return out_proj(o, w_o.astype(BF16), jnp.zeros((D_MODEL,), F32), h)


def kernel(x, positions, ln_ffn1, ffn1_w_in, ffn1_w_out, ln_mix, ln_ffn2, ffn2_w_in, ffn2_w_out, ssm_w_in, ssm_conv_w, ssm_conv_b, ssm_dt_bias, ssm_a_log, ssm_d, ssm_norm_w, ssm_w_out, swa_w_qkv, swa_b_qkv, swa_sinks, swa_w_o, swa_b_o, nsa_w_in, nsa_pe_k, nsa_k_w1, nsa_k_w2, nsa_pe_v, nsa_v_w1, nsa_v_w2, nsa_w_o, final_norm):
    b, s, d = x.shape
    depth = ln_ffn1.shape[0]
    rope = _rope_tables(positions)
    h = x.reshape(b * s, d)
    for i in range(depth):
        kind, inst = i % 3, i // 3
        h = ffn(h, ln_ffn1[i], ffn1_w_in[i], ffn1_w_out[i])
        if kind == 0:
            h = mamba2_mixer(h, ln_mix[i], ssm_w_in[inst], ssm_conv_w[inst], ssm_conv_b[inst],
                             ssm_dt_bias[inst], ssm_a_log[inst], ssm_d[inst], ssm_norm_w[inst],
                             ssm_w_out[inst], b, s)
        elif kind == 1:
            h = swa_mixer(h, ln_mix[i], swa_w_qkv[inst], swa_b_qkv[inst], swa_sinks[inst],
                          swa_w_o[inst], swa_b_o[inst], rope, b, s)
        else:
            h = nsa_mixer(h, ln_mix[i], nsa_w_in[inst], nsa_pe_k[inst], nsa_k_w1[inst], nsa_k_w2[inst],
                          nsa_pe_v[inst], nsa_v_w1[inst], nsa_v_w2[inst], nsa_w_o[inst], rope, b, s)
        h = ffn(h, ln_ffn2[i], ffn2_w_in[i], ffn2_w_out[i], final_norm if i == depth - 1 else None)
    return h.reshape(b, s, d)
```

```python
import functools
import math

import numpy as np
import jax
import jax.numpy as jnp
from jax import lax
from jax.experimental import pallas as pl
from jax.experimental.pallas import tpu as pltpu

F32 = jnp.float32
BF16 = jnp.bfloat16

D_MODEL = 1024
RMS_EPS = 1e-6
D_FF = 2816
HEAD_DIM = 64
ROPE_DIM = HEAD_DIM // 4
ROPE_THETA = 500000.0
ATTN_SCALE = HEAD_DIM ** -0.5
MASK_NEG = -1e30

SSM_D_INNER = 2 * D_MODEL
SSM_HEADS = 32
SSM_GROUPS = 8
SSM_STATE = 128
SSM_CONV = 4
SSM_CHUNK = 128
SSM_CONV_DIM = SSM_D_INNER + 2 * SSM_GROUPS * SSM_STATE
SSM_IN_PAD = SSM_D_INNER + SSM_CONV_DIM + 256
SSM_IN_TOKEN_TILE = 512

N_HEADS = 16
KV_HEADS = 4
GQA_R = N_HEADS // KV_HEADS
KV_DIM = KV_HEADS * HEAD_DIM

SWA_BLOCK = 128

NSA_CMP_LEN = 32
NSA_CMP_STRIDE = 16
NSA_CMP_HIDDEN = 256
NSA_SEL_BLOCK = 64
NSA_TOP_N = 8
NSA_WINDOW = 512
NSA_Q_BLOCK = 64
NSA_SEL_CHUNK = 512
NSA_AHEAD = 2
VT_SLAB = 128
NSA_WIN_SLABS = NSA_WINDOW // VT_SLAB + 1

LANES = 128
VMEM_LIMIT = 56 * 1024 * 1024
TOKEN_TILE = 1024
ROW_SPLITS = 4
FFN_TOKEN_TILE = 1024
FFN_ROW_SPLITS = 4


def _cparams(*sem):
    return pltpu.CompilerParams(dimension_semantics=sem, vmem_limit_bytes=VMEM_LIMIT)


def _dot(a, b):
    return jnp.dot(a, b, preferred_element_type=F32)


def _dot_nt(a, b):
    return lax.dot_general(a, b, (((1,), (1,)), ((), ())), preferred_element_type=F32)


def _sigmoid(x):
    return 1.0 / (1.0 + jnp.exp(-x))


def _rms(x, w):
    return x * lax.rsqrt(jnp.mean(x * x, axis=-1, keepdims=True) + RMS_EPS) * w


def _one_ahead(items, start, finish, depth=1, primed=()):
    items = list(items)
    pending = list(primed)
    nxt = len(pending)
    while nxt < min(depth, len(items)):
        pending.append(start(items[nxt]))
        nxt += 1
    for item in items:
        if nxt < len(items):
            pending.append(start(items[nxt]))
            nxt += 1
        finish(item, pending.pop(0))


def _rope_store(o_ref, rows, acc, rope_segs, c, sa, sb):
    for k, roped in enumerate(rope_segs):
        seg = acc[:, LANES * k:LANES * (k + 1)]
        if roped:
            seg = (seg * c + pltpu.roll(seg, LANES - ROPE_DIM // 2, 1) * sa
                   + pltpu.roll(seg, ROPE_DIM // 2, 1) * sb)
        o_ref[rows, LANES * k:LANES * (k + 1)] = seg.astype(o_ref.dtype)


def _norm_proj_kernel(*refs, rope_segs, col_tiles, with_vt, row_splits):
    refs = list(refs)
    x_ref, lnw_ref, w_ref, b_ref = refs[:4]
    del refs[:4]
    if rope_segs is not None:
        c_ref, sa_ref, sb_ref = refs[:3]
        del refs[:3]
    if with_vt:
        wt_ref, bt_ref = refs[:2]
        del refs[:2]
    o_ref = refs.pop(0)
    vt_ref = refs.pop(0) if with_vt else None
    xn_ref = refs.pop(0) if col_tiles > 1 else None
    tm = x_ref.shape[0]
    rt = tm // row_splits
    row_tiles = [slice(rt * h, rt * (h + 1)) for h in range(row_splits)]

    if col_tiles > 1:
        @pl.when(pl.program_id(1) == 0)
        def _():
            xn_ref[...] = _rms(x_ref[...], lnw_ref[...]).astype(BF16)

    def start(rows):
        xn = xn_ref[rows, :] if col_tiles > 1 else _rms(x_ref[rows, :], lnw_ref[...]).astype(BF16)
        acc = _dot(xn, w_ref[pl.program_id(1) if col_tiles > 1 else 0]) + b_ref[...]
        return (acc, _dot_nt(wt_ref[...], xn) + bt_ref[...]) if with_vt else (acc, None)

    def finish(rows, res):
        acc, vt = res
        if rope_segs is None:
            o_ref[rows, :] = acc.astype(o_ref.dtype)
        else:
            _rope_store(o_ref, rows, acc, rope_segs, c_ref[rows, :], sa_ref[rows, :], sb_ref[rows, :])
        if with_vt:
            for u in range(rt // VT_SLAB):
                vt_ref[rows.start // VT_SLAB + u] = vt[:, VT_SLAB * u:VT_SLAB * (u + 1)].astype(vt_ref.dtype)

    _one_ahead(row_tiles, start, finish)


def norm_proj(h, ln_w, w, bias, out_dtype, tn=None, rope=None, rope_cols=(), w_t=None, bias_t=None,
              tm=TOKEN_TILE, row_splits=ROW_SPLITS):
    t, d = h.shape
    n = w.shape[1]
    tm = min(tm, t)
    tn = n if tn is None else tn
    col_tiles = n // tn
    assert n == col_tiles * tn and t % tm == 0 and tn % LANES == 0
    with_vt = w_t is not None
    rope_segs = None
    if rope_cols:
        assert col_tiles == 1
        rope_segs = tuple(any(lo <= LANES * k < hi for lo, hi in rope_cols) for k in range(n // LANES))
    w_tiles = w.reshape(d, col_tiles, tn).transpose(1, 0, 2)
    in_specs = [pl.BlockSpec((tm, d), lambda i, j: (i, 0)),
                pl.BlockSpec((1, d), lambda i, j: (0, 0)),
                pl.BlockSpec((col_tiles, d, tn), lambda i, j: (0, 0, 0), pipeline_mode=pl.Buffered(1)),
                pl.BlockSpec((1, tn), lambda i, j: (0, j))]
    args = [h, ln_w.reshape(1, d), w_tiles, bias.reshape(1, n)]
    if rope_segs is not None:
        in_specs += [pl.BlockSpec((tm, LANES), lambda i, j: (i, 0))] * 3
        args += list(rope)
    out_specs = [pl.BlockSpec((tm, tn), lambda i, j: (i, j))]
    out_shape = [jax.ShapeDtypeStruct((t, n), out_dtype)]
    if with_vt:
        assert col_tiles == 1
        nv = w_t.shape[0]
        in_specs += [pl.BlockSpec((nv, d), lambda i, j: (0, 0)), pl.BlockSpec((nv, 1), lambda i, j: (0, 0))]
        args += [w_t, bias_t.reshape(nv, 1)]
        out_specs.append(pl.BlockSpec((tm // VT_SLAB, nv, VT_SLAB), lambda i, j: (i, 0, 0)))
        out_shape.append(jax.ShapeDtypeStruct((t // VT_SLAB, nv, VT_SLAB), BF16))
    res = pl.pallas_call(
        functools.partial(_norm_proj_kernel, rope_segs=rope_segs, col_tiles=col_tiles, with_vt=with_vt,
                          row_splits=row_splits),
        grid=(t // tm, col_tiles),
        in_specs=in_specs,
        out_specs=out_specs,
        out_shape=out_shape,
        scratch_shapes=[pltpu.VMEM((tm, d), BF16)] if col_tiles > 1 else [],
        compiler_params=_cparams("parallel", "arbitrary"),
        name="norm_proj",
    )(*args)
    return tuple(res) if with_vt else res[0]


def _out_proj_kernel(y_ref, w_ref, b_ref, r_ref, o_ref):
    o_ref[...] = r_ref[...] + _dot(y_ref[...], w_ref[...]) + b_ref[...]


def out_proj(y, w, bias, res):
    t, k = y.shape
    d = w.shape[1]
    tm = min(TOKEN_TILE, t)
    return pl.pallas_call(
        _out_proj_kernel,
        grid=(t // tm,),
        in_specs=[pl.BlockSpec((tm, k), lambda i: (i, 0)),
                  pl.BlockSpec((k, d), lambda i: (0, 0)),
                  pl.BlockSpec((1, d), lambda i: (0, 0)),
                  pl.BlockSpec((tm, d), lambda i: (i, 0))],
        out_specs=pl.BlockSpec((tm, d), lambda i: (i, 0)),
        out_shape=jax.ShapeDtypeStruct((t, d), F32),
        compiler_params=_cparams("parallel"),
        name="out_proj",
    )(y, w, bias.reshape(1, d), res)


def _ffn_kernel(x_ref, lnw_ref, wi_ref, wo_ref, fnw_ref, o_ref, *, final_norm):
    tm = x_ref.shape[0]
    rt = tm // FFN_ROW_SPLITS
    dff = wo_ref.shape[0]

    def start(rows):
        xn = _rms(x_ref[rows, :], lnw_ref[...]).astype(BF16)
        return _dot(xn, wi_ref[:, 0:dff]), _dot(xn, wi_ref[:, dff:2 * dff])

    def finish(rows, gu):
        g, u = gu
        y = x_ref[rows, :] + 0.5 * _dot((g * _sigmoid(g) * u).astype(BF16), wo_ref[...])
        if final_norm:
            y = _rms(y, fnw_ref[...])
        o_ref[rows, :] = y

    _one_ahead([slice(rt * h, rt * (h + 1)) for h in range(FFN_ROW_SPLITS)], start, finish)


def ffn(h, ln_w, w_in, w_out, final_w=None):
    t, d = h.shape
    tm = min(FFN_TOKEN_TILE, t)
    final_norm = final_w is not None
    fnw = (final_w if final_norm else ln_w).reshape(1, d)
    resident = lambda shape: pl.BlockSpec(shape, lambda i: (0, 0), pipeline_mode=pl.Buffered(1))
    return pl.pallas_call(
        functools.partial(_ffn_kernel, final_norm=final_norm),
        grid=(t // tm,),
        in_specs=[pl.BlockSpec((tm, d), lambda i: (i, 0)),
                  pl.BlockSpec((1, d), lambda i: (0, 0)),
                  resident(w_in.shape),
                  resident(w_out.shape),
                  pl.BlockSpec((1, d), lambda i: (0, 0))],
        out_specs=pl.BlockSpec((tm, d), lambda i: (i, 0)),
        out_shape=jax.ShapeDtypeStruct((t, d), F32),
        compiler_params=_cparams("parallel"),
        name="ffn",
    )(h, ln_w.reshape(1, d), w_in.astype(BF16), w_out.astype(BF16), fnw)


def _expand_heads(mat, g):
    rows = mat.shape[0]
    lane = lax.broadcasted_iota(jnp.int32, (rows, LANES), 1)
    pieces = []
    for p in range(2):
        h0 = GQA_R * g + 2 * p
        a = jnp.broadcast_to(mat[:, h0:h0 + 1], (rows, LANES))
        b = jnp.broadcast_to(mat[:, h0 + 1:h0 + 2], (rows, LANES))
        pieces.append(jnp.where(lane < HEAD_DIM, a, b))
    return jnp.concatenate(pieces, axis=1)


def _ssd_kernel(z_ref, x_ref, bc_ref, dt_ref, cw_ref, cb_ref, dtb_ref, alog_ref, dsk_ref, nw_ref,
                tril_ref, o_ref, state_ref, xbuf_ref):
    q = SSM_CHUNK
    gw = GQA_R * HEAD_DIM
    c = pl.program_id(1)

    @pl.when(c == 0)
    def _():
        state_ref[...] = jnp.zeros(state_ref.shape, F32)
        xbuf_ref[0:8, :] = jnp.zeros((8, SSM_CONV_DIM), F32)

    xbuf_ref[8:8 + q, 0:SSM_D_INNER] = x_ref[0]
    xbuf_ref[8:8 + q, SSM_D_INNER:SSM_CONV_DIM] = bc_ref[0]

    def conv_silu(c0, width):
        acc = cb_ref[:, c0:c0 + width]
        for k in reversed(range(SSM_CONV)):
            r0 = 8 - (SSM_CONV - 1) + k
            acc = acc + cw_ref[k:k + 1, c0:c0 + width] * xbuf_ref[r0:r0 + q, c0:c0 + width]
        return acc * _sigmoid(acc)

    dtr = dt_ref[0] + dtb_ref[...]
    e_neg = jnp.exp(-jnp.abs(dtr))
    u = 1.0 + e_neg
    log_u = jnp.log2(u) * math.log(2.0)
    dt = jnp.maximum(dtr, 0.0) + jnp.where(u == 1.0, e_neg, log_u * (e_neg / (u - 1.0)))
    ad = dt * (-jnp.exp(alog_ref[...]))
    acum = jnp.dot(tril_ref[...], ad, precision=lax.Precision.HIGHEST,
                   preferred_element_type=F32)
    acum_row = acum.T
    causal = (lax.broadcasted_iota(jnp.int32, (q, q), 0) >= lax.broadcasted_iota(jnp.int32, (q, q), 1))
    head_of_lane = lax.broadcasted_iota(jnp.int32, (q, 2 * LANES), 1) // HEAD_DIM

    def start(g):
        xs = conv_silu(gw * g, gw)
        bm = conv_silu(SSM_D_INNER + SSM_STATE * g, SSM_STATE)
        cm = conv_silu(SSM_D_INNER + SSM_GROUPS * SSM_STATE + SSM_STATE * g, SSM_STATE)
        xd = xs * _expand_heads(dt, g)
        cbf = cm.astype(BF16)
        st = state_ref[g]
        return xs, bm, xd, _dot_nt(cbf, bm.astype(BF16)), _dot(cbf, st.astype(BF16))

    def finish(g, vals):
        xs, bm, xd, cb, y_off = vals
        st = state_ref[g]
        cs = slice(gw * g, gw * (g + 1))
        acum_ch = _expand_heads(acum, g)
        a_last = acum_ch[q - 1:q, :]
        xdb = xd.astype(BF16)
        y_diag = None
        for r in range(GQA_R):
            hh = GQA_R * g + r
            diff = acum[:, hh:hh + 1] - acum_row[hh:hh + 1, :]
            decay = jnp.exp(jnp.where(causal, diff, -jnp.inf))
            y_r = _dot((cb * decay).astype(BF16), xdb)
            y_diag = y_r if r == 0 else jnp.where(head_of_lane == r, y_r, y_diag)
        state_ref[g] = (st * jnp.exp(a_last)
                        + _dot(bm.T.astype(BF16), (xd * jnp.exp(a_last - acum_ch)).astype(BF16)))
        y = y_off * jnp.exp(acum_ch) + y_diag + dsk_ref[:, cs] * xs
        zz = z_ref[0, :, cs]
        gated = y * (zz * _sigmoid(zz))
        o_ref[0, :, cs] = _rms(gated, nw_ref[:, cs]).astype(o_ref.dtype)

    _one_ahead(range(SSM_GROUPS), start, finish)
    xbuf_ref[0:8, :] = xbuf_ref[q:q + 8, :]


def ssd_core(zx, conv_w, conv_b, dt_bias, a_log, d_skip, norm_w, b, s):
    q = SSM_CHUNK
    zx3 = zx.reshape(b, s, SSM_IN_PAD)
    pad = lambda v: jnp.pad(v.astype(F32), (0, LANES - SSM_HEADS)).reshape(1, LANES)
    tril = jnp.asarray(np.tril(np.ones((q, q), np.float32)))
    wide = SSM_D_INNER
    full = lambda shape: pl.BlockSpec(shape, lambda i, c: (0,) * len(shape))
    out = pl.pallas_call(
        _ssd_kernel,
        grid=(b, s // q),
        in_specs=[pl.BlockSpec((1, q, wide), lambda i, c: (i, c, 0)),
                  pl.BlockSpec((1, q, wide), lambda i, c: (i, c, 1)),
                  pl.BlockSpec((1, q, wide), lambda i, c: (i, c, 2)),
                  pl.BlockSpec((1, q, LANES), lambda i, c: (i, c, (SSM_D_INNER + SSM_CONV_DIM) // LANES)),
                  full((SSM_CONV, SSM_CONV_DIM)), full((1, SSM_CONV_DIM)),
                  full((1, LANES)), full((1, LANES)), full((1, wide)), full((1, wide)),
                  full((q, q))],
        out_specs=pl.BlockSpec((1, q, wide), lambda i, c: (i, c, 0)),
        out_shape=jax.ShapeDtypeStruct((b, s, wide), BF16),
        scratch_shapes=[pltpu.VMEM((SSM_GROUPS, SSM_STATE, GQA_R * HEAD_DIM), F32),
                        pltpu.VMEM((q + 8, SSM_CONV_DIM), F32)],
        compiler_params=_cparams("parallel", "arbitrary"),
        name="ssd_core",
    )(zx3, zx3, zx3, zx3, conv_w.astype(F32), conv_b.reshape(1, -1).astype(F32),
      pad(dt_bias), pad(a_log), jnp.repeat(d_skip.astype(F32), HEAD_DIM).reshape(1, wide),
      norm_w.reshape(1, wide).astype(F32), tril)
    return out.reshape(b * s, wide)


def _swa_kernel(q_ref, kc_ref, kp_ref, vtc_ref, vtp_ref, sink_ref, o_ref):
    i = pl.program_id(1)
    w = SWA_BLOCK
    rq = GQA_R * w
    lane = lax.broadcasted_iota(jnp.int32, (1, rq), 1)
    qloc = lane & (w - 1)
    head_of_lane = lane // w
    kk = lax.broadcasted_iota(jnp.int32, (2 * w, rq), 0)
    valid = ((kk < w) & (kk > qloc) & (i > 0)) | ((kk >= w) & ((kk - w) <= qloc))
    scale = jnp.asarray(ATTN_SCALE, BF16)

    def start(kv):
        dsl = slice(HEAD_DIM * kv, HEAD_DIM * (kv + 1))
        qs = jnp.concatenate(
            [q_ref[0, :, HEAD_DIM * (GQA_R * kv + r):HEAD_DIM * (GQA_R * kv + r + 1)] for r in range(GQA_R)],
            axis=0) * scale
        k = jnp.concatenate([kp_ref[0, :, dsl], kc_ref[0, :, dsl]], axis=0)
        return _dot_nt(k, qs)

    def finish(kv, st):
        dsl = slice(HEAD_DIM * kv, HEAD_DIM * (kv + 1))
        sink = jnp.zeros((1, rq), F32)
        for r in range(GQA_R):
            h = GQA_R * kv + r
            sink = jnp.where(head_of_lane == r, sink_ref[:, h:h + 1], sink)
        sc = jnp.where(valid, st, -jnp.inf)
        m = jnp.maximum(jnp.max(sc, axis=0, keepdims=True), sink)
        e = jnp.exp(sc - m)
        den = jnp.sum(e, axis=0, keepdims=True) + jnp.exp(sink - m)
        vt = jnp.concatenate([vtp_ref[0, 0, dsl, :], vtc_ref[0, 0, dsl, :]], axis=1)
        o_q = (_dot(vt, e.astype(BF16)) * (1.0 / den)).T
        o_ref[0, :, KV_DIM * kv:KV_DIM * (kv + 1)] = jnp.concatenate(
            [o_q[w * r:w * (r + 1)] for r in range(GQA_R)], axis=1).astype(o_ref.dtype)

    _one_ahead(range(KV_HEADS), start, finish, depth=2)


def swa_core(qk, vt, sinks, b, s):
    w = SWA_BLOCK
    qkv3 = qk.reshape(b, s, -1)
    qd = N_HEADS * HEAD_DIM
    kcol = qd // KV_DIM
    vt = vt.reshape(b, s // VT_SLAB, KV_DIM, VT_SLAB)
    prev = lambda i, j: (i, jnp.maximum(j - 1, 0))
    out = pl.pallas_call(
        _swa_kernel,
        grid=(b, s // w),
        in_specs=[pl.BlockSpec((1, w, qd), lambda i, j: (i, j, 0)),
                  pl.BlockSpec((1, w, KV_DIM), lambda i, j: (i, j, kcol)),
                  pl.BlockSpec((1, w, KV_DIM), lambda i, j: prev(i, j) + (kcol,)),
                  pl.BlockSpec((1, 1, KV_DIM, VT_SLAB), lambda i, j: (i, j, 0, 0)),
                  pl.BlockSpec((1, 1, KV_DIM, VT_SLAB), lambda i, j: prev(i, j) + (0, 0)),
                  pl.BlockSpec((1, LANES), lambda i, j: (0, 0))],
        out_specs=pl.BlockSpec((1, w, qd), lambda i, j: (i, j, 0)),
        out_shape=jax.ShapeDtypeStruct((b, s, qd), BF16),
        compiler_params=_cparams("parallel", "arbitrary"),
        name="swa_core",
    )(qkv3, qkv3, qkv3, vt, vt, jnp.pad(sinks.astype(F32), (0, LANES - N_HEADS)).reshape(1, LANES))
    return out.reshape(b * s, qd)


def _compress_kernel(f_ref, pe_ref, w1_ref, w2_ref, o_ref, *, transposed):
    half = NSA_CMP_STRIDE * HEAD_DIM
    x = f_ref[0, 0]
    za = (x + pe_ref[:, 0:half]).astype(BF16)
    zb = (x + pe_ref[:, half:2 * half]).astype(BF16)
    hb = _dot(zb, w1_ref[half:2 * half, :])
    hid = _dot(za, w1_ref[0:half, :]) + pltpu.roll(hb, hb.shape[0] - 1, 0)
    act = (hid * _sigmoid(hid)).astype(BF16)
    if transposed:
        o_ref[0, 0] = _dot_nt(w2_ref[...], act).astype(o_ref.dtype)
    else:
        o_ref[0, 0] = _dot(act, w2_ref[...]).astype(o_ref.dtype)


def compress(tok, pe, w1, w2, b, s, transposed):
    nch = s // NSA_CMP_STRIDE
    half = NSA_CMP_STRIDE * HEAD_DIM
    f = tok.reshape(b, nch, NSA_CMP_STRIDE, KV_HEADS, HEAD_DIM).transpose(0, 3, 1, 2, 4).reshape(b, KV_HEADS, nch, half)
    w2 = (w2.T if transposed else w2).astype(BF16)
    oshape = (HEAD_DIM, nch) if transposed else (nch, HEAD_DIM)
    return pl.pallas_call(
        functools.partial(_compress_kernel, transposed=transposed),
        grid=(b, KV_HEADS),
        in_specs=[pl.BlockSpec((1, 1, nch, half), lambda i, j: (i, j, 0, 0)),
                  pl.BlockSpec((1, 2 * half), lambda i, j: (0, 0)),
                  pl.BlockSpec((2 * half, NSA_CMP_HIDDEN), lambda i, j: (0, 0)),
                  pl.BlockSpec(w2.shape, lambda i, j: (0, 0))],
        out_specs=pl.BlockSpec((1, 1) + oshape, lambda i, j: (i, j, 0, 0)),
        out_shape=jax.ShapeDtypeStruct((b, KV_HEADS) + oshape, BF16),
        compiler_params=_cparams("parallel", "parallel"),
        name="nsa_compress",
    )(f, pe.reshape(1, 2 * half).astype(F32), w1.astype(BF16), w2)


def _dot_split3(a_bf16, x):
    hi = x.astype(BF16)
    r1 = x - hi.astype(F32)
    mid = r1.astype(BF16)
    lo = (r1 - mid.astype(F32)).astype(BF16)
    return _dot(a_bf16, hi) + _dot(a_bf16, mid) + _dot(a_bf16, lo)


def _nsa_kernel(q_ref, ks_ref, vst_ref, kw_ref, vwt_ref, kcmp_ref, vcmpt_ref, g_ref, ov_ref,
                o_ref, qt_ref, bias_ref, ocmp_ref, owin_ref, acc_ref, m_ref, l_ref, st_ref, *, seq):
    i = pl.program_id(1)
    qb = NSA_Q_BLOCK
    rq = GQA_R * qb
    n_cmp = seq // NSA_CMP_STRIDE
    n_sel = seq // NSA_SEL_BLOCK
    blocks_per_chunk = NSA_SEL_CHUNK // NSA_SEL_BLOCK
    slabs_per_chunk = NSA_SEL_CHUNK // VT_SLAB
    t0 = i * qb
    qpos = t0 + (lax.broadcasted_iota(jnp.int32, (1, rq), 1) & (qb - 1))
    gates_t = _sigmoid(g_ref[0]).T
    q_t = q_ref[0].astype(F32).T * ATTN_SCALE

    cmp_end = lax.broadcasted_iota(jnp.int32, (n_cmp, rq), 0) * NSA_CMP_STRIDE + (NSA_CMP_LEN - 1)
    mask_c = cmp_end <= qpos
    has_cmp = jnp.where(qpos >= NSA_CMP_LEN - 1, 1.0, 0.0)
    blk = lax.broadcasted_iota(jnp.int32, (n_sel, LANES), 0)
    blk_f = blk.astype(F32)
    forced = (blk == i) | (blk == 0)
    allowed = blk <= i
    lane_lo = lax.broadcasted_iota(jnp.int32, (n_sel, LANES), 1) < qb
    importance = {}
    cw = jnp.maximum(t0 - NSA_WINDOW, 0) // VT_SLAB
    wkeys = NSA_WIN_SLABS * VT_SLAB
    wstart = pl.multiple_of(cw * VT_SLAB, VT_SLAB)
    delta_w = qpos - (wstart + lax.broadcasted_iota(jnp.int32, (wkeys, rq), 0))
    mask_w = (delta_w >= 0) & (delta_w < NSA_WINDOW)
    last_chunk = i // blocks_per_chunk
    key_in_blk = lax.broadcasted_iota(jnp.int32, (NSA_SEL_BLOCK, rq), 0)
    groups = range(KV_HEADS)

    def cmp_start(kv):
        qt = jnp.concatenate(
            [q_t[HEAD_DIM * (GQA_R * kv + r):HEAD_DIM * (GQA_R * kv + r + 1), :] for r in range(GQA_R)],
            axis=1).astype(BF16)
        qt_ref[kv] = qt
        return _dot(kcmp_ref[0, kv], qt)

    def cmp_finish(kv, logits):
        lc = jnp.where(mask_c, logits, MASK_NEG)
        ec = jnp.exp(lc - jnp.max(lc, axis=0, keepdims=True))
        inv = has_cmp / jnp.sum(ec, axis=0, keepdims=True)
        ocmp_ref[kv] = _dot(vcmpt_ref[0, kv], ec.astype(BF16)) * inv
        imp = _dot_split3(ov_ref[...], ec) * inv
        imp = imp + pltpu.roll(imp, qb, 1)
        imp = imp + pltpu.roll(imp, 2 * qb, 1)
        m_ref[kv] = jnp.full((1, rq), MASK_NEG, F32)
        l_ref[kv] = jnp.zeros((1, rq), F32)
        acc_ref[kv] = jnp.zeros((HEAD_DIM, rq), F32)
        importance[kv] = imp[:, 0:LANES]
        if kv % 2 == 1:
            select_pair(kv - 1, kv)

    def select_pair(ka, kb):
        v = jnp.where(lane_lo, importance[ka], importance[kb])
        v = jnp.where(allowed, jnp.where(forced, jnp.inf, v), -jnp.inf)
        sel = blk < 0
        for _ in range(NSA_TOP_N):
            best = jnp.max(v, axis=0, keepdims=True)
            first = jnp.min(jnp.where(v == best, blk_f, float(n_sel)), axis=0, keepdims=True)
            pick = blk_f == first
            sel = sel | pick
            v = jnp.where(pick, -jnp.inf, v)
        bias = jnp.where(sel & allowed, 0.0, MASK_NEG)
        swapped = pltpu.roll(bias, qb, 1)
        for kv, own_lo in ((ka, True), (kb, False)):
            half = jnp.where(lane_lo, bias, swapped) if own_lo else jnp.where(lane_lo, swapped, bias)
            bias_ref[kv] = jnp.concatenate([half, half], axis=1)

    def cmp_phase_start(item):
        if not isinstance(item, tuple):
            return cmp_start(item)
        kind, kv = item
        if kind == "window":
            return _dot(kw_ref[0, pl.ds(wstart, wkeys), HEAD_DIM * kv:HEAD_DIM * (kv + 1)], qt_ref[kv])
        return sel_start(0, kv)

    def cmp_phase_finish(item, res):
        if not isinstance(item, tuple):
            cmp_finish(item, res)
            return
        kind, kv = item
        if kind == "first_chunk":
            st_ref[kv] = res
            return
        dsl = slice(HEAD_DIM * kv, HEAD_DIM * (kv + 1))
        lw = jnp.where(mask_w, res, MASK_NEG)
        ew = jnp.exp(lw - jnp.max(lw, axis=0, keepdims=True))
        vwt = jnp.concatenate([vwt_ref[0, cw + u, dsl, :] for u in range(NSA_WIN_SLABS)], axis=1)
        owin_ref[kv] = _dot(vwt, ew.astype(BF16)) * (1.0 / jnp.sum(ew, axis=0, keepdims=True))

    def sel_start(c, kv):
        k0 = pl.multiple_of(c * NSA_SEL_CHUNK, NSA_SEL_CHUNK)
        return _dot(ks_ref[0, pl.ds(k0, NSA_SEL_CHUNK), HEAD_DIM * kv:HEAD_DIM * (kv + 1)], qt_ref[kv])

    def sel_finish(c, kv, st, causal):
        dsl = slice(HEAD_DIM * kv, HEAD_DIM * (kv + 1))
        k0 = c * NSA_SEL_CHUNK
        bias8 = bias_ref[kv, pl.ds(pl.multiple_of(c * blocks_per_chunk, blocks_per_chunk), blocks_per_chunk), :]
        rows = []
        for jb in range(blocks_per_chunk):
            sj = st[NSA_SEL_BLOCK * jb:NSA_SEL_BLOCK * (jb + 1)] + bias8[jb:jb + 1, :]
            if causal:
                sj = jnp.where(k0 + NSA_SEL_BLOCK * jb + key_in_blk <= qpos, sj, MASK_NEG)
            rows.append(sj)
        sc = jnp.concatenate(rows, axis=0)
        m_run = m_ref[kv]
        m_new = jnp.maximum(m_run, jnp.max(sc, axis=0, keepdims=True))
        p = jnp.exp(sc - m_new)
        alpha = jnp.exp(m_run - m_new)
        vt = jnp.concatenate([vst_ref[0, slabs_per_chunk * c + u, dsl, :] for u in range(slabs_per_chunk)],
                             axis=1)
        m_ref[kv] = m_new
        l_ref[kv] = alpha * l_ref[kv] + jnp.sum(p, axis=0, keepdims=True)
        acc_ref[kv] = alpha * acc_ref[kv] + _dot(vt, p.astype(BF16))

    ahead = NSA_AHEAD
    _one_ahead(list(groups) + [("window", kv) for kv in groups] + [("first_chunk", kv) for kv in range(ahead)],
               cmp_phase_start, cmp_phase_finish, depth=ahead)

    def sel_trip(c, carry):
        pending = [st_ref[k] for k in range(ahead)]
        for kv in groups:
            nk = kv + ahead
            pending.append(sel_start(c, nk) if nk < KV_HEADS else sel_start(c + 1, nk - KV_HEADS))
            sel_finish(c, kv, pending.pop(0), False)
        for k in range(ahead):
            st_ref[k] = pending[k]
        return carry

    lax.fori_loop(0, last_chunk, sel_trip, 0)

    def tail_finish(kv, logits):
        sel_finish(last_chunk, kv, logits, True)
        o_sel = acc_ref[kv] * (1.0 / l_ref[kv])

        def gate_row(branch):
            return jnp.concatenate(
                [gates_t[3 * (GQA_R * kv + r) + branch:3 * (GQA_R * kv + r) + branch + 1, :] for r in range(GQA_R)],
                axis=1)
        o_t = gate_row(0) * ocmp_ref[kv] + gate_row(1) * o_sel + gate_row(2) * owin_ref[kv]
        o_q = o_t.T
        o_ref[0, :, KV_DIM * kv:KV_DIM * (kv + 1)] = jnp.concatenate(
            [o_q[qb * r:qb * (r + 1)] for r in range(GQA_R)], axis=1).astype(o_ref.dtype)

    _one_ahead(groups, functools.partial(sel_start, last_chunk), tail_finish,
               depth=ahead, primed=[st_ref[k] for k in range(ahead)])


def nsa_core(pa, vt, pb, k_cmp, v_cmp_t, b, s):
    qb = NSA_Q_BLOCK
    qd = N_HEADS * HEAD_DIM
    n_cmp = s // NSA_CMP_STRIDE
    n_sel = s // NSA_SEL_BLOCK
    n_slab = s // VT_SLAB
    pa3 = pa.reshape(b, s, -1)
    pb3 = pb.reshape(b, s, -1)
    vt4 = vt.reshape(b, n_slab, 2 * KV_DIM, VT_SLAB)
    sel_lo = np.arange(n_sel)[:, None] * NSA_SEL_BLOCK
    cmp_lo = np.arange(n_cmp)[None, :] * NSA_CMP_STRIDE
    ov = np.clip(np.minimum(sel_lo + NSA_SEL_BLOCK, cmp_lo + NSA_CMP_LEN) - np.maximum(sel_lo, cmp_lo), 0, None)
    ov = jnp.asarray((ov / NSA_CMP_LEN).astype(np.float32), BF16)
    kcol = qd // KV_DIM
    seqspec = lambda col: pl.BlockSpec((1, s, KV_DIM), lambda i, j: (i, 0, col))
    slabspec = lambda k: pl.BlockSpec((1, n_slab, KV_DIM, VT_SLAB), lambda i, j: (i, 0, k, 0))
    rq = GQA_R * qb
    out = pl.pallas_call(
        functools.partial(_nsa_kernel, seq=s),
        grid=(b, s // qb),
        in_specs=[pl.BlockSpec((1, qb, qd), lambda i, j: (i, j, 0)),
                  seqspec(kcol), slabspec(0), seqspec(kcol + 1), slabspec(1),
                  pl.BlockSpec((1, KV_HEADS, n_cmp, HEAD_DIM), lambda i, j: (i, 0, 0, 0)),
                  pl.BlockSpec((1, KV_HEADS, HEAD_DIM, n_cmp), lambda i, j: (i, 0, 0, 0)),
                  pl.BlockSpec((1, qb, KV_DIM), lambda i, j: (i, j, 2)),
                  pl.BlockSpec((n_sel, n_cmp), lambda i, j: (0, 0))],
        out_specs=pl.BlockSpec((1, qb, qd), lambda i, j: (i, j, 0)),
        out_shape=jax.ShapeDtypeStruct((b, s, qd), BF16),
        scratch_shapes=[pltpu.VMEM((KV_HEADS, HEAD_DIM, rq), BF16),
                        pltpu.VMEM((KV_HEADS, n_sel, rq), F32),
                        pltpu.VMEM((KV_HEADS, HEAD_DIM, rq), F32),
                        pltpu.VMEM((KV_HEADS, HEAD_DIM, rq), F32),
                        pltpu.VMEM((KV_HEADS, HEAD_DIM, rq), F32),
                        pltpu.VMEM((KV_HEADS, 1, rq), F32),
                        pltpu.VMEM((KV_HEADS, 1, rq), F32),
                        pltpu.VMEM((NSA_AHEAD, NSA_SEL_CHUNK, rq), F32)],
        compiler_params=_cparams("parallel", "arbitrary"),
        name="nsa_core",
    )(pa3, pa3, vt4, pa3, vt4, k_cmp, v_cmp_t, pb3, ov)
    return out.reshape(b * s, qd)


def _rope_tables(positions):
    half = ROPE_DIM // 2
    inv_freq = ROPE_THETA ** (-jnp.arange(0, ROPE_DIM, 2, dtype=F32) / ROPE_DIM)
    ang = positions.astype(F32).reshape(-1)[:, None] * inv_freq
    cos, sin = jnp.cos(ang), jnp.sin(ang)
    t = cos.shape[0]
    ones = jnp.ones((t, HEAD_DIM - ROPE_DIM), F32)
    zeros = jnp.zeros((t, HEAD_DIM - ROPE_DIM), F32)
    zh = jnp.zeros((t, half), F32)
    c = jnp.concatenate([cos, cos, ones], axis=1)
    sa = jnp.concatenate([-sin, zh, zeros], axis=1)
    sb = jnp.concatenate([zh, sin, zeros], axis=1)
    return tuple(jnp.tile(v, (1, LANES // HEAD_DIM)) for v in (c, sa, sb))


def _pad_cols(w, n):
    return jnp.pad(w, ((0, 0), (0, n - w.shape[1])))


def mamba2_mixer(h, ln_w, w_in, conv_w, conv_b, dt_bias, a_log, d_skip, norm_w, w_out, b, s):
    w = _pad_cols(w_in, SSM_IN_PAD).astype(BF16)
    zx = norm_proj(h, ln_w, w, jnp.zeros((SSM_IN_PAD,), F32), F32, tm=SSM_IN_TOKEN_TILE, row_splits=2)
    y = ssd_core(zx, conv_w, conv_b, dt_bias, a_log, d_skip, norm_w, b, s)
    return out_proj(y, w_out.astype(BF16), jnp.zeros((D_MODEL,), F32), h)


def swa_mixer(h, ln_w, w_qkv, b_qkv, sinks, w_o, b_o, rope, b, s):
    qk = (N_HEADS + KV_HEADS) * HEAD_DIM
    w = w_qkv.astype(BF16)
    bias = b_qkv.astype(F32)
    q_k, vt = norm_proj(h, ln_w, w[:, :qk], bias[:qk], BF16, rope=rope, rope_cols=((0, qk),),
                        w_t=w[:, qk:].T, bias_t=bias[qk:])
    o = swa_core(q_k, vt, sinks, b, s)
    return out_proj(o, w_o.astype(BF16), b_o.astype(F32), h)


def nsa_mixer(h, ln_w, w_in, pe_k, k_w1, k_w2, pe_v, v_w1, v_w2, w_o, rope, b, s):
    qd = N_HEADS * HEAD_DIM
    cols = lambda k: w_in[:, qd + KV_DIM * k:qd + KV_DIM * (k + 1)]
    wa = jnp.concatenate([w_in[:, :qd], cols(2), cols(4)], axis=1).astype(BF16)
    wv_t = jnp.concatenate([cols(3), cols(5)], axis=1).astype(BF16).T
    wb = _pad_cols(jnp.concatenate([cols(0), cols(1), w_in[:, qd + 6 * KV_DIM:]], axis=1), 3 * KV_DIM).astype(BF16)
    pa, vt = norm_proj(h, ln_w, wa, jnp.zeros((wa.shape[1],), F32), BF16, rope=rope,
                       rope_cols=((0, wa.shape[1]),), w_t=wv_t, bias_t=jnp.zeros((wv_t.shape[0],), F32))
    pb = norm_proj(h, ln_w, wb, jnp.zeros((wb.shape[1],), F32), F32, rope=rope, rope_cols=((0, KV_DIM),))
    pb3 = pb.reshape(b, s, -1)
    k_cmp = compress(pb3[..., 0:KV_DIM], pe_k, k_w1, k_w2, b, s, False)
    v_cmp_t = compress(pb3[..., KV_DIM:2 * KV_DIM], pe_v, v_w1, v_w2, b, s, True)
    o = nsa_core(pa, vt, pb, k_cmp, v_cmp_t, b, s)
    return out_proj(o, w_o.astype(BF16), jnp.zeros((D_MODEL,), F32), h)


def kernel(x, positions, ln_ffn1, ffn1_w_in, ffn1_w_out, ln_mix, ln_ffn2, ffn2_w_in, ffn2_w_out, ssm_w_in, ssm_conv_w, ssm_conv_b, ssm_dt_bias, ssm_a_log, ssm_d, ssm_norm_w, ssm_w_out, swa_w_qkv, swa_b_qkv, swa_sinks, swa_w_o, swa_b_o, nsa_w_in, nsa_pe_k, nsa_k_w1, nsa_k_w2, nsa_pe_v, nsa_v_w1, nsa_v_w2, nsa_w_o, final_norm):
    b, s, d = x.shape
    depth = ln_ffn1.shape[0]
    rope = _rope_tables(positions)
    h = x.reshape(b * s, d)
    for i in range(depth):
        kind, inst = i % 3, i // 3
        h = ffn(h, ln_ffn1[i], ffn1_w_in[i], ffn1_w_out[i])
        if kind == 0:
            h = mamba2_mixer(h, ln_mix[i], ssm_w_in[inst], ssm_conv_w[inst], ssm_conv_b[inst],
                             ssm_dt_bias[inst], ssm_a_log[inst], ssm_d[inst], ssm_norm_w[inst],
                             ssm_w_out[inst], b, s)
        elif kind == 1:
            h = swa_mixer(h, ln_mix[i], swa_w_qkv[inst], swa_b_qkv[inst], swa_sinks[inst],
                          swa_w_o[inst], swa_b_o[inst], rope, b, s)
        else:
            h = nsa_mixer(h, ln_mix[i], nsa_w_in[inst], nsa_pe_k[inst], nsa_k_w1[inst], nsa_k_w2[inst],
                          nsa_pe_v[inst], nsa_v_w1[inst], nsa_v_w2[inst], nsa_w_o[inst], rope, b, s)
        h = ffn(h, ln_ffn2[i], ffn2_w_in[i], ffn2_w_out[i], final_norm if i == depth - 1 else None)
    return h.reshape(b, s, d)
```

```python
import functools
import math

import numpy as np
import jax
import jax.numpy as jnp
from jax import lax
from jax.experimental import pallas as pl
from jax.experimental.pallas import tpu as pltpu

F32 = jnp.float32
BF16 = jnp.bfloat16

D_MODEL = 1024
RMS_EPS = 1e-6
D_FF = 2816
HEAD_DIM = 64
ROPE_DIM = HEAD_DIM // 4
ROPE_THETA = 500000.0
ATTN_SCALE = HEAD_DIM ** -0.5
MASK_NEG = -1e30

SSM_D_INNER = 2 * D_MODEL
SSM_HEADS = 32
SSM_GROUPS = 8
SSM_STATE = 128
SSM_CONV = 4
SSM_CHUNK = 128
SSM_CONV_DIM = SSM_D_INNER + 2 * SSM_GROUPS * SSM_STATE
SSM_IN_PAD = SSM_D_INNER + SSM_CONV_DIM + 256
SSM_IN_TOKEN_TILE = 512

N_HEADS = 16
KV_HEADS = 4
GQA_R = N_HEADS // KV_HEADS
KV_DIM = KV_HEADS * HEAD_DIM

SWA_BLOCK = 128

NSA_CMP_LEN = 32
NSA_CMP_STRIDE = 16
NSA_CMP_HIDDEN = 256
NSA_SEL_BLOCK = 64
NSA_TOP_N = 8
NSA_WINDOW = 512
NSA_Q_BLOCK = 64
NSA_SEL_CHUNK = 512
NSA_AHEAD = 2
VT_SLAB = 128
NSA_WIN_SLABS = NSA_WINDOW // VT_SLAB + 1

LANES = 128
VMEM_LIMIT = 56 * 1024 * 1024
TOKEN_TILE = 1024
ROW_SPLITS = 4
FFN_TOKEN_TILE = 1024
FFN_ROW_SPLITS = 4


def _cparams(*sem):
    return pltpu.CompilerParams(dimension_semantics=sem, vmem_limit_bytes=VMEM_LIMIT)


def _dot(a, b):
    return jnp.dot(a, b, preferred_element_type=F32)


def _dot_nt(a, b):
    return lax.dot_general(a, b, (((1,), (1,)), ((), ())), preferred_element_type=F32)


def _sigmoid(x):
    return 1.0 / (1.0 + jnp.exp(-x))


def _rms(x, w):
    return x * lax.rsqrt(jnp.mean(x * x, axis=-1, keepdims=True) + RMS_EPS) * w


def _one_ahead(items, start, finish, depth=1, primed=()):
    items = list(items)
    pending = list(primed)
    nxt = len(pending)
    while nxt < min(depth, len(items)):
        pending.append(start(items[nxt]))
        nxt += 1
    for item in items:
        if nxt < len(items):
            pending.append(start(items[nxt]))
            nxt += 1
        finish(item, pending.pop(0))


def _rope_store(o_ref, rows, acc, rope_segs, c, sa, sb):
    for k, roped in enumerate(rope_segs):
        seg = acc[:, LANES * k:LANES * (k + 1)]
        if roped:
            seg = (seg * c + pltpu.roll(seg, LANES - ROPE_DIM // 2, 1) * sa
                   + pltpu.roll(seg, ROPE_DIM // 2, 1) * sb)
        o_ref[rows, LANES * k:LANES * (k + 1)] = seg.astype(o_ref.dtype)


def _norm_proj_kernel(*refs, rope_segs, col_tiles, with_vt, row_splits):
    refs = list(refs)
    x_ref, lnw_ref, w_ref, b_ref = refs[:4]
    del refs[:4]
    if rope_segs is not None:
        c_ref, sa_ref, sb_ref = refs[:3]
        del refs[:3]
    if with_vt:
        wt_ref, bt_ref = refs[:2]
        del refs[:2]
    o_ref = refs.pop(0)
    vt_ref = refs.pop(0) if with_vt else None
    xn_ref = refs.pop(0) if col_tiles > 1 else None
    tm = x_ref.shape[0]
    rt = tm // row_splits
    row_tiles = [slice(rt * h, rt * (h + 1)) for h in range(row_splits)]

    if col_tiles > 1:
        @pl.when(pl.program_id(1) == 0)
        def _():
            xn_ref[...] = _rms(x_ref[...], lnw_ref[...]).astype(BF16)

    def start(rows):
        xn = xn_ref[rows, :] if col_tiles > 1 else _rms(x_ref[rows, :], lnw_ref[...]).astype(BF16)
        acc = _dot(xn, w_ref[pl.program_id(1) if col_tiles > 1 else 0]) + b_ref[...]
        return (acc, _dot_nt(wt_ref[...], xn) + bt_ref[...]) if with_vt else (acc, None)

    def finish(rows, res):
        acc, vt = res
        if rope_segs is None:
            o_ref[rows, :] = acc.astype(o_ref.dtype)
        else:
            _rope_store(o_ref, rows, acc, rope_segs, c_ref[rows, :], sa_ref[rows, :], sb_ref[rows, :])
        if with_vt:
            for u in range(rt // VT_SLAB):
                vt_ref[rows.start // VT_SLAB + u] = vt[:, VT_SLAB * u:VT_SLAB * (u + 1)].astype(vt_ref.dtype)

    _one_ahead(row_tiles, start, finish)


def norm_proj(h, ln_w, w, bias, out_dtype, tn=None, rope=None, rope_cols=(), w_t=None, bias_t=None,
              tm=TOKEN_TILE, row_splits=ROW_SPLITS):
    t, d = h.shape
    n = w.shape[1]
    tm = min(tm, t)
    tn = n if tn is None else tn
    col_tiles = n // tn
    assert n == col_tiles * tn and t % tm == 0 and tn % LANES == 0
    with_vt = w_t is not None
    rope_segs = None
    if rope_cols:
        assert col_tiles == 1
        rope_segs = tuple(any(lo <= LANES * k < hi for lo, hi in rope_cols) for k in range(n // LANES))
    w_tiles = w.reshape(d, col_tiles, tn).transpose(1, 0, 2)
    in_specs = [pl.BlockSpec((tm, d), lambda i, j: (i, 0)),
                pl.BlockSpec((1, d), lambda i, j: (0, 0)),
                pl.BlockSpec((col_tiles, d, tn), lambda i, j: (0, 0, 0), pipeline_mode=pl.Buffered(1)),
                pl.BlockSpec((1, tn), lambda i, j: (0, j))]
    args = [h, ln_w.reshape(1, d), w_tiles, bias.reshape(1, n)]
    if rope_segs is not None:
        in_specs += [pl.BlockSpec((tm, LANES), lambda i, j: (i, 0))] * 3
        args += list(rope)
    out_specs = [pl.BlockSpec((tm, tn), lambda i, j: (i, j))]
    out_shape = [jax.ShapeDtypeStruct((t, n), out_dtype)]
    if with_vt:
        assert col_tiles == 1
        nv = w_t.shape[0]
        in_specs += [pl.BlockSpec((nv, d), lambda i, j: (0, 0)), pl.BlockSpec((nv, 1), lambda i, j: (0, 0))]
        args += [w_t, bias_t.reshape(nv, 1)]
        out_specs.append(pl.BlockSpec((tm // VT_SLAB, nv, VT_SLAB), lambda i, j: (i, 0, 0)))
        out_shape.append(jax.ShapeDtypeStruct((t // VT_SLAB, nv, VT_SLAB), BF16))
    res = pl.pallas_call(
        functools.partial(_norm_proj_kernel, rope_segs=rope_segs, col_tiles=col_tiles, with_vt=with_vt,
                          row_splits=row_splits),
        grid=(t // tm, col_tiles),
        in_specs=in_specs,
        out_specs=out_specs,
        out_shape=out_shape,
        scratch_shapes=[pltpu.VMEM((tm, d), BF16)] if col_tiles > 1 else [],
        compiler_params=_cparams("parallel", "arbitrary"),
        name="norm_proj",
    )(*args)
    return tuple(res) if with_vt else res[0]


def _out_proj_kernel(y_ref, w_ref, b_ref, r_ref, o_ref):
    o_ref[...] = r_ref[...] + _dot(y_ref[...], w_ref[...]) + b_ref[...]


def out_proj(y, w, bias, res):
    t, k = y.shape
    d = w.shape[1]
    tm = min(TOKEN_TILE, t)
    return pl.pallas_call(
        _out_proj_kernel,
        grid=(t // tm,),
        in_specs=[pl.BlockSpec((tm, k), lambda i: (i, 0)),
                  pl.BlockSpec((k, d), lambda i: (0, 0)),
                  pl.BlockSpec((1, d), lambda i: (0, 0)),
                  pl.BlockSpec((tm, d), lambda i: (i, 0))],
        out_specs=pl.BlockSpec((tm, d), lambda i: (i, 0)),
        out_shape=jax.ShapeDtypeStruct((t, d), F32),
        compiler_params=_cparams("parallel"),
        name="out_proj",
    )(y, w, bias.reshape(1, d), res)


def _ffn_kernel(x_ref, lnw_ref, wi_ref, wo_ref, fnw_ref, o_ref, *, final_norm):
    tm = x_ref.shape[0]
    rt = tm // FFN_ROW_SPLITS
    dff = wo_ref.shape[0]

    def start(rows):
        xn = _rms(x_ref[rows, :], lnw_ref[...]).astype(BF16)
        return _dot(xn, wi_ref[:, 0:dff]), _dot(xn, wi_ref[:, dff:2 * dff])

    def finish(rows, gu):
        g, u = gu
        y = x_ref[rows, :] + 0.5 * _dot((g * _sigmoid(g) * u).astype(BF16), wo_ref[...])
        if final_norm:
            y = _rms(y, fnw_ref[...])
        o_ref[rows, :] = y

    _one_ahead([slice(rt * h, rt * (h + 1)) for h in range(FFN_ROW_SPLITS)], start, finish)


def ffn(h, ln_w, w_in, w_out, final_w=None):
    t, d = h.shape
    tm = min(FFN_TOKEN_TILE, t)
    final_norm = final_w is not None
    fnw = (final_w if final_norm else ln_w).reshape(1, d)
    resident = lambda shape: pl.BlockSpec(shape, lambda i: (0, 0), pipeline_mode=pl.Buffered(1))
    return pl.pallas_call(
        functools.partial(_ffn_kernel, final_norm=final_norm),
        grid=(t // tm,),
        in_specs=[pl.BlockSpec((tm, d), lambda i: (i, 0)),
                  pl.BlockSpec((1, d), lambda i: (0, 0)),
                  resident(w_in.shape),
                  resident(w_out.shape),
                  pl.BlockSpec((1, d), lambda i: (0, 0))],
        out_specs=pl.BlockSpec((tm, d), lambda i: (i, 0)),
        out_shape=jax.ShapeDtypeStruct((t, d), F32),
        compiler_params=_cparams("parallel"),
        name="ffn",
    )(h, ln_w.reshape(1, d), w_in.astype(BF16), w_out.astype(BF16), fnw)


def _expand_heads(mat, g):
    rows = mat.shape[0]
    lane = lax.broadcasted_iota(jnp.int32, (rows, LANES), 1)
    pieces = []
    for p in range(2):
        h0 = GQA_R * g + 2 * p
        a = jnp.broadcast_to(mat[:, h0:h0 + 1], (rows, LANES))
        b = jnp.broadcast_to(mat[:, h0 + 1:h0 + 2], (rows, LANES))
        pieces.append(jnp.where(lane < HEAD_DIM, a, b))
    return jnp.concatenate(pieces, axis=1)


def _ssd_kernel(z_ref, x_ref, bc_ref, dt_ref, cw_ref, cb_ref, dtb_ref, alog_ref, dsk_ref, nw_ref,
                tril_ref, o_ref, state_ref, xbuf_ref):
    q = SSM_CHUNK
    gw = GQA_R * HEAD_DIM
    c = pl.program_id(1)

    @pl.when(c == 0)
    def _():
        state_ref[...] = jnp.zeros(state_ref.shape, F32)
        xbuf_ref[0:8, :] = jnp.zeros((8, SSM_CONV_DIM), F32)

    xbuf_ref[8:8 + q, 0:SSM_D_INNER] = x_ref[0]
    xbuf_ref[8:8 + q, SSM_D_INNER:SSM_CONV_DIM] = bc_ref[0]

    def conv_silu(c0, width):
        acc = cb_ref[:, c0:c0 + width]
        for k in reversed(range(SSM_CONV)):
            r0 = 8 - (SSM_CONV - 1) + k
            acc = acc + cw_ref[k:k + 1, c0:c0 + width] * xbuf_ref[r0:r0 + q, c0:c0 + width]
        return acc * _sigmoid(acc)

    dtr = dt_ref[0] + dtb_ref[...]
    e_neg = jnp.exp(-jnp.abs(dtr))
    u = 1.0 + e_neg
    log_u = jnp.log2(u) * math.log(2.0)
    dt = jnp.maximum(dtr, 0.0) + jnp.where(u == 1.0, e_neg, log_u * (e_neg / (u - 1.0)))
    ad = dt * (-jnp.exp(alog_ref[...]))
    acum = jnp.dot(tril_ref[...], ad, precision=lax.Precision.HIGHEST,
                   preferred_element_type=F32)
    acum_row = acum.T
    causal = (lax.broadcasted_iota(jnp.int32, (q, q), 0) >= lax.broadcasted_iota(jnp.int32, (q, q), 1))
    head_of_lane = lax.broadcasted_iota(jnp.int32, (q, 2 * LANES), 1) // HEAD_DIM

    def start(g):
        xs = conv_silu(gw * g, gw)
        bm = conv_silu(SSM_D_INNER + SSM_STATE * g, SSM_STATE)
        cm = conv_silu(SSM_D_INNER + SSM_GROUPS * SSM_STATE + SSM_STATE * g, SSM_STATE)
        xd = xs * _expand_heads(dt, g)
        cbf = cm.astype(BF16)
        st = state_ref[g]
        return xs, bm, xd, _dot_nt(cbf, bm.astype(BF16)), _dot(cbf, st.astype(BF16))

    def finish(g, vals):
        xs, bm, xd, cb, y_off = vals
        st = state_ref[g]
        cs = slice(gw * g, gw * (g + 1))
        acum_ch = _expand_heads(acum, g)
        a_last = acum_ch[q - 1:q, :]
        xdb = xd.astype(BF16)
        y_diag = None
        for r in range(GQA_R):
            hh = GQA_R * g + r
            diff = acum[:, hh:hh + 1] - acum_row[hh:hh + 1, :]
            decay = jnp.exp(jnp.where(causal, diff, -jnp.inf))
            y_r = _dot((cb * decay).astype(BF16), xdb)
            y_diag = y_r if r == 0 else jnp.where(head_of_lane == r, y_r, y_diag)
        state_ref[g] = (st * jnp.exp(a_last)
                        + _dot(bm.T.astype(BF16), (xd * jnp.exp(a_last - acum_ch)).astype(BF16)))
        y = y_off * jnp.exp(acum_ch) + y_diag + dsk_ref[:, cs] * xs
        zz = z_ref[0, :, cs]
        gated = y * (zz * _sigmoid(zz))
        o_ref[0, :, cs] = _rms(gated, nw_ref[:, cs]).astype(o_ref.dtype)

    _one_ahead(range(SSM_GROUPS), start, finish)
    xbuf_ref[0:8, :] = xbuf_ref[q:q + 8, :]


def ssd_core(zx, conv_w, conv_b, dt_bias, a_log, d_skip, norm_w, b, s):
    q = SSM_CHUNK
    zx3 = zx.reshape(b, s, SSM_IN_PAD)
    pad = lambda v: jnp.pad(v.astype(F32), (0, LANES - SSM_HEADS)).reshape(1, LANES)
    tril = jnp.asarray(np.tril(np.ones((q, q), np.float32)))
    wide = SSM_D_INNER
    full = lambda shape: pl.BlockSpec(shape, lambda i, c: (0,) * len(shape))
    out = pl.pallas_call(
        _ssd_kernel,
        grid=(b, s // q),
        in_specs=[pl.BlockSpec((1, q, wide), lambda i, c: (i, c, 0)),
                  pl.BlockSpec((1, q, wide), lambda i, c: (i, c, 1)),
                  pl.BlockSpec((1, q, wide), lambda i, c: (i, c, 2)),
                  pl.BlockSpec((1, q, LANES), lambda i, c: (i, c, (SSM_D_INNER + SSM_CONV_DIM) // LANES)),
                  full((SSM_CONV, SSM_CONV_DIM)), full((1, SSM_CONV_DIM)),
                  full((1, LANES)), full((1, LANES)), full((1, wide)), full((1, wide)),
                  full((q, q))],
        out_specs=pl.BlockSpec((1, q, wide), lambda i, c: (i, c, 0)),
        out_shape=jax.ShapeDtypeStruct((b, s, wide), BF16),
        scratch_shapes=[pltpu.VMEM((SSM_GROUPS, SSM_STATE, GQA_R * HEAD_DIM), F32),
                        pltpu.VMEM((q + 8, SSM_CONV_DIM), F32)],
        compiler_params=_cparams("parallel", "arbitrary"),
        name="ssd_core",
    )(zx3, zx3, zx3, zx3, conv_w.astype(F32), conv_b.reshape(1, -1).astype(F32),
      pad(dt_bias), pad(a_log), jnp.repeat(d_skip.astype(F32), HEAD_DIM).reshape(1, wide),
      norm_w.reshape(1, wide).astype(F32), tril)
    return out.reshape(b * s, wide)


def _swa_kernel(q_ref, kc_ref, kp_ref, vtc_ref, vtp_ref, sink_ref, o_ref):
    i = pl.program_id(1)
    w = SWA_BLOCK
    rq = GQA_R * w
    lane = lax.broadcasted_iota(jnp.int32, (1, rq), 1)
    qloc = lane & (w - 1)
    head_of_lane = lane // w
    kk = lax.broadcasted_iota(jnp.int32, (2 * w, rq), 0)
    valid = ((kk < w) & (kk > qloc) & (i > 0)) | ((kk >= w) & ((kk - w) <= qloc))
    scale = jnp.asarray(ATTN_SCALE, BF16)

    def start(kv):
        dsl = slice(HEAD_DIM * kv, HEAD_DIM * (kv + 1))
        qs = jnp.concatenate(
            [q_ref[0, :, HEAD_DIM * (GQA_R * kv + r):HEAD_DIM * (GQA_R * kv + r + 1)] for r in range(GQA_R)],
            axis=0) * scale
        k = jnp.concatenate([kp_ref[0, :, dsl], kc_ref[0, :, dsl]], axis=0)
        return _dot_nt(k, qs)

    def finish(kv, st):
        dsl = slice(HEAD_DIM * kv, HEAD_DIM * (kv + 1))
        sink = jnp.zeros((1, rq), F32)
        for r in range(GQA_R):
            h = GQA_R * kv + r
            sink = jnp.where(head_of_lane == r, sink_ref[:, h:h + 1], sink)
        sc = jnp.where(valid, st, -jnp.inf)
        m = jnp.maximum(jnp.max(sc, axis=0, keepdims=True), sink)
        e = jnp.exp(sc - m)
        den = jnp.sum(e, axis=0, keepdims=True) + jnp.exp(sink - m)
        vt = jnp.concatenate([vtp_ref[0, 0, dsl, :], vtc_ref[0, 0, dsl, :]], axis=1)
        o_q = (_dot(vt, e.astype(BF16)) * (1.0 / den)).T
        o_ref[0, :, KV_DIM * kv:KV_DIM * (kv + 1)] = jnp.concatenate(
            [o_q[w * r:w * (r + 1)] for r in range(GQA_R)], axis=1).astype(o_ref.dtype)

    _one_ahead(range(KV_HEADS), start, finish, depth=2)


def swa_core(qk, vt, sinks, b, s):
    w = SWA_BLOCK
    qkv3 = qk.reshape(b, s, -1)
    qd = N_HEADS * HEAD_DIM
    kcol = qd // KV_DIM
    vt = vt.reshape(b, s // VT_SLAB, KV_DIM, VT_SLAB)
    prev = lambda i, j: (i, jnp.maximum(j - 1, 0))
    out = pl.pallas_call(
        _swa_kernel,
        grid=(b, s // w),
        in_specs=[pl.BlockSpec((1, w, qd), lambda i, j: (i, j, 0)),
                  pl.BlockSpec((1, w, KV_DIM), lambda i, j: (i, j, kcol)),
                  pl.BlockSpec((1, w, KV_DIM), lambda i, j: prev(i, j) + (kcol,)),
                  pl.BlockSpec((1, 1, KV_DIM, VT_SLAB), lambda i, j: (i, j, 0, 0)),
                  pl.BlockSpec((1, 1, KV_DIM, VT_SLAB), lambda i, j: prev(i, j) + (0, 0)),
                  pl.BlockSpec((1, LANES), lambda i, j: (0, 0))],
        out_specs=pl.BlockSpec((1, w, qd), lambda i, j: (i, j, 0)),
        out_shape=jax.ShapeDtypeStruct((b, s, qd), BF16),
        compiler_params=_cparams("parallel", "arbitrary"),
        name="swa_core",
    )(qkv3, qkv3, qkv3, vt, vt, jnp.pad(sinks.astype(F32), (0, LANES - N_HEADS)).reshape(1, LANES))
    return out.reshape(b * s, qd)


def _compress_kernel(f_ref, pe_ref, w1_ref, w2_ref, o_ref, *, transposed):
    half = NSA_CMP_STRIDE * HEAD_DIM
    x = f_ref[0, 0]
    za = (x + pe_ref[:, 0:half]).astype(BF16)
    zb = (x + pe_ref[:, half:2 * half]).astype(BF16)
    hb = _dot(zb, w1_ref[half:2 * half, :])
    hid = _dot(za, w1_ref[0:half, :]) + pltpu.roll(hb, hb.shape[0] - 1, 0)
    act = (hid * _sigmoid(hid)).astype(BF16)
    if transposed:
        o_ref[0, 0] = _dot_nt(w2_ref[...], act).astype(o_ref.dtype)
    else:
        o_ref[0, 0] = _dot(act, w2_ref[...]).astype(o_ref.dtype)


def compress(tok, pe, w1, w2, b, s, transposed):
    nch = s // NSA_CMP_STRIDE
    half = NSA_CMP_STRIDE * HEAD_DIM
    f = tok.reshape(b, nch, NSA_CMP_STRIDE, KV_HEADS, HEAD_DIM).transpose(0, 3, 1, 2, 4).reshape(b, KV_HEADS, nch, half)
    w2 = (w2.T if transposed else w2).astype(BF16)
    oshape = (HEAD_DIM, nch) if transposed else (nch, HEAD_DIM)
    return pl.pallas_call(
        functools.partial(_compress_kernel, transposed=transposed),
        grid=(b, KV_HEADS),
        in_specs=[pl.BlockSpec((1, 1, nch, half), lambda i, j: (i, j, 0, 0)),
                  pl.BlockSpec((1, 2 * half), lambda i, j: (0, 0)),
                  pl.BlockSpec((2 * half, NSA_CMP_HIDDEN), lambda i, j: (0, 0)),
                  pl.BlockSpec(w2.shape, lambda i, j: (0, 0))],
        out_specs=pl.BlockSpec((1, 1) + oshape, lambda i, j: (i, j, 0, 0)),
        out_shape=jax.ShapeDtypeStruct((b, KV_HEADS) + oshape, BF16),
        compiler_params=_cparams("parallel", "parallel"),
        name="nsa_compress",
    )(f, pe.reshape(1, 2 * half).astype(F32), w1.astype(BF16), w2)


def _dot_split3(a_bf16, x):
    hi = x.astype(BF16)
    r1 = x - hi.astype(F32)
    mid = r1.astype(BF16)
    lo = (r1 - mid.astype(F32)).astype(BF16)
    return _dot(a_bf16, hi) + _dot(a_bf16, mid) + _dot(a_bf16, lo)


def _nsa_kernel(q_ref, ks_ref, vst_ref, kw_ref, vwt_ref, kcmp_ref, vcmpt_ref, g_ref, ov_ref,
                o_ref, qt_ref, bias_ref, ocmp_ref, owin_ref, acc_ref, m_ref, l_ref, st_ref, *, seq):
    i = pl.program_id(1)
    qb = NSA_Q_BLOCK
    rq = GQA_R * qb
    n_cmp = seq // NSA_CMP_STRIDE
    n_sel = seq // NSA_SEL_BLOCK
    blocks_per_chunk = NSA_SEL_CHUNK // NSA_SEL_BLOCK
    slabs_per_chunk = NSA_SEL_CHUNK // VT_SLAB
    t0 = i * qb
    qpos = t0 + (lax.broadcasted_iota(jnp.int32, (1, rq), 1) & (qb - 1))
    gates_t = _sigmoid(g_ref[0]).T
    q_t = q_ref[0].astype(F32).T * ATTN_SCALE

    cmp_end = lax.broadcasted_iota(jnp.int32, (n_cmp, rq), 0) * NSA_CMP_STRIDE + (NSA_CMP_LEN - 1)
    mask_c = cmp_end <= qpos
    has_cmp = jnp.where(qpos >= NSA_CMP_LEN - 1, 1.0, 0.0)
    blk = lax.broadcasted_iota(jnp.int32, (n_sel, LANES), 0)
    blk_f = blk.astype(F32)
    forced = (blk == i) | (blk == 0)
    allowed = blk <= i
    lane_lo = lax.broadcasted_iota(jnp.int32, (n_sel, LANES), 1) < qb
    importance = {}
    cw = jnp.maximum(t0 - NSA_WINDOW, 0) // VT_SLAB
    wkeys = NSA_WIN_SLABS * VT_SLAB
    wstart = pl.multiple_of(cw * VT_SLAB, VT_SLAB)
    delta_w = qpos - (wstart + lax.broadcasted_iota(jnp.int32, (wkeys, rq), 0))
    mask_w = (delta_w >= 0) & (delta_w < NSA_WINDOW)
    last_chunk = i // blocks_per_chunk
    key_in_blk = lax.broadcasted_iota(jnp.int32, (NSA_SEL_BLOCK, rq), 0)
    groups = range(KV_HEADS)

    def cmp_start(kv):
        qt = jnp.concatenate(
            [q_t[HEAD_DIM * (GQA_R * kv + r):HEAD_DIM * (GQA_R * kv + r + 1), :] for r in range(GQA_R)],
            axis=1).astype(BF16)
        qt_ref[kv] = qt
        return _dot(kcmp_ref[0, kv], qt)

    def cmp_finish(kv, logits):
        lc = jnp.where(mask_c, logits, MASK_NEG)
        ec = jnp.exp(lc - jnp.max(lc, axis=0, keepdims=True))
        inv = has_cmp / jnp.sum(ec, axis=0, keepdims=True)
        ocmp_ref[kv] = _dot(vcmpt_ref[0, kv], ec.astype(BF16)) * inv
        imp = _dot_split3(ov_ref[...], ec) * inv
        imp = imp + pltpu.roll(imp, qb, 1)
        imp = imp + pltpu.roll(imp, 2 * qb, 1)
        m_ref[kv] = jnp.full((1, rq), MASK_NEG, F32)
        l_ref[kv] = jnp.zeros((1, rq), F32)
        acc_ref[kv] = jnp.zeros((HEAD_DIM, rq), F32)
        importance[kv] = imp[:, 0:LANES]
        if kv % 2 == 1:
            select_pair(kv - 1, kv)

    def select_pair(ka, kb):
        v = jnp.where(lane_lo, importance[ka], importance[kb])
        v = jnp.where(allowed & jnp.logical_not(forced), v, -jnp.inf)
        sel = forced
        for _ in range(NSA_TOP_N - 2):
            best = jnp.max(v, axis=0, keepdims=True)
            first = jnp.min(jnp.where(v == best, blk_f, float(n_sel)), axis=0, keepdims=True)
            pick = blk_f == first
            sel = sel | pick
            v = jnp.where(pick, -jnp.inf, v)
        bias = jnp.where(sel & allowed, 0.0, MASK_NEG)
        swapped = pltpu.roll(bias, qb, 1)
        for kv, own_lo in ((ka, True), (kb, False)):
            half = jnp.where(lane_lo, bias, swapped) if own_lo else jnp.where(lane_lo, swapped, bias)
            bias_ref[kv] = jnp.concatenate([half, half], axis=1)

    def cmp_phase_start(item):
        if not isinstance(item, tuple):
            return cmp_start(item)
        kind, kv = item
        if kind == "window":
            return _dot(kw_ref[0, pl.ds(wstart, wkeys), HEAD_DIM * kv:HEAD_DIM * (kv + 1)], qt_ref[kv])
        return sel_start(0, kv)

    def cmp_phase_finish(item, res):
        if not isinstance(item, tuple):
            cmp_finish(item, res)
            return
        kind, kv = item
        if kind == "first_chunk":
            st_ref[kv] = res
            return
        dsl = slice(HEAD_DIM * kv, HEAD_DIM * (kv + 1))
        lw = jnp.where(mask_w, res, MASK_NEG)
        ew = jnp.exp(lw - jnp.max(lw, axis=0, keepdims=True))
        vwt = jnp.concatenate([vwt_ref[0, cw + u, dsl, :] for u in range(NSA_WIN_SLABS)], axis=1)
        owin_ref[kv] = _dot(vwt, ew.astype(BF16)) * (1.0 / jnp.sum(ew, axis=0, keepdims=True))

    def sel_start(c, kv):
        k0 = pl.multiple_of(c * NSA_SEL_CHUNK, NSA_SEL_CHUNK)
        return _dot(ks_ref[0, pl.ds(k0, NSA_SEL_CHUNK), HEAD_DIM * kv:HEAD_DIM * (kv + 1)], qt_ref[kv])

    def sel_finish(c, kv, st, causal):
        dsl = slice(HEAD_DIM * kv, HEAD_DIM * (kv + 1))
        k0 = c * NSA_SEL_CHUNK
        bias8 = bias_ref[kv, pl.ds(pl.multiple_of(c * blocks_per_chunk, blocks_per_chunk), blocks_per_chunk), :]
        rows = []
        for jb in range(blocks_per_chunk):
            sj = st[NSA_SEL_BLOCK * jb:NSA_SEL_BLOCK * (jb + 1)] + bias8[jb:jb + 1, :]
            if causal:
                sj = jnp.where(k0 + NSA_SEL_BLOCK * jb + key_in_blk <= qpos, sj, MASK_NEG)
            rows.append(sj)
        sc = jnp.concatenate(rows, axis=0)
        m_run = m_ref[kv]
        m_new = jnp.maximum(m_run, jnp.max(sc, axis=0, keepdims=True))
        p = jnp.exp(sc - m_new)
        alpha = jnp.exp(m_run - m_new)
        vt = jnp.concatenate([vst_ref[0, slabs_per_chunk * c + u, dsl, :] for u in range(slabs_per_chunk)],
                             axis=1)
        m_ref[kv] = m_new
        l_ref[kv] = alpha * l_ref[kv] + jnp.sum(p, axis=0, keepdims=True)
        acc_ref[kv] = alpha * acc_ref[kv] + _dot(vt, p.astype(BF16))

    ahead = NSA_AHEAD
    _one_ahead(list(groups) + [("window", kv) for kv in groups] + [("first_chunk", kv) for kv in range(ahead)],
               cmp_phase_start, cmp_phase_finish, depth=ahead)

    def sel_trip(c, carry):
        pending = [st_ref[k] for k in range(ahead)]
        for kv in groups:
            nk = kv + ahead
            if nk < KV_HEADS:
                pending.append(sel_start(c, nk))
            elif nk == KV_HEADS:
                parked = [sel_start(c + 1, k) for k in range(ahead)]
            sel_finish(c, kv, pending.pop(0), False)
            if nk == KV_HEADS:
                for k in range(ahead):
                    st_ref[k] = parked[k]
        return carry

    lax.fori_loop(0, last_chunk, sel_trip, 0)

    def tail_finish(kv, logits):
        sel_finish(last_chunk, kv, logits, True)
        o_sel = acc_ref[kv] * (1.0 / l_ref[kv])

        def gate_row(branch):
            return jnp.concatenate(
                [gates_t[3 * (GQA_R * kv + r) + branch:3 * (GQA_R * kv + r) + branch + 1, :] for r in range(GQA_R)],
                axis=1)
        o_t = gate_row(0) * ocmp_ref[kv] + gate_row(1) * o_sel + gate_row(2) * owin_ref[kv]
        o_q = o_t.T
        o_ref[0, :, KV_DIM * kv:KV_DIM * (kv + 1)] = jnp.concatenate(
            [o_q[qb * r:qb * (r + 1)] for r in range(GQA_R)], axis=1).astype(o_ref.dtype)

    _one_ahead(groups, functools.partial(sel_start, last_chunk), tail_finish,
               depth=ahead, primed=[st_ref[k] for k in range(ahead)])


def nsa_core(pa, vt, pb, k_cmp, v_cmp_t, b, s):
    qb = NSA_Q_BLOCK
    qd = N_HEADS * HEAD_DIM
    n_cmp = s // NSA_CMP_STRIDE
    n_sel = s // NSA_SEL_BLOCK
    n_slab = s // VT_SLAB
    pa3 = pa.reshape(b, s, -1)
    pb3 = pb.reshape(b, s, -1)
    vt4 = vt.reshape(b, n_slab, 2 * KV_DIM, VT_SLAB)
    sel_lo = np.arange(n_sel)[:, None] * NSA_SEL_BLOCK
    cmp_lo = np.arange(n_cmp)[None, :] * NSA_CMP_STRIDE
    ov = np.clip(np.minimum(sel_lo + NSA_SEL_BLOCK, cmp_lo + NSA_CMP_LEN) - np.maximum(sel_lo, cmp_lo), 0, None)
    ov = jnp.asarray((ov / NSA_CMP_LEN).astype(np.float32), BF16)
    kcol = qd // KV_DIM
    seqspec = lambda col: pl.BlockSpec((1, s, KV_DIM), lambda i, j: (i, 0, col))
    slabspec = lambda k: pl.BlockSpec((1, n_slab, KV_DIM, VT_SLAB), lambda i, j: (i, 0, k, 0))
    rq = GQA_R * qb
    out = pl.pallas_call(
        functools.partial(_nsa_kernel, seq=s),
        grid=(b, s // qb),
        in_specs=[pl.BlockSpec((1, qb, qd), lambda i, j: (i, j, 0)),
                  seqspec(kcol), slabspec(0), seqspec(kcol + 1), slabspec(1),
                  pl.BlockSpec((1, KV_HEADS, n_cmp, HEAD_DIM), lambda i, j: (i, 0, 0, 0)),
                  pl.BlockSpec((1, KV_HEADS, HEAD_DIM, n_cmp), lambda i, j: (i, 0, 0, 0)),
                  pl.BlockSpec((1, qb, KV_DIM), lambda i, j: (i, j, 2)),
                  pl.BlockSpec((n_sel, n_cmp), lambda i, j: (0, 0))],
        out_specs=pl.BlockSpec((1, qb, qd), lambda i, j: (i, j, 0)),
        out_shape=jax.ShapeDtypeStruct((b, s, qd), BF16),
        scratch_shapes=[pltpu.VMEM((KV_HEADS, HEAD_DIM, rq), BF16),
                        pltpu.VMEM((KV_HEADS, n_sel, rq), F32),
                        pltpu.VMEM((KV_HEADS, HEAD_DIM, rq), F32),
                        pltpu.VMEM((KV_HEADS, HEAD_DIM, rq), F32),
                        pltpu.VMEM((KV_HEADS, HEAD_DIM, rq), F32),
                        pltpu.VMEM((KV_HEADS, 1, rq), F32),
                        pltpu.VMEM((KV_HEADS, 1, rq), F32),
                        pltpu.VMEM((NSA_AHEAD, NSA_SEL_CHUNK, rq), F32)],
        compiler_params=_cparams("parallel", "arbitrary"),
        name="nsa_core",
    )(pa3, pa3, vt4, pa3, vt4, k_cmp, v_cmp_t, pb3, ov)
    return out.reshape(b * s, qd)


def _rope_tables(positions):
    half = ROPE_DIM // 2
    inv_freq = ROPE_THETA ** (-jnp.arange(0, ROPE_DIM, 2, dtype=F32) / ROPE_DIM)
    ang = positions.astype(F32).reshape(-1)[:, None] * inv_freq
    cos, sin = jnp.cos(ang), jnp.sin(ang)
    t = cos.shape[0]
    ones = jnp.ones((t, HEAD_DIM - ROPE_DIM), F32)
    zeros = jnp.zeros((t, HEAD_DIM - ROPE_DIM), F32)
    zh = jnp.zeros((t, half), F32)
    c = jnp.concatenate([cos, cos, ones], axis=1)
    sa = jnp.concatenate([-sin, zh, zeros], axis=1)
    sb = jnp.concatenate([zh, sin, zeros], axis=1)
    return tuple(jnp.tile(v, (1, LANES // HEAD_DIM)) for v in (c, sa, sb))


def _pad_cols(w, n):
    return jnp.pad(w, ((0, 0), (0, n - w.shape[1])))


def mamba2_mixer(h, ln_w, w_in, conv_w, conv_b, dt_bias, a_log, d_skip, norm_w, w_out, b, s):
    w = _pad_cols(w_in, SSM_IN_PAD).astype(BF16)
    zx = norm_proj(h, ln_w, w, jnp.zeros((SSM_IN_PAD,), F32), F32, tm=SSM_IN_TOKEN_TILE, row_splits=2)
    y = ssd_core(zx, conv_w, conv_b, dt_bias, a_log, d_skip, norm_w, b, s)
    return out_proj(y, w_out.astype(BF16), jnp.zeros((D_MODEL,), F32), h)


def swa_mixer(h, ln_w, w_qkv, b_qkv, sinks, w_o, b_o, rope, b, s):
    qk = (N_HEADS + KV_HEADS) * HEAD_DIM
    w = w_qkv.astype(BF16)
    bias = b_qkv.astype(F32)
    q_k, vt = norm_proj(h, ln_w, w[:, :qk], bias[:qk], BF16, rope=rope, rope_cols=((0, qk),),
                        w_t=w[:, qk:].T, bias_t=bias[qk:])
    o = swa_core(q_k, vt, sinks, b, s)
    return out_proj(o, w_o.astype(BF16), b_o.astype(F32), h)


def nsa_mixer(h, ln_w, w_in, pe_k, k_w1, k_w2, pe_v, v_w1, v_w2, w_o, rope, b, s):
    qd = N_HEADS * HEAD_DIM
    cols = lambda k: w_in[:, qd + KV_DIM * k:qd + KV_DIM * (k + 1)]
    wa = jnp.concatenate([w_in[:, :qd], cols(2), cols(4)], axis=1).astype(BF16)
    wv_t = jnp.concatenate([cols(3), cols(5)], axis=1).astype(BF16).T
    wb = _pad_cols(jnp.concatenate([cols(0), cols(1), w_in[:, qd + 6 * KV_DIM:]], axis=1), 3 * KV_DIM).astype(BF16)
    pa, vt = norm_proj(h, ln_w, wa, jnp.zeros((wa.shape[1],), F32), BF16, rope=rope,
                       rope_cols=((0, wa.shape[1]),), w_t=wv_t, bias_t=jnp.zeros((wv_t.shape[0],), F32))
    pb = norm_proj(h, ln_w, wb, jnp.zeros((wb.shape[1],), F32), F32, rope=rope, rope_cols=((0, KV_DIM),))
    pb3 = pb.reshape(b, s, -1)
    k_cmp = compress(pb3[..., 0:KV_DIM], pe_k, k_w1, k_w2, b, s, False)
    v_cmp_t = compress(pb3[..., KV_DIM:2 * KV_DIM], pe_v, v_w1, v_w2, b, s, True)
    o = nsa_core(pa, vt, pb, k_cmp, v_cmp_t, b, s)
    return out_proj(o, w_o.astype(BF16), jnp.zeros((D_MODEL,), F32), h)


def kernel(x, positions, ln_ffn1, ffn1_w_in, ffn1_w_out, ln_mix, ln_ffn2, ffn2_w_in, ffn2_w_out, ssm_w_in, ssm_conv_w, ssm_conv_b, ssm_dt_bias, ssm_a_log, ssm_d, ssm_norm_w, ssm_w_out, swa_w_qkv, swa_b_qkv, swa_sinks, swa_w_o, swa_b_o, nsa_w_in, nsa_pe_k, nsa_k_w1, nsa_k_w2, nsa_pe_v, nsa_v_w1, nsa_v_w2, nsa_w_o, final_norm):
    b, s, d = x.shape
    depth = ln_ffn1.shape[0]
    rope = _rope_tables(positions)
    h = x.reshape(b * s, d)
    for i in range(depth):
        kind, inst = i % 3, i // 3
        h = ffn(h, ln_ffn1[i], ffn1_w_in[i], ffn1_w_out[i])
        if kind == 0:
            h = mamba2_mixer(h, ln_mix[i], ssm_w_in[inst], ssm_conv_w[inst], ssm_conv_b[inst],
                             ssm_dt_bias[inst], ssm_a_log[inst], ssm_d[inst], ssm_norm_w[inst],
                             ssm_w_out[inst], b, s)
        elif kind == 1:
            h = swa_mixer(h, ln_mix[i], swa_w_qkv[inst], swa_b_qkv[inst], swa_sinks[inst],
                          swa_w_o[inst], swa_b_o[inst], rope, b, s)
        else:
            h = nsa_mixer(h, ln_mix[i], nsa_w_in[inst], nsa_pe_k[inst], nsa_k_w1[inst], nsa_k_w2[inst],
                          nsa_pe_v[inst], nsa_v_w1[inst], nsa_v_w2[inst], nsa_w_o[inst], rope, b, s)
        h = ffn(h, ln_ffn2[i], ffn2_w_in[i], ffn2_w_out[i], final_norm if i == depth - 1 else None)
    return h.reshape(b, s, d)
```

```python
import functools
import math

import numpy as np
import jax
import jax.numpy as jnp
from jax import lax
from jax.experimental import pallas as pl
from jax.experimental.pallas import tpu as pltpu

F32 = jnp.float32
BF16 = jnp.bfloat16

D_MODEL = 1024
RMS_EPS = 1e-6
D_FF = 2816
HEAD_DIM = 64
ROPE_DIM = HEAD_DIM // 4
ROPE_THETA = 500000.0
ATTN_SCALE = HEAD_DIM ** -0.5
MASK_NEG = -1e30

SSM_D_INNER = 2 * D_MODEL
SSM_HEADS = 32
SSM_GROUPS = 8
SSM_STATE = 128
SSM_CONV = 4
SSM_CHUNK = 128
SSM_CONV_DIM = SSM_D_INNER + 2 * SSM_GROUPS * SSM_STATE
SSM_IN_PAD = SSM_D_INNER + SSM_CONV_DIM + 256
SSM_IN_TOKEN_TILE = 512

N_HEADS = 16
KV_HEADS = 4
GQA_R = N_HEADS // KV_HEADS
KV_DIM = KV_HEADS * HEAD_DIM

SWA_BLOCK = 128

NSA_CMP_LEN = 32
NSA_CMP_STRIDE = 16
NSA_CMP_HIDDEN = 256
NSA_SEL_BLOCK = 64
NSA_TOP_N = 8
NSA_WINDOW = 512
NSA_Q_BLOCK = 64
NSA_SEL_CHUNK = 512
NSA_AHEAD = 2
VT_SLAB = 128
NSA_WIN_SLABS = NSA_WINDOW // VT_SLAB + 1

LANES = 128
VMEM_LIMIT = 56 * 1024 * 1024
TOKEN_TILE = 1024
ROW_SPLITS = 4
FFN_TOKEN_TILE = 1024
FFN_ROW_SPLITS = 4


def _cparams(*sem):
    return pltpu.CompilerParams(dimension_semantics=sem, vmem_limit_bytes=VMEM_LIMIT)


def _dot(a, b):
    return jnp.dot(a, b, preferred_element_type=F32)


def _dot_nt(a, b):
    return lax.dot_general(a, b, (((1,), (1,)), ((), ())), preferred_element_type=F32)


def _sigmoid(x):
    return 1.0 / (1.0 + jnp.exp(-x))


def _rms(x, w):
    return x * lax.rsqrt(jnp.mean(x * x, axis=-1, keepdims=True) + RMS_EPS) * w


def _one_ahead(items, start, finish, depth=1, primed=()):
    items = list(items)
    pending = list(primed)
    nxt = len(pending)
    while nxt < min(depth, len(items)):
        pending.append(start(items[nxt]))
        nxt += 1
    for item in items:
        if nxt < len(items):
            pending.append(start(items[nxt]))
            nxt += 1
        finish(item, pending.pop(0))


def _rope_store(o_ref, rows, acc, rope_segs, c, sa, sb):
    for k, roped in enumerate(rope_segs):
        seg = acc[:, LANES * k:LANES * (k + 1)]
        if roped:
            seg = (seg * c + pltpu.roll(seg, LANES - ROPE_DIM // 2, 1) * sa
                   + pltpu.roll(seg, ROPE_DIM // 2, 1) * sb)
        o_ref[rows, LANES * k:LANES * (k + 1)] = seg.astype(o_ref.dtype)


def _norm_proj_kernel(*refs, rope_segs, col_tiles, with_vt, row_splits):
    refs = list(refs)
    x_ref, lnw_ref, w_ref, b_ref = refs[:4]
    del refs[:4]
    if rope_segs is not None:
        c_ref, sa_ref, sb_ref = refs[:3]
        del refs[:3]
    if with_vt:
        wt_ref, bt_ref = refs[:2]
        del refs[:2]
    o_ref = refs.pop(0)
    vt_ref = refs.pop(0) if with_vt else None
    xn_ref = refs.pop(0) if col_tiles > 1 else None
    tm = x_ref.shape[0]
    rt = tm // row_splits
    row_tiles = [slice(rt * h, rt * (h + 1)) for h in range(row_splits)]

    if col_tiles > 1:
        @pl.when(pl.program_id(1) == 0)
        def _():
            xn_ref[...] = _rms(x_ref[...], lnw_ref[...]).astype(BF16)

    def start(rows):
        xn = xn_ref[rows, :] if col_tiles > 1 else _rms(x_ref[rows, :], lnw_ref[...]).astype(BF16)
        acc = _dot(xn, w_ref[pl.program_id(1) if col_tiles > 1 else 0]) + b_ref[...]
        return (acc, _dot_nt(wt_ref[...], xn) + bt_ref[...]) if with_vt else (acc, None)

    def finish(rows, res):
        acc, vt = res
        if rope_segs is None:
            o_ref[rows, :] = acc.astype(o_ref.dtype)
        else:
            _rope_store(o_ref, rows, acc, rope_segs, c_ref[rows, :], sa_ref[rows, :], sb_ref[rows, :])
        if with_vt:
            for u in range(rt // VT_SLAB):
                vt_ref[rows.start // VT_SLAB + u] = vt[:, VT_SLAB * u:VT_SLAB * (u + 1)].astype(vt_ref.dtype)

    _one_ahead(row_tiles, start, finish)


def norm_proj(h, ln_w, w, bias, out_dtype, tn=None, rope=None, rope_cols=(), w_t=None, bias_t=None,
              tm=TOKEN_TILE, row_splits=ROW_SPLITS):
    t, d = h.shape
    n = w.shape[1]
    tm = min(tm, t)
    tn = n if tn is None else tn
    col_tiles = n // tn
    assert n == col_tiles * tn and t % tm == 0 and tn % LANES == 0
    with_vt = w_t is not None
    rope_segs = None
    if rope_cols:
        assert col_tiles == 1
        rope_segs = tuple(any(lo <= LANES * k < hi for lo, hi in rope_cols) for k in range(n // LANES))
    w_tiles = w.reshape(d, col_tiles, tn).transpose(1, 0, 2)
    in_specs = [pl.BlockSpec((tm, d), lambda i, j: (i, 0)),
                pl.BlockSpec((1, d), lambda i, j: (0, 0)),
                pl.BlockSpec((col_tiles, d, tn), lambda i, j: (0, 0, 0), pipeline_mode=pl.Buffered(1)),
                pl.BlockSpec((1, tn), lambda i, j: (0, j))]
    args = [h, ln_w.reshape(1, d), w_tiles, bias.reshape(1, n)]
    if rope_segs is not None:
        in_specs += [pl.BlockSpec((tm, LANES), lambda i, j: (i, 0))] * 3
        args += list(rope)
    out_specs = [pl.BlockSpec((tm, tn), lambda i, j: (i, j))]
    out_shape = [jax.ShapeDtypeStruct((t, n), out_dtype)]
    if with_vt:
        assert col_tiles == 1
        nv = w_t.shape[0]
        in_specs += [pl.BlockSpec((nv, d), lambda i, j: (0, 0)), pl.BlockSpec((nv, 1), lambda i, j: (0, 0))]
        args += [w_t, bias_t.reshape(nv, 1)]
        out_specs.append(pl.BlockSpec((tm // VT_SLAB, nv, VT_SLAB), lambda i, j: (i, 0, 0)))
        out_shape.append(jax.ShapeDtypeStruct((t // VT_SLAB, nv, VT_SLAB), BF16))
    res = pl.pallas_call(
        functools.partial(_norm_proj_kernel, rope_segs=rope_segs, col_tiles=col_tiles, with_vt=with_vt,
                          row_splits=row_splits),
        grid=(t // tm, col_tiles),
        in_specs=in_specs,
        out_specs=out_specs,
        out_shape=out_shape,
        scratch_shapes=[pltpu.VMEM((tm, d), BF16)] if col_tiles > 1 else [],
        compiler_params=_cparams("parallel", "arbitrary"),
        name="norm_proj",
    )(*args)
    return tuple(res) if with_vt else res[0]


def _out_proj_kernel(y_ref, w_ref, b_ref, r_ref, o_ref):
    o_ref[...] = r_ref[...] + _dot(y_ref[...], w_ref[...]) + b_ref[...]


def out_proj(y, w, bias, res):
    t, k = y.shape
    d = w.shape[1]
    tm = min(TOKEN_TILE, t)
    return pl.pallas_call(
        _out_proj_kernel,
        grid=(t // tm,),
        in_specs=[pl.BlockSpec((tm, k), lambda i: (i, 0)),
                  pl.BlockSpec((k, d), lambda i: (0, 0)),
                  pl.BlockSpec((1, d), lambda i: (0, 0)),
                  pl.BlockSpec((tm, d), lambda i: (i, 0))],
        out_specs=pl.BlockSpec((tm, d), lambda i: (i, 0)),
        out_shape=jax.ShapeDtypeStruct((t, d), F32),
        compiler_params=_cparams("parallel"),
        name="out_proj",
    )(y, w, bias.reshape(1, d), res)


def _ffn_kernel(x_ref, lnw_ref, wi_ref, wo_ref, fnw_ref, o_ref, *, final_norm):
    tm = x_ref.shape[0]
    rt = tm // FFN_ROW_SPLITS
    dff = wo_ref.shape[0]

    def start(rows):
        xn = _rms(x_ref[rows, :], lnw_ref[...]).astype(BF16)
        return _dot(xn, wi_ref[:, 0:dff]), _dot(xn, wi_ref[:, dff:2 * dff])

    def finish(rows, gu):
        g, u = gu
        y = x_ref[rows, :] + 0.5 * _dot((g * _sigmoid(g) * u).astype(BF16), wo_ref[...])
        if final_norm:
            y = _rms(y, fnw_ref[...])
        o_ref[rows, :] = y

    _one_ahead([slice(rt * h, rt * (h + 1)) for h in range(FFN_ROW_SPLITS)], start, finish)


def ffn(h, ln_w, w_in, w_out, final_w=None):
    t, d = h.shape
    tm = min(FFN_TOKEN_TILE, t)
    final_norm = final_w is not None
    fnw = (final_w if final_norm else ln_w).reshape(1, d)
    resident = lambda shape: pl.BlockSpec(shape, lambda i: (0, 0), pipeline_mode=pl.Buffered(1))
    return pl.pallas_call(
        functools.partial(_ffn_kernel, final_norm=final_norm),
        grid=(t // tm,),
        in_specs=[pl.BlockSpec((tm, d), lambda i: (i, 0)),
                  pl.BlockSpec((1, d), lambda i: (0, 0)),
                  resident(w_in.shape),
                  resident(w_out.shape),
                  pl.BlockSpec((1, d), lambda i: (0, 0))],
        out_specs=pl.BlockSpec((tm, d), lambda i: (i, 0)),
        out_shape=jax.ShapeDtypeStruct((t, d), F32),
        compiler_params=_cparams("parallel"),
        name="ffn",
    )(h, ln_w.reshape(1, d), w_in.astype(BF16), w_out.astype(BF16), fnw)


def _expand_heads(mat, g):
    rows = mat.shape[0]
    lane = lax.broadcasted_iota(jnp.int32, (rows, LANES), 1)
    pieces = []
    for p in range(2):
        h0 = GQA_R * g + 2 * p
        a = jnp.broadcast_to(mat[:, h0:h0 + 1], (rows, LANES))
        b = jnp.broadcast_to(mat[:, h0 + 1:h0 + 2], (rows, LANES))
        pieces.append(jnp.where(lane < HEAD_DIM, a, b))
    return jnp.concatenate(pieces, axis=1)


def _ssd_kernel(z_ref, x_ref, bc_ref, dt_ref, cw_ref, cb_ref, dtb_ref, alog_ref, dsk_ref, nw_ref,
                tril_ref, o_ref, state_ref, xbuf_ref):
    q = SSM_CHUNK
    gw = GQA_R * HEAD_DIM
    c = pl.program_id(1)

    @pl.when(c == 0)
    def _():
        state_ref[...] = jnp.zeros(state_ref.shape, F32)
        xbuf_ref[0:8, :] = jnp.zeros((8, SSM_CONV_DIM), F32)

    xbuf_ref[8:8 + q, 0:SSM_D_INNER] = x_ref[0]
    xbuf_ref[8:8 + q, SSM_D_INNER:SSM_CONV_DIM] = bc_ref[0]

    def conv_silu(c0, width):
        acc = cb_ref[:, c0:c0 + width]
        for k in reversed(range(SSM_CONV)):
            r0 = 8 - (SSM_CONV - 1) + k
            acc = acc + cw_ref[k:k + 1, c0:c0 + width] * xbuf_ref[r0:r0 + q, c0:c0 + width]
        return acc * _sigmoid(acc)

    dtr = dt_ref[0] + dtb_ref[...]
    e_neg = jnp.exp(-jnp.abs(dtr))
    u = 1.0 + e_neg
    log_u = jnp.log2(u) * math.log(2.0)
    dt = jnp.maximum(dtr, 0.0) + jnp.where(u == 1.0, e_neg, log_u * (e_neg / (u - 1.0)))
    ad = dt * (-jnp.exp(alog_ref[...]))
    acum = jnp.dot(tril_ref[...], ad, precision=lax.Precision.HIGHEST,
                   preferred_element_type=F32)
    acum_row = acum.T
    causal = (lax.broadcasted_iota(jnp.int32, (q, q), 0) >= lax.broadcasted_iota(jnp.int32, (q, q), 1))
    head_of_lane = lax.broadcasted_iota(jnp.int32, (q, 2 * LANES), 1) // HEAD_DIM

    def start(g):
        xs = conv_silu(gw * g, gw)
        bm = conv_silu(SSM_D_INNER + SSM_STATE * g, SSM_STATE)
        cm = conv_silu(SSM_D_INNER + SSM_GROUPS * SSM_STATE + SSM_STATE * g, SSM_STATE)
        xd = xs * _expand_heads(dt, g)
        cbf = cm.astype(BF16)
        st = state_ref[g]
        return xs, bm, xd, _dot_nt(cbf, bm.astype(BF16)), _dot(cbf, st.astype(BF16))

    def finish(g, vals):
        xs, bm, xd, cb, y_off = vals
        st = state_ref[g]
        cs = slice(gw * g, gw * (g + 1))
        acum_ch = _expand_heads(acum, g)
        a_last = acum_ch[q - 1:q, :]
        xdb = xd.astype(BF16)
        y_diag = None
        for r in range(GQA_R):
            hh = GQA_R * g + r
            diff = acum[:, hh:hh + 1] - acum_row[hh:hh + 1, :]
            decay = jnp.exp(jnp.where(causal, diff, -jnp.inf))
            y_r = _dot((cb * decay).astype(BF16), xdb)
            y_diag = y_r if r == 0 else jnp.where(head_of_lane == r, y_r, y_diag)
        state_ref[g] = (st * jnp.exp(a_last)
                        + _dot(bm.T.astype(BF16), (xd * jnp.exp(a_last - acum_ch)).astype(BF16)))
        y = y_off * jnp.exp(acum_ch) + y_diag + dsk_ref[:, cs] * xs
        zz = z_ref[0, :, cs]
        gated = y * (zz * _sigmoid(zz))
        o_ref[0, :, cs] = _rms(gated, nw_ref[:, cs]).astype(o_ref.dtype)

    _one_ahead(range(SSM_GROUPS), start, finish)
    xbuf_ref[0:8, :] = xbuf_ref[q:q + 8, :]


def ssd_core(zx, conv_w, conv_b, dt_bias, a_log, d_skip, norm_w, b, s):
    q = SSM_CHUNK
    zx3 = zx.reshape(b, s, SSM_IN_PAD)
    pad = lambda v: jnp.pad(v.astype(F32), (0, LANES - SSM_HEADS)).reshape(1, LANES)
    tril = jnp.asarray(np.tril(np.ones((q, q), np.float32)))
    wide = SSM_D_INNER
    full = lambda shape: pl.BlockSpec(shape, lambda i, c: (0,) * len(shape))
    out = pl.pallas_call(
        _ssd_kernel,
        grid=(b, s // q),
        in_specs=[pl.BlockSpec((1, q, wide), lambda i, c: (i, c, 0)),
                  pl.BlockSpec((1, q, wide), lambda i, c: (i, c, 1)),
                  pl.BlockSpec((1, q, wide), lambda i, c: (i, c, 2)),
                  pl.BlockSpec((1, q, LANES), lambda i, c: (i, c, (SSM_D_INNER + SSM_CONV_DIM) // LANES)),
                  full((SSM_CONV, SSM_CONV_DIM)), full((1, SSM_CONV_DIM)),
                  full((1, LANES)), full((1, LANES)), full((1, wide)), full((1, wide)),
                  full((q, q))],
        out_specs=pl.BlockSpec((1, q, wide), lambda i, c: (i, c, 0)),
        out_shape=jax.ShapeDtypeStruct((b, s, wide), BF16),
        scratch_shapes=[pltpu.VMEM((SSM_GROUPS, SSM_STATE, GQA_R * HEAD_DIM), F32),
                        pltpu.VMEM((q + 8, SSM_CONV_DIM), F32)],
        compiler_params=_cparams("parallel", "arbitrary"),
        name="ssd_core",
    )(zx3, zx3, zx3, zx3, conv_w.astype(F32), conv_b.reshape(1, -1).astype(F32),
      pad(dt_bias), pad(a_log), jnp.repeat(d_skip.astype(F32), HEAD_DIM).reshape(1, wide),
      norm_w.reshape(1, wide).astype(F32), tril)
    return out.reshape(b * s, wide)


def _swa_kernel(q_ref, kc_ref, kp_ref, vtc_ref, vtp_ref, sink_ref, o_ref):
    i = pl.program_id(1)
    w = SWA_BLOCK
    rq = GQA_R * w
    lane = lax.broadcasted_iota(jnp.int32, (1, rq), 1)
    qloc = lane & (w - 1)
    head_of_lane = lane // w
    kk = lax.broadcasted_iota(jnp.int32, (2 * w, rq), 0)
    valid = ((kk < w) & (kk > qloc) & (i > 0)) | ((kk >= w) & ((kk - w) <= qloc))
    scale = jnp.asarray(ATTN_SCALE, BF16)

    def start(kv):
        dsl = slice(HEAD_DIM * kv, HEAD_DIM * (kv + 1))
        qs = jnp.concatenate(
            [q_ref[0, :, HEAD_DIM * (GQA_R * kv + r):HEAD_DIM * (GQA_R * kv + r + 1)] for r in range(GQA_R)],
            axis=0) * scale
        k = jnp.concatenate([kp_ref[0, :, dsl], kc_ref[0, :, dsl]], axis=0)
        return _dot_nt(k, qs)

    def finish(kv, st):
        dsl = slice(HEAD_DIM * kv, HEAD_DIM * (kv + 1))
        sink = jnp.zeros((1, rq), F32)
        for r in range(GQA_R):
            h = GQA_R * kv + r
            sink = jnp.where(head_of_lane == r, sink_ref[:, h:h + 1], sink)
        sc = jnp.where(valid, st, -jnp.inf)
        m = jnp.maximum(jnp.max(sc, axis=0, keepdims=True), sink)
        e = jnp.exp(sc - m)
        den = jnp.sum(e, axis=0, keepdims=True) + jnp.exp(sink - m)
        vt = jnp.concatenate([vtp_ref[0, 0, dsl, :], vtc_ref[0, 0, dsl, :]], axis=1)
        o_q = (_dot(vt, e.astype(BF16)) * (1.0 / den)).T
        o_ref[0, :, KV_DIM * kv:KV_DIM * (kv + 1)] = jnp.concatenate(
            [o_q[w * r:w * (r + 1)] for r in range(GQA_R)], axis=1).astype(o_ref.dtype)

    _one_ahead(range(KV_HEADS), start, finish, depth=2)


def swa_core(qk, vt, sinks, b, s):
    w = SWA_BLOCK
    qkv3 = qk.reshape(b, s, -1)
    qd = N_HEADS * HEAD_DIM
    kcol = qd // KV_DIM
    vt = vt.reshape(b, s // VT_SLAB, KV_DIM, VT_SLAB)
    prev = lambda i, j: (i, jnp.maximum(j - 1, 0))
    out = pl.pallas_call(
        _swa_kernel,
        grid=(b, s // w),
        in_specs=[pl.BlockSpec((1, w, qd), lambda i, j: (i, j, 0)),
                  pl.BlockSpec((1, w, KV_DIM), lambda i, j: (i, j, kcol)),
                  pl.BlockSpec((1, w, KV_DIM), lambda i, j: prev(i, j) + (kcol,)),
                  pl.BlockSpec((1, 1, KV_DIM, VT_SLAB), lambda i, j: (i, j, 0, 0)),
                  pl.BlockSpec((1, 1, KV_DIM, VT_SLAB), lambda i, j: prev(i, j) + (0, 0)),
                  pl.BlockSpec((1, LANES), lambda i, j: (0, 0))],
        out_specs=pl.BlockSpec((1, w, qd), lambda i, j: (i, j, 0)),
        out_shape=jax.ShapeDtypeStruct((b, s, qd), BF16),
        compiler_params=_cparams("parallel", "arbitrary"),
        name="swa_core",
    )(qkv3, qkv3, qkv3, vt, vt, jnp.pad(sinks.astype(F32), (0, LANES - N_HEADS)).reshape(1, LANES))
    return out.reshape(b * s, qd)


def _compress_kernel(f_ref, pe_ref, w1_ref, w2_ref, o_ref, *, transposed):
    half = NSA_CMP_STRIDE * HEAD_DIM
    x = f_ref[0, 0]
    za = (x + pe_ref[:, 0:half]).astype(BF16)
    zb = (x + pe_ref[:, half:2 * half]).astype(BF16)
    hb = _dot(zb, w1_ref[half:2 * half, :])
    hid = _dot(za, w1_ref[0:half, :]) + pltpu.roll(hb, hb.shape[0] - 1, 0)
    act = (hid * _sigmoid(hid)).astype(BF16)
    if transposed:
        o_ref[0, 0] = _dot_nt(w2_ref[...], act).astype(o_ref.dtype)
    else:
        o_ref[0, 0] = _dot(act, w2_ref[...]).astype(o_ref.dtype)


def compress(tok, pe, w1, w2, b, s, transposed):
    nch = s // NSA_CMP_STRIDE
    half = NSA_CMP_STRIDE * HEAD_DIM
    f = tok.reshape(b, nch, NSA_CMP_STRIDE, KV_HEADS, HEAD_DIM).transpose(0, 3, 1, 2, 4).reshape(b, KV_HEADS, nch, half)
    w2 = (w2.T if transposed else w2).astype(BF16)
    oshape = (HEAD_DIM, nch) if transposed else (nch, HEAD_DIM)
    return pl.pallas_call(
        functools.partial(_compress_kernel, transposed=transposed),
        grid=(b, KV_HEADS),
        in_specs=[pl.BlockSpec((1, 1, nch, half), lambda i, j: (i, j, 0, 0)),
                  pl.BlockSpec((1, 2 * half), lambda i, j: (0, 0)),
                  pl.BlockSpec((2 * half, NSA_CMP_HIDDEN), lambda i, j: (0, 0)),
                  pl.BlockSpec(w2.shape, lambda i, j: (0, 0))],
        out_specs=pl.BlockSpec((1, 1) + oshape, lambda i, j: (i, j, 0, 0)),
        out_shape=jax.ShapeDtypeStruct((b, KV_HEADS) + oshape, BF16),
        compiler_params=_cparams("parallel", "parallel"),
        name="nsa_compress",
    )(f, pe.reshape(1, 2 * half).astype(F32), w1.astype(BF16), w2)


def _dot_split3(a_bf16, x):
    hi = x.astype(BF16)
    r1 = x - hi.astype(F32)
    mid = r1.astype(BF16)
    lo = (r1 - mid.astype(F32)).astype(BF16)
    return _dot(a_bf16, hi) + _dot(a_bf16, mid) + _dot(a_bf16, lo)


def _nsa_kernel(q_ref, ks_ref, vst_ref, kw_ref, vwt_ref, kcmp_ref, vcmpt_ref, g_ref, ov_ref,
                o_ref, qt_ref, bias_ref, ocmp_ref, owin_ref, acc_ref, m_ref, l_ref, st_ref, *, seq):
    i = pl.program_id(1)
    qb = NSA_Q_BLOCK
    rq = GQA_R * qb
    n_cmp = seq // NSA_CMP_STRIDE
    n_sel = seq // NSA_SEL_BLOCK
    blocks_per_chunk = NSA_SEL_CHUNK // NSA_SEL_BLOCK
    slabs_per_chunk = NSA_SEL_CHUNK // VT_SLAB
    t0 = i * qb
    qpos = t0 + (lax.broadcasted_iota(jnp.int32, (1, rq), 1) & (qb - 1))
    gates_t = _sigmoid(g_ref[0]).T
    q_t = q_ref[0].astype(F32).T * ATTN_SCALE

    cmp_end = lax.broadcasted_iota(jnp.int32, (n_cmp, rq), 0) * NSA_CMP_STRIDE + (NSA_CMP_LEN - 1)
    mask_c = cmp_end <= qpos
    has_cmp = jnp.where(qpos >= NSA_CMP_LEN - 1, 1.0, 0.0)
    blk = lax.broadcasted_iota(jnp.int32, (n_sel, LANES), 0)
    blk_f = blk.astype(F32)
    forced = (blk == i) | (blk == 0)
    allowed = blk <= i
    lane_lo = lax.broadcasted_iota(jnp.int32, (n_sel, LANES), 1) < qb
    importance = {}
    cw = jnp.maximum(t0 - NSA_WINDOW, 0) // VT_SLAB
    wkeys = NSA_WIN_SLABS * VT_SLAB
    wstart = pl.multiple_of(cw * VT_SLAB, VT_SLAB)
    delta_w = qpos - (wstart + lax.broadcasted_iota(jnp.int32, (wkeys, rq), 0))
    mask_w = (delta_w >= 0) & (delta_w < NSA_WINDOW)
    last_chunk = i // blocks_per_chunk
    key_in_blk = lax.broadcasted_iota(jnp.int32, (NSA_SEL_BLOCK, rq), 0)
    diag_bias = jnp.where(key_in_blk <= qpos - t0, 0.0, MASK_NEG)
    groups = range(KV_HEADS)

    def cmp_start(kv):
        qt = jnp.concatenate(
            [q_t[HEAD_DIM * (GQA_R * kv + r):HEAD_DIM * (GQA_R * kv + r + 1), :] for r in range(GQA_R)],
            axis=1).astype(BF16)
        qt_ref[kv] = qt
        return _dot(kcmp_ref[0, kv], qt)

    def cmp_finish(kv, logits):
        lc = jnp.where(mask_c, logits, MASK_NEG)
        ec = jnp.exp(lc - jnp.max(lc, axis=0, keepdims=True))
        inv = has_cmp / jnp.sum(ec, axis=0, keepdims=True)
        ocmp_ref[kv] = _dot(vcmpt_ref[0, kv], ec.astype(BF16)) * inv
        imp = _dot_split3(ov_ref[...], ec) * inv
        imp = imp + pltpu.roll(imp, qb, 1)
        imp = imp + pltpu.roll(imp, 2 * qb, 1)
        m_ref[kv] = jnp.full((1, rq), MASK_NEG, F32)
        l_ref[kv] = jnp.zeros((1, rq), F32)
        acc_ref[kv] = jnp.zeros((HEAD_DIM, rq), F32)
        importance[kv] = imp[:, 0:LANES]
        if kv % 2 == 1:
            select_pair(kv - 1, kv)

    def select_pair(ka, kb):
        v = jnp.where(lane_lo, importance[ka], importance[kb])
        v = jnp.where(allowed & jnp.logical_not(forced), v, -jnp.inf)
        sel = forced
        for _ in range(NSA_TOP_N - 2):
            best = jnp.max(v, axis=0, keepdims=True)
            first = jnp.min(jnp.where(v == best, blk_f, float(n_sel)), axis=0, keepdims=True)
            pick = blk_f == first
            sel = sel | pick
            v = jnp.where(pick, -jnp.inf, v)
        bias = jnp.where(sel & allowed, 0.0, MASK_NEG)
        swapped = pltpu.roll(bias, qb, 1)
        for kv, own_lo in ((ka, True), (kb, False)):
            half = jnp.where(lane_lo, bias, swapped) if own_lo else jnp.where(lane_lo, swapped, bias)
            bias_ref[kv] = jnp.concatenate([half, half], axis=1)

    def cmp_phase_start(item):
        if not isinstance(item, tuple):
            return cmp_start(item)
        kind, kv = item
        if kind == "window":
            return _dot(kw_ref[0, pl.ds(wstart, wkeys), HEAD_DIM * kv:HEAD_DIM * (kv + 1)], qt_ref[kv])
        return sel_start(0, kv)

    def cmp_phase_finish(item, res):
        if not isinstance(item, tuple):
            cmp_finish(item, res)
            return
        kind, kv = item
        if kind == "first_chunk":
            st_ref[kv] = res
            return
        dsl = slice(HEAD_DIM * kv, HEAD_DIM * (kv + 1))
        lw = jnp.where(mask_w, res, MASK_NEG)
        ew = jnp.exp(lw - jnp.max(lw, axis=0, keepdims=True))
        vwt = jnp.concatenate([vwt_ref[0, cw + u, dsl, :] for u in range(NSA_WIN_SLABS)], axis=1)
        owin_ref[kv] = _dot(vwt, ew.astype(BF16)) * (1.0 / jnp.sum(ew, axis=0, keepdims=True))

    def sel_start(c, kv):
        k0 = pl.multiple_of(c * NSA_SEL_CHUNK, NSA_SEL_CHUNK)
        return _dot(ks_ref[0, pl.ds(k0, NSA_SEL_CHUNK), HEAD_DIM * kv:HEAD_DIM * (kv + 1)], qt_ref[kv])

    def sel_finish(c, kv, st, causal):
        dsl = slice(HEAD_DIM * kv, HEAD_DIM * (kv + 1))
        k0 = c * NSA_SEL_CHUNK
        bias8 = bias_ref[kv, pl.ds(pl.multiple_of(c * blocks_per_chunk, blocks_per_chunk), blocks_per_chunk), :]
        rows = []
        for jb in range(blocks_per_chunk):
            sj = st[NSA_SEL_BLOCK * jb:NSA_SEL_BLOCK * (jb + 1)] + bias8[jb:jb + 1, :]
            if causal:
                sj = sj + jnp.where(jb == i % blocks_per_chunk, diag_bias, 0.0)
            rows.append(sj)
        sc = jnp.concatenate(rows, axis=0)
        m_run = m_ref[kv]
        m_new = jnp.maximum(m_run, jnp.max(sc, axis=0, keepdims=True))
        p = jnp.exp(sc - m_new)
        alpha = jnp.exp(m_run - m_new)
        vt = jnp.concatenate([vst_ref[0, slabs_per_chunk * c + u, dsl, :] for u in range(slabs_per_chunk)],
                             axis=1)
        m_ref[kv] = m_new
        l_ref[kv] = alpha * l_ref[kv] + jnp.sum(p, axis=0, keepdims=True)
        acc_ref[kv] = alpha * acc_ref[kv] + _dot(vt, p.astype(BF16))

    ahead = NSA_AHEAD
    _one_ahead(list(groups) + [("window", kv) for kv in groups] + [("first_chunk", kv) for kv in range(ahead)],
               cmp_phase_start, cmp_phase_finish, depth=ahead)

    def sel_trip(c, carry):
        pending = [st_ref[k] for k in range(ahead)]
        for kv in groups:
            nk = kv + ahead
            if nk < KV_HEADS:
                pending.append(sel_start(c, nk))
            elif nk == KV_HEADS:
                parked = [sel_start(c + 1, k) for k in range(ahead)]
            sel_finish(c, kv, pending.pop(0), False)
            if nk == KV_HEADS:
                for k in range(ahead):
                    st_ref[k] = parked[k]
        return carry

    lax.fori_loop(0, last_chunk, sel_trip, 0)

    def tail_finish(kv, logits):
        sel_finish(last_chunk, kv, logits, True)
        o_sel = acc_ref[kv] * (1.0 / l_ref[kv])

        def gate_row(branch):
            return jnp.concatenate(
                [gates_t[3 * (GQA_R * kv + r) + branch:3 * (GQA_R * kv + r) + branch + 1, :] for r in range(GQA_R)],
                axis=1)
        o_t = gate_row(0) * ocmp_ref[kv] + gate_row(1) * o_sel + gate_row(2) * owin_ref[kv]
        o_q = o_t.T
        o_ref[0, :, KV_DIM * kv:KV_DIM * (kv + 1)] = jnp.concatenate(
            [o_q[qb * r:qb * (r + 1)] for r in range(GQA_R)], axis=1).astype(o_ref.dtype)

    _one_ahead(groups, functools.partial(sel_start, last_chunk), tail_finish,
               depth=ahead, primed=[st_ref[k] for k in range(ahead)])


def nsa_core(pa, vt, pb, k_cmp, v_cmp_t, b, s):
    qb = NSA_Q_BLOCK
    qd = N_HEADS * HEAD_DIM
    n_cmp = s // NSA_CMP_STRIDE
    n_sel = s // NSA_SEL_BLOCK
    n_slab = s // VT_SLAB
    pa3 = pa.reshape(b, s, -1)
    pb3 = pb.reshape(b, s, -1)
    vt4 = vt.reshape(b, n_slab, 2 * KV_DIM, VT_SLAB)
    sel_lo = np.arange(n_sel)[:, None] * NSA_SEL_BLOCK
    cmp_lo = np.arange(n_cmp)[None, :] * NSA_CMP_STRIDE
    ov = np.clip(np.minimum(sel_lo + NSA_SEL_BLOCK, cmp_lo + NSA_CMP_LEN) - np.maximum(sel_lo, cmp_lo), 0, None)
    ov = jnp.asarray((ov / NSA_CMP_LEN).astype(np.float32), BF16)
    kcol = qd // KV_DIM
    seqspec = lambda col: pl.BlockSpec((1, s, KV_DIM), lambda i, j: (i, 0, col))
    slabspec = lambda k: pl.BlockSpec((1, n_slab, KV_DIM, VT_SLAB), lambda i, j: (i, 0, k, 0))
    rq = GQA_R * qb
    out = pl.pallas_call(
        functools.partial(_nsa_kernel, seq=s),
        grid=(b, s // qb),
        in_specs=[pl.BlockSpec((1, qb, qd), lambda i, j: (i, j, 0)),
                  seqspec(kcol), slabspec(0), seqspec(kcol + 1), slabspec(1),
                  pl.BlockSpec((1, KV_HEADS, n_cmp, HEAD_DIM), lambda i, j: (i, 0, 0, 0)),
                  pl.BlockSpec((1, KV_HEADS, HEAD_DIM, n_cmp), lambda i, j: (i, 0, 0, 0)),
                  pl.BlockSpec((1, qb, KV_DIM), lambda i, j: (i, j, 2)),
                  pl.BlockSpec((n_sel, n_cmp), lambda i, j: (0, 0))],
        out_specs=pl.BlockSpec((1, qb, qd), lambda i, j: (i, j, 0)),
        out_shape=jax.ShapeDtypeStruct((b, s, qd), BF16),
        scratch_shapes=[pltpu.VMEM((KV_HEADS, HEAD_DIM, rq), BF16),
                        pltpu.VMEM((KV_HEADS, n_sel, rq), F32),
                        pltpu.VMEM((KV_HEADS, HEAD_DIM, rq), F32),
                        pltpu.VMEM((KV_HEADS, HEAD_DIM, rq), F32),
                        pltpu.VMEM((KV_HEADS, HEAD_DIM, rq), F32),
                        pltpu.VMEM((KV_HEADS, 1, rq), F32),
                        pltpu.VMEM((KV_HEADS, 1, rq), F32),
                        pltpu.VMEM((NSA_AHEAD, NSA_SEL_CHUNK, rq), F32)],
        compiler_params=_cparams("parallel", "arbitrary"),
        name="nsa_core",
    )(pa3, pa3, vt4, pa3, vt4, k_cmp, v_cmp_t, pb3, ov)
    return out.reshape(b * s, qd)


def _rope_tables(positions):
    half = ROPE_DIM // 2
    inv_freq = ROPE_THETA ** (-jnp.arange(0, ROPE_DIM, 2, dtype=F32) / ROPE_DIM)
    ang = positions.astype(F32).reshape(-1)[:, None] * inv_freq
    cos, sin = jnp.cos(ang), jnp.sin(ang)
    t = cos.shape[0]
    ones = jnp.ones((t, HEAD_DIM - ROPE_DIM), F32)
    zeros = jnp.zeros((t, HEAD_DIM - ROPE_DIM), F32)
    zh = jnp.zeros((t, half), F32)
    c = jnp.concatenate([cos, cos, ones], axis=1)
    sa = jnp.concatenate([-sin, zh, zeros], axis=1)
    sb = jnp.concatenate([zh, sin, zeros], axis=1)
    return tuple(jnp.tile(v, (1, LANES // HEAD_DIM)) for v in (c, sa, sb))


def _pad_cols(w, n):
    return jnp.pad(w, ((0, 0), (0, n - w.shape[1])))


def mamba2_mixer(h, ln_w, w_in, conv_w, conv_b, dt_bias, a_log, d_skip, norm_w, w_out, b, s):
    w = _pad_cols(w_in, SSM_IN_PAD).astype(BF16)
    zx = norm_proj(h, ln_w, w, jnp.zeros((SSM_IN_PAD,), F32), F32, tm=SSM_IN_TOKEN_TILE, row_splits=2)
    y = ssd_core(zx, conv_w, conv_b, dt_bias, a_log, d_skip, norm_w, b, s)
    return out_proj(y, w_out.astype(BF16), jnp.zeros((D_MODEL,), F32), h)


def swa_mixer(h, ln_w, w_qkv, b_qkv, sinks, w_o, b_o, rope, b, s):
    qk = (N_HEADS + KV_HEADS) * HEAD_DIM
    w = w_qkv.astype(BF16)
    bias = b_qkv.astype(F32)
    q_k, vt = norm_proj(h, ln_w, w[:, :qk], bias[:qk], BF16, rope=rope, rope_cols=((0, qk),),
                        w_t=w[:, qk:].T, bias_t=bias[qk:])
    o = swa_core(q_k, vt, sinks, b, s)
    return out_proj(o, w_o.astype(BF16), b_o.astype(F32), h)


def nsa_mixer(h, ln_w, w_in, pe_k, k_w1, k_w2, pe_v, v_w1, v_w2, w_o, rope, b, s):
    qd = N_HEADS * HEAD_DIM
    cols = lambda k: w_in[:, qd + KV_DIM * k:qd + KV_DIM * (k + 1)]
    wa = jnp.concatenate([w_in[:, :qd], cols(2), cols(4)], axis=1).astype(BF16)
    wv_t = jnp.concatenate([cols(3), cols(5)], axis=1).astype(BF16).T
    wb = _pad_cols(jnp.concatenate([cols(0), cols(1), w_in[:, qd + 6 * KV_DIM:]], axis=1), 3 * KV_DIM).astype(BF16)
    pa, vt = norm_proj(h, ln_w, wa, jnp.zeros((wa.shape[1],), F32), BF16, rope=rope,
                       rope_cols=((0, wa.shape[1]),), w_t=wv_t, bias_t=jnp.zeros((wv_t.shape[0],), F32))
    pb = norm_proj(h, ln_w, wb, jnp.zeros((wb.shape[1],), F32), F32, rope=rope, rope_cols=((0, KV_DIM),))
    pb3 = pb.reshape(b, s, -1)
    k_cmp = compress(pb3[..., 0:KV_DIM], pe_k, k_w1, k_w2, b, s, False)
    v_cmp_t = compress(pb3[..., KV_DIM:2 * KV_DIM], pe_v, v_w1, v_w2, b, s, True)
    o = nsa_core(pa, vt, pb, k_cmp, v_cmp_t, b, s)
    return out_proj(o, w_o.astype(BF16), jnp.zeros((D_MODEL,), F32), h)


def kernel(x, positions, ln_ffn1, ffn1_w_in, ffn1_w_out, ln_mix, ln_ffn2, ffn2_w_in, ffn2_w_out, ssm_w_in, ssm_conv_w, ssm_conv_b, ssm_dt_bias, ssm_a_log, ssm_d, ssm_norm_w, ssm_w_out, swa_w_qkv, swa_b_qkv, swa_sinks, swa_w_o, swa_b_o, nsa_w_in, nsa_pe_k, nsa_k_w1, nsa_k_w2, nsa_pe_v, nsa_v_w1, nsa_v_w2, nsa_w_o, final_norm):
    b, s, d = x.shape
    depth = ln_ffn1.shape[0]
    rope = _rope_tables(positions)
    h = x.reshape(b * s, d)
    for i in range(depth):
        kind, inst = i % 3, i // 3
        h = ffn(h, ln_ffn1[i], ffn1_w_in[i], ffn1_w_out[i])
        if kind == 0:
            h = mamba2_mixer(h, ln_mix[i], ssm_w_in[inst], ssm_conv_w[inst], ssm_conv_b[inst],
                             ssm_dt_bias[inst], ssm_a_log[inst], ssm_d[inst], ssm_norm_w[inst],
                             ssm_w_out[inst], b, s)
        elif kind == 1:
            h = swa_mixer(h, ln_mix[i], swa_w_qkv[inst], swa_b_qkv[inst], swa_sinks[inst],
                          swa_w_o[inst], swa_b_o[inst], rope, b, s)
        else:
            h = nsa_mixer(h, ln_mix[i], nsa_w_in[inst], nsa_pe_k[inst], nsa_k_w1[inst], nsa_k_w2[inst],
                          nsa_pe_v[inst], nsa_v_w1[inst], nsa_v_w2[inst], nsa_w_o[inst], rope, b, s)
        h = ffn(h, ln_ffn2[i], ffn2_w_in[i], ffn2_w_out[i], final_norm if i == depth - 1 else None)
    return h.reshape(b, s, d)
```

```python
import functools
import math

import numpy as np
import jax
import jax.numpy as jnp
from jax import lax
from jax.experimental import pallas as pl
from jax.experimental.pallas import tpu as pltpu

F32 = jnp.float32
BF16 = jnp.bfloat16

D_MODEL = 1024
RMS_EPS = 1e-6
D_FF = 2816
HEAD_DIM = 64
ROPE_DIM = HEAD_DIM // 4
ROPE_THETA = 500000.0
ATTN_SCALE = HEAD_DIM ** -0.5
MASK_NEG = -1e30

SSM_D_INNER = 2 * D_MODEL
SSM_HEADS = 32
SSM_GROUPS = 8
SSM_STATE = 128
SSM_CONV = 4
SSM_CHUNK = 128
SSM_CONV_DIM = SSM_D_INNER + 2 * SSM_GROUPS * SSM_STATE
SSM_IN_PAD = SSM_D_INNER + SSM_CONV_DIM + 256
SSM_IN_TOKEN_TILE = 512

N_HEADS = 16
KV_HEADS = 4
GQA_R = N_HEADS // KV_HEADS
KV_DIM = KV_HEADS * HEAD_DIM

SWA_BLOCK = 128

NSA_CMP_LEN = 32
NSA_CMP_STRIDE = 16
NSA_CMP_HIDDEN = 256
NSA_SEL_BLOCK = 64
NSA_TOP_N = 8
NSA_WINDOW = 512
NSA_Q_BLOCK = 64
NSA_SEL_CHUNK = 512
NSA_AHEAD = 2
VT_SLAB = 128
NSA_WIN_SLABS = NSA_WINDOW // VT_SLAB + 1

LANES = 128
VMEM_LIMIT = 56 * 1024 * 1024
TOKEN_TILE = 1024
ROW_SPLITS = 4
FFN_TOKEN_TILE = 1024
FFN_ROW_SPLITS = 4


def _cparams(*sem):
    return pltpu.CompilerParams(dimension_semantics=sem, vmem_limit_bytes=VMEM_LIMIT)


def _dot(a, b):
    return jnp.dot(a, b, preferred_element_type=F32)


def _dot_nt(a, b):
    return lax.dot_general(a, b, (((1,), (1,)), ((), ())), preferred_element_type=F32)


def _sigmoid(x):
    return 1.0 / (1.0 + jnp.exp(-x))


def _rms(x, w):
    return x * lax.rsqrt(jnp.mean(x * x, axis=-1, keepdims=True) + RMS_EPS) * w


def _one_ahead(items, start, finish, depth=1, primed=()):
    items = list(items)
    pending = list(primed)
    nxt = len(pending)
    while nxt < min(depth, len(items)):
        pending.append(start(items[nxt]))
        nxt += 1
    for item in items:
        if nxt < len(items):
            pending.append(start(items[nxt]))
            nxt += 1
        finish(item, pending.pop(0))


def _rope_store(o_ref, rows, acc, rope_segs, c, sa, sb):
    for k, roped in enumerate(rope_segs):
        seg = acc[:, LANES * k:LANES * (k + 1)]
        if roped:
            seg = (seg * c + pltpu.roll(seg, LANES - ROPE_DIM // 2, 1) * sa
                   + pltpu.roll(seg, ROPE_DIM // 2, 1) * sb)
        o_ref[rows, LANES * k:LANES * (k + 1)] = seg.astype(o_ref.dtype)


def _norm_proj_kernel(*refs, rope_segs, col_tiles, with_vt, row_splits):
    refs = list(refs)
    x_ref, lnw_ref, w_ref, b_ref = refs[:4]
    del refs[:4]
    if rope_segs is not None:
        c_ref, sa_ref, sb_ref = refs[:3]
        del refs[:3]
    if with_vt:
        wt_ref, bt_ref = refs[:2]
        del refs[:2]
    o_ref = refs.pop(0)
    vt_ref = refs.pop(0) if with_vt else None
    xn_ref = refs.pop(0) if col_tiles > 1 else None
    tm = x_ref.shape[0]
    rt = tm // row_splits
    row_tiles = [slice(rt * h, rt * (h + 1)) for h in range(row_splits)]

    if col_tiles > 1:
        @pl.when(pl.program_id(1) == 0)
        def _():
            xn_ref[...] = _rms(x_ref[...], lnw_ref[...]).astype(BF16)

    def start(rows):
        xn = xn_ref[rows, :] if col_tiles > 1 else _rms(x_ref[rows, :], lnw_ref[...]).astype(BF16)
        acc = _dot(xn, w_ref[pl.program_id(1) if col_tiles > 1 else 0]) + b_ref[...]
        return (acc, _dot_nt(wt_ref[...], xn) + bt_ref[...]) if with_vt else (acc, None)

    def finish(rows, res):
        acc, vt = res
        if rope_segs is None:
            o_ref[rows, :] = acc.astype(o_ref.dtype)
        else:
            _rope_store(o_ref, rows, acc, rope_segs, c_ref[rows, :], sa_ref[rows, :], sb_ref[rows, :])
        if with_vt:
            for u in range(rt // VT_SLAB):
                vt_ref[rows.start // VT_SLAB + u] = vt[:, VT_SLAB * u:VT_SLAB * (u + 1)].astype(vt_ref.dtype)

    _one_ahead(row_tiles, start, finish)


def norm_proj(h, ln_w, w, bias, out_dtype, tn=None, rope=None, rope_cols=(), w_t=None, bias_t=None,
              tm=TOKEN_TILE, row_splits=ROW_SPLITS):
    t, d = h.shape
    n = w.shape[1]
    tm = min(tm, t)
    tn = n if tn is None else tn
    col_tiles = n // tn
    assert n == col_tiles * tn and t % tm == 0 and tn % LANES == 0
    with_vt = w_t is not None
    rope_segs = None
    if rope_cols:
        assert col_tiles == 1
        rope_segs = tuple(any(lo <= LANES * k < hi for lo, hi in rope_cols) for k in range(n // LANES))
    w_tiles = w.reshape(d, col_tiles, tn).transpose(1, 0, 2)
    in_specs = [pl.BlockSpec((tm, d), lambda i, j: (i, 0)),
                pl.BlockSpec((1, d), lambda i, j: (0, 0)),
                pl.BlockSpec((col_tiles, d, tn), lambda i, j: (0, 0, 0), pipeline_mode=pl.Buffered(1)),
                pl.BlockSpec((1, tn), lambda i, j: (0, j))]
    args = [h, ln_w.reshape(1, d), w_tiles, bias.reshape(1, n)]
    if rope_segs is not None:
        in_specs += [pl.BlockSpec((tm, LANES), lambda i, j: (i, 0))] * 3
        args += list(rope)
    out_specs = [pl.BlockSpec((tm, tn), lambda i, j: (i, j))]
    out_shape = [jax.ShapeDtypeStruct((t, n), out_dtype)]
    if with_vt:
        assert col_tiles == 1
        nv = w_t.shape[0]
        in_specs += [pl.BlockSpec((nv, d), lambda i, j: (0, 0)), pl.BlockSpec((nv, 1), lambda i, j: (0, 0))]
        args += [w_t, bias_t.reshape(nv, 1)]
        out_specs.append(pl.BlockSpec((tm // VT_SLAB, nv, VT_SLAB), lambda i, j: (i, 0, 0)))
        out_shape.append(jax.ShapeDtypeStruct((t // VT_SLAB, nv, VT_SLAB), BF16))
    res = pl.pallas_call(
        functools.partial(_norm_proj_kernel, rope_segs=rope_segs, col_tiles=col_tiles, with_vt=with_vt,
                          row_splits=row_splits),
        grid=(t // tm, col_tiles),
        in_specs=in_specs,
        out_specs=out_specs,
        out_shape=out_shape,
        scratch_shapes=[pltpu.VMEM((tm, d), BF16)] if col_tiles > 1 else [],
        compiler_params=_cparams("parallel", "arbitrary"),
        name="norm_proj",
    )(*args)
    return tuple(res) if with_vt else res[0]


def _out_proj_kernel(y_ref, w_ref, b_ref, r_ref, o_ref):
    o_ref[...] = r_ref[...] + _dot(y_ref[...], w_ref[...]) + b_ref[...]


def out_proj(y, w, bias, res):
    t, k = y.shape
    d = w.shape[1]
    tm = min(TOKEN_TILE, t)
    return pl.pallas_call(
        _out_proj_kernel,
        grid=(t // tm,),
        in_specs=[pl.BlockSpec((tm, k), lambda i: (i, 0)),
                  pl.BlockSpec((k, d), lambda i: (0, 0)),
                  pl.BlockSpec((1, d), lambda i: (0, 0)),
                  pl.BlockSpec((tm, d), lambda i: (i, 0))],
        out_specs=pl.BlockSpec((tm, d), lambda i: (i, 0)),
        out_shape=jax.ShapeDtypeStruct((t, d), F32),
        compiler_params=_cparams("parallel"),
        name="out_proj",
    )(y, w, bias.reshape(1, d), res)


def _ffn_kernel(x_ref, lnw_ref, wi_ref, wo_ref, fnw_ref, o_ref, *, final_norm):
    tm = x_ref.shape[0]
    rt = tm // FFN_ROW_SPLITS
    dff = wo_ref.shape[0]

    def start(rows):
        xn = _rms(x_ref[rows, :], lnw_ref[...]).astype(BF16)
        return _dot(xn, wi_ref[:, 0:dff]), _dot(xn, wi_ref[:, dff:2 * dff])

    def finish(rows, gu):
        g, u = gu
        y = x_ref[rows, :] + 0.5 * _dot((g * _sigmoid(g) * u).astype(BF16), wo_ref[...])
        if final_norm:
            y = _rms(y, fnw_ref[...])
        o_ref[rows, :] = y

    _one_ahead([slice(rt * h, rt * (h + 1)) for h in range(FFN_ROW_SPLITS)], start, finish)


def ffn(h, ln_w, w_in, w_out, final_w=None):
    t, d = h.shape
    tm = min(FFN_TOKEN_TILE, t)
    final_norm = final_w is not None
    fnw = (final_w if final_norm else ln_w).reshape(1, d)
    resident = lambda shape: pl.BlockSpec(shape, lambda i: (0, 0), pipeline_mode=pl.Buffered(1))
    return pl.pallas_call(
        functools.partial(_ffn_kernel, final_norm=final_norm),
        grid=(t // tm,),
        in_specs=[pl.BlockSpec((tm, d), lambda i: (i, 0)),
                  pl.BlockSpec((1, d), lambda i: (0, 0)),
                  resident(w_in.shape),
                  resident(w_out.shape),
                  pl.BlockSpec((1, d), lambda i: (0, 0))],
        out_specs=pl.BlockSpec((tm, d), lambda i: (i, 0)),
        out_shape=jax.ShapeDtypeStruct((t, d), F32),
        compiler_params=_cparams("parallel"),
        name="ffn",
    )(h, ln_w.reshape(1, d), w_in.astype(BF16), w_out.astype(BF16), fnw)


def _expand_heads(mat, g):
    rows = mat.shape[0]
    lane = lax.broadcasted_iota(jnp.int32, (rows, LANES), 1)
    pieces = []
    for p in range(2):
        h0 = GQA_R * g + 2 * p
        a = jnp.broadcast_to(mat[:, h0:h0 + 1], (rows, LANES))
        b = jnp.broadcast_to(mat[:, h0 + 1:h0 + 2], (rows, LANES))
        pieces.append(jnp.where(lane < HEAD_DIM, a, b))
    return jnp.concatenate(pieces, axis=1)


def _ssd_kernel(z_ref, x_ref, bc_ref, dt_ref, cw_ref, cb_ref, dtb_ref, alog_ref, dsk_ref, nw_ref,
                tril_ref, o_ref, state_ref, xbuf_ref):
    q = SSM_CHUNK
    gw = GQA_R * HEAD_DIM
    c = pl.program_id(1)

    @pl.when(c == 0)
    def _():
        state_ref[...] = jnp.zeros(state_ref.shape, F32)
        xbuf_ref[0:8, :] = jnp.zeros((8, SSM_CONV_DIM), F32)

    xbuf_ref[8:8 + q, 0:SSM_D_INNER] = x_ref[0]
    xbuf_ref[8:8 + q, SSM_D_INNER:SSM_CONV_DIM] = bc_ref[0]

    def conv_silu(c0, width):
        acc = cb_ref[:, c0:c0 + width]
        for k in reversed(range(SSM_CONV)):
            r0 = 8 - (SSM_CONV - 1) + k
            acc = acc + cw_ref[k:k + 1, c0:c0 + width] * xbuf_ref[r0:r0 + q, c0:c0 + width]
        return acc * _sigmoid(acc)

    dtr = dt_ref[0] + dtb_ref[...]
    e_neg = jnp.exp(-jnp.abs(dtr))
    u = 1.0 + e_neg
    log_u = jnp.log2(u) * math.log(2.0)
    dt = jnp.maximum(dtr, 0.0) + jnp.where(u == 1.0, e_neg, log_u * (e_neg / (u - 1.0)))
    ad = dt * (-jnp.exp(alog_ref[...]))
    acum = jnp.dot(tril_ref[...], ad, precision=lax.Precision.HIGHEST,
                   preferred_element_type=F32)
    acum_row = acum.T
    causal = (lax.broadcasted_iota(jnp.int32, (q, q), 0) >= lax.broadcasted_iota(jnp.int32, (q, q), 1))
    head_of_lane = lax.broadcasted_iota(jnp.int32, (q, 2 * LANES), 1) // HEAD_DIM

    def start(g):
        xs = conv_silu(gw * g, gw)
        bm = conv_silu(SSM_D_INNER + SSM_STATE * g, SSM_STATE)
        cm = conv_silu(SSM_D_INNER + SSM_GROUPS * SSM_STATE + SSM_STATE * g, SSM_STATE)
        xd = xs * _expand_heads(dt, g)
        cbf = cm.astype(BF16)
        st = state_ref[g]
        return xs, bm, xd, _dot_nt(cbf, bm.astype(BF16)), _dot(cbf, st.astype(BF16))

    def finish(g, vals):
        xs, bm, xd, cb, y_off = vals
        st = state_ref[g]
        cs = slice(gw * g, gw * (g + 1))
        acum_ch = _expand_heads(acum, g)
        a_last = acum_ch[q - 1:q, :]
        xdb = xd.astype(BF16)
        y_diag = None
        for r in range(GQA_R):
            hh = GQA_R * g + r
            diff = acum[:, hh:hh + 1] - acum_row[hh:hh + 1, :]
            decay = jnp.exp(jnp.where(causal, diff, -jnp.inf))
            y_r = _dot((cb * decay).astype(BF16), xdb)
            y_diag = y_r if r == 0 else jnp.where(head_of_lane == r, y_r, y_diag)
        state_ref[g] = (st * jnp.exp(a_last)
                        + _dot(bm.T.astype(BF16), (xd * jnp.exp(a_last - acum_ch)).astype(BF16)))
        y = y_off * jnp.exp(acum_ch) + y_diag + dsk_ref[:, cs] * xs
        zz = z_ref[0, :, cs]
        gated = y * (zz * _sigmoid(zz))
        o_ref[0, :, cs] = _rms(gated, nw_ref[:, cs]).astype(o_ref.dtype)

    _one_ahead(range(SSM_GROUPS), start, finish)
    xbuf_ref[0:8, :] = xbuf_ref[q:q + 8, :]


def ssd_core(zx, conv_w, conv_b, dt_bias, a_log, d_skip, norm_w, b, s):
    q = SSM_CHUNK
    zx3 = zx.reshape(b, s, SSM_IN_PAD)
    pad = lambda v: jnp.pad(v.astype(F32), (0, LANES - SSM_HEADS)).reshape(1, LANES)
    tril = jnp.asarray(np.tril(np.ones((q, q), np.float32)))
    wide = SSM_D_INNER
    full = lambda shape: pl.BlockSpec(shape, lambda i, c: (0,) * len(shape))
    out = pl.pallas_call(
        _ssd_kernel,
        grid=(b, s // q),
        in_specs=[pl.BlockSpec((1, q, wide), lambda i, c: (i, c, 0)),
                  pl.BlockSpec((1, q, wide), lambda i, c: (i, c, 1)),
                  pl.BlockSpec((1, q, wide), lambda i, c: (i, c, 2)),
                  pl.BlockSpec((1, q, LANES), lambda i, c: (i, c, (SSM_D_INNER + SSM_CONV_DIM) // LANES)),
                  full((SSM_CONV, SSM_CONV_DIM)), full((1, SSM_CONV_DIM)),
                  full((1, LANES)), full((1, LANES)), full((1, wide)), full((1, wide)),
                  full((q, q))],
        out_specs=pl.BlockSpec((1, q, wide), lambda i, c: (i, c, 0)),
        out_shape=jax.ShapeDtypeStruct((b, s, wide), BF16),
        scratch_shapes=[pltpu.VMEM((SSM_GROUPS, SSM_STATE, GQA_R * HEAD_DIM), F32),
                        pltpu.VMEM((q + 8, SSM_CONV_DIM), F32)],
        compiler_params=_cparams("parallel", "arbitrary"),
        name="ssd_core",
    )(zx3, zx3, zx3, zx3, conv_w.astype(F32), conv_b.reshape(1, -1).astype(F32),
      pad(dt_bias), pad(a_log), jnp.repeat(d_skip.astype(F32), HEAD_DIM).reshape(1, wide),
      norm_w.reshape(1, wide).astype(F32), tril)
    return out.reshape(b * s, wide)


def _swa_kernel(q_ref, kc_ref, kp_ref, vtc_ref, vtp_ref, sink_ref, o_ref):
    i = pl.program_id(1)
    w = SWA_BLOCK
    rq = GQA_R * w
    lane = lax.broadcasted_iota(jnp.int32, (1, rq), 1)
    qloc = lane & (w - 1)
    head_of_lane = lane // w
    kk = lax.broadcasted_iota(jnp.int32, (2 * w, rq), 0)
    valid = ((kk < w) & (kk > qloc) & (i > 0)) | ((kk >= w) & ((kk - w) <= qloc))
    scale = jnp.asarray(ATTN_SCALE, BF16)

    def start(kv):
        dsl = slice(HEAD_DIM * kv, HEAD_DIM * (kv + 1))
        qs = jnp.concatenate(
            [q_ref[0, :, HEAD_DIM * (GQA_R * kv + r):HEAD_DIM * (GQA_R * kv + r + 1)] for r in range(GQA_R)],
            axis=0) * scale
        k = jnp.concatenate([kp_ref[0, :, dsl], kc_ref[0, :, dsl]], axis=0)
        return _dot_nt(k, qs)

    def finish(kv, st):
        dsl = slice(HEAD_DIM * kv, HEAD_DIM * (kv + 1))
        sink = jnp.zeros((1, rq), F32)
        for r in range(GQA_R):
            h = GQA_R * kv + r
            sink = jnp.where(head_of_lane == r, sink_ref[:, h:h + 1], sink)
        sc = jnp.where(valid, st, -jnp.inf)
        m = jnp.maximum(jnp.max(sc, axis=0, keepdims=True), sink)
        e = jnp.exp(sc - m)
        den = jnp.sum(e, axis=0, keepdims=True) + jnp.exp(sink - m)
        vt = jnp.concatenate([vtp_ref[0, 0, dsl, :], vtc_ref[0, 0, dsl, :]], axis=1)
        o_q = (_dot(vt, e.astype(BF16)) * (1.0 / den)).T
        o_ref[0, :, KV_DIM * kv:KV_DIM * (kv + 1)] = jnp.concatenate(
            [o_q[w * r:w * (r + 1)] for r in range(GQA_R)], axis=1).astype(o_ref.dtype)

    _one_ahead(range(KV_HEADS), start, finish, depth=2)


def swa_core(qk, vt, sinks, b, s):
    w = SWA_BLOCK
    qkv3 = qk.reshape(b, s, -1)
    qd = N_HEADS * HEAD_DIM
    kcol = qd // KV_DIM
    vt = vt.reshape(b, s // VT_SLAB, KV_DIM, VT_SLAB)
    prev = lambda i, j: (i, jnp.maximum(j - 1, 0))
    out = pl.pallas_call(
        _swa_kernel,
        grid=(b, s // w),
        in_specs=[pl.BlockSpec((1, w, qd), lambda i, j: (i, j, 0)),
                  pl.BlockSpec((1, w, KV_DIM), lambda i, j: (i, j, kcol)),
                  pl.BlockSpec((1, w, KV_DIM), lambda i, j: prev(i, j) + (kcol,)),
                  pl.BlockSpec((1, 1, KV_DIM, VT_SLAB), lambda i, j: (i, j, 0, 0)),
                  pl.BlockSpec((1, 1, KV_DIM, VT_SLAB), lambda i, j: prev(i, j) + (0, 0)),
                  pl.BlockSpec((1, LANES), lambda i, j: (0, 0))],
        out_specs=pl.BlockSpec((1, w, qd), lambda i, j: (i, j, 0)),
        out_shape=jax.ShapeDtypeStruct((b, s, qd), BF16),
        compiler_params=_cparams("parallel", "arbitrary"),
        name="swa_core",
    )(qkv3, qkv3, qkv3, vt, vt, jnp.pad(sinks.astype(F32), (0, LANES - N_HEADS)).reshape(1, LANES))
    return out.reshape(b * s, qd)


def _compress_kernel(f_ref, pe_ref, w1_ref, w2_ref, o_ref, *, transposed):
    half = NSA_CMP_STRIDE * HEAD_DIM
    x = f_ref[0, 0]
    za = (x + pe_ref[:, 0:half]).astype(BF16)
    zb = (x + pe_ref[:, half:2 * half]).astype(BF16)
    hb = _dot(zb, w1_ref[half:2 * half, :])
    hid = _dot(za, w1_ref[0:half, :]) + pltpu.roll(hb, hb.shape[0] - 1, 0)
    act = (hid * _sigmoid(hid)).astype(BF16)
    if transposed:
        o_ref[0, 0] = _dot_nt(w2_ref[...], act).astype(o_ref.dtype)
    else:
        o_ref[0, 0] = _dot(act, w2_ref[...]).astype(o_ref.dtype)


def compress(tok, pe, w1, w2, b, s, transposed):
    nch = s // NSA_CMP_STRIDE
    half = NSA_CMP_STRIDE * HEAD_DIM
    f = tok.reshape(b, nch, NSA_CMP_STRIDE, KV_HEADS, HEAD_DIM).transpose(0, 3, 1, 2, 4).reshape(b, KV_HEADS, nch, half)
    w2 = (w2.T if transposed else w2).astype(BF16)
    oshape = (HEAD_DIM, nch) if transposed else (nch, HEAD_DIM)
    return pl.pallas_call(
        functools.partial(_compress_kernel, transposed=transposed),
        grid=(b, KV_HEADS),
        in_specs=[pl.BlockSpec((1, 1, nch, half), lambda i, j: (i, j, 0, 0)),
                  pl.BlockSpec((1, 2 * half), lambda i, j: (0, 0)),
                  pl.BlockSpec((2 * half, NSA_CMP_HIDDEN), lambda i, j: (0, 0)),
                  pl.BlockSpec(w2.shape, lambda i, j: (0, 0))],
        out_specs=pl.BlockSpec((1, 1) + oshape, lambda i, j: (i, j, 0, 0)),
        out_shape=jax.ShapeDtypeStruct((b, KV_HEADS) + oshape, BF16),
        compiler_params=_cparams("parallel", "parallel"),
        name="nsa_compress",
    )(f, pe.reshape(1, 2 * half).astype(F32), w1.astype(BF16), w2)


def _dot_split3(a_bf16, x):
    hi = x.astype(BF16)
    r1 = x - hi.astype(F32)
    mid = r1.astype(BF16)
    lo = (r1 - mid.astype(F32)).astype(BF16)
    return _dot(a_bf16, hi) + _dot(a_bf16, mid) + _dot(a_bf16, lo)


def _nsa_kernel(q_ref, ks_ref, vst_ref, kw_ref, vwt_ref, kcmp_ref, vcmpt_ref, g_ref, ov_ref,
                o_ref, qt_ref, bias_ref, ocmp_ref, owin_ref, acc_ref, m_ref, l_ref, st_ref, *, seq):
    i = pl.program_id(1)
    qb = NSA_Q_BLOCK
    rq = GQA_R * qb
    n_cmp = seq // NSA_CMP_STRIDE
    n_sel = seq // NSA_SEL_BLOCK
    blocks_per_chunk = NSA_SEL_CHUNK // NSA_SEL_BLOCK
    slabs_per_chunk = NSA_SEL_CHUNK // VT_SLAB
    t0 = i * qb
    qpos = t0 + (lax.broadcasted_iota(jnp.int32, (1, rq), 1) & (qb - 1))
    gates_t = _sigmoid(g_ref[0]).T
    q_t = q_ref[0].astype(F32).T * ATTN_SCALE

    cmp_end = lax.broadcasted_iota(jnp.int32, (n_cmp, rq), 0) * NSA_CMP_STRIDE + (NSA_CMP_LEN - 1)
    mask_c = cmp_end <= qpos
    has_cmp = jnp.where(qpos >= NSA_CMP_LEN - 1, 1.0, 0.0)
    blk = lax.broadcasted_iota(jnp.int32, (n_sel, LANES), 0)
    blk_f = blk.astype(F32)
    forced = (blk == i) | (blk == 0)
    allowed = blk <= i
    lane_lo = lax.broadcasted_iota(jnp.int32, (n_sel, LANES), 1) < qb
    importance = {}
    cw = jnp.maximum(t0 - NSA_WINDOW, 0) // VT_SLAB
    wkeys = NSA_WIN_SLABS * VT_SLAB
    wstart = pl.multiple_of(cw * VT_SLAB, VT_SLAB)
    delta_w = qpos - (wstart + lax.broadcasted_iota(jnp.int32, (wkeys, rq), 0))
    mask_w = (delta_w >= 0) & (delta_w < NSA_WINDOW)
    last_chunk = i // blocks_per_chunk
    key_in_blk = lax.broadcasted_iota(jnp.int32, (NSA_SEL_BLOCK, rq), 0)
    diag_bias = jnp.where(key_in_blk <= qpos - t0, 0.0, MASK_NEG)
    groups = range(KV_HEADS)

    def cmp_start(kv):
        qt = jnp.concatenate(
            [q_t[HEAD_DIM * (GQA_R * kv + r):HEAD_DIM * (GQA_R * kv + r + 1), :] for r in range(GQA_R)],
            axis=1).astype(BF16)
        qt_ref[kv] = qt
        qs = jnp.concatenate(
            [q_ref[0, :, HEAD_DIM * (GQA_R * kv + r):HEAD_DIM * (GQA_R * kv + r + 1)] for r in range(GQA_R)],
            axis=0) * jnp.asarray(ATTN_SCALE, BF16)
        return _dot_nt(kcmp_ref[0, kv], qs)

    def cmp_finish(kv, logits):
        lc = jnp.where(mask_c, logits, MASK_NEG)
        ec = jnp.exp(lc - jnp.max(lc, axis=0, keepdims=True))
        inv = has_cmp / jnp.sum(ec, axis=0, keepdims=True)
        ocmp_ref[kv] = _dot(vcmpt_ref[0, kv], ec.astype(BF16)) * inv
        imp = _dot_split3(ov_ref[...], ec) * inv
        imp = imp + pltpu.roll(imp, qb, 1)
        imp = imp + pltpu.roll(imp, 2 * qb, 1)
        m_ref[kv] = jnp.full((1, rq), MASK_NEG, F32)
        l_ref[kv] = jnp.zeros((1, rq), F32)
        acc_ref[kv] = jnp.zeros((HEAD_DIM, rq), F32)
        importance[kv] = imp[:, 0:LANES]
        if kv % 2 == 1:
            select_pair(kv - 1, kv)

    def select_pair(ka, kb):
        v = jnp.where(lane_lo, importance[ka], importance[kb])
        v = jnp.where(allowed & jnp.logical_not(forced), v, -jnp.inf)
        sel = forced
        for _ in range(NSA_TOP_N - 2):
            best = jnp.max(v, axis=0, keepdims=True)
            first = jnp.min(jnp.where(v == best, blk_f, float(n_sel)), axis=0, keepdims=True)
            pick = blk_f == first
            sel = sel | pick
            v = jnp.where(pick, -jnp.inf, v)
        bias = jnp.where(sel & allowed, 0.0, MASK_NEG)
        swapped = pltpu.roll(bias, qb, 1)
        for kv, own_lo in ((ka, True), (kb, False)):
            half = jnp.where(lane_lo, bias, swapped) if own_lo else jnp.where(lane_lo, swapped, bias)
            bias_ref[kv] = jnp.concatenate([half, half], axis=1)

    def cmp_phase_start(item):
        if not isinstance(item, tuple):
            return cmp_start(item)
        kind, kv = item
        if kind == "window":
            return _dot(kw_ref[0, pl.ds(wstart, wkeys), HEAD_DIM * kv:HEAD_DIM * (kv + 1)], qt_ref[kv])
        return sel_start(0, kv)

    def cmp_phase_finish(item, res):
        if not isinstance(item, tuple):
            cmp_finish(item, res)
            return
        kind, kv = item
        if kind == "first_chunk":
            st_ref[kv] = res
            return
        dsl = slice(HEAD_DIM * kv, HEAD_DIM * (kv + 1))
        lw = jnp.where(mask_w, res, MASK_NEG)
        ew = jnp.exp(lw - jnp.max(lw, axis=0, keepdims=True))
        vwt = jnp.concatenate([vwt_ref[0, cw + u, dsl, :] for u in range(NSA_WIN_SLABS)], axis=1)
        owin_ref[kv] = _dot(vwt, ew.astype(BF16)) * (1.0 / jnp.sum(ew, axis=0, keepdims=True))

    def sel_start(c, kv):
        k0 = pl.multiple_of(c * NSA_SEL_CHUNK, NSA_SEL_CHUNK)
        return _dot(ks_ref[0, pl.ds(k0, NSA_SEL_CHUNK), HEAD_DIM * kv:HEAD_DIM * (kv + 1)], qt_ref[kv])

    def sel_finish(c, kv, st, causal):
        dsl = slice(HEAD_DIM * kv, HEAD_DIM * (kv + 1))
        k0 = c * NSA_SEL_CHUNK
        bias8 = bias_ref[kv, pl.ds(pl.multiple_of(c * blocks_per_chunk, blocks_per_chunk), blocks_per_chunk), :]
        rows = []
        for jb in range(blocks_per_chunk):
            sj = st[NSA_SEL_BLOCK * jb:NSA_SEL_BLOCK * (jb + 1)] + bias8[jb:jb + 1, :]
            if causal:
                sj = sj + jnp.where(jb == i % blocks_per_chunk, diag_bias, 0.0)
            rows.append(sj)
        sc = jnp.concatenate(rows, axis=0)
        m_run = m_ref[kv]
        m_new = jnp.maximum(m_run, jnp.max(sc, axis=0, keepdims=True))
        p = jnp.exp(sc - m_new)
        alpha = jnp.exp(m_run - m_new)
        vt = jnp.concatenate([vst_ref[0, slabs_per_chunk * c + u, dsl, :] for u in range(slabs_per_chunk)],
                             axis=1)
        m_ref[kv] = m_new
        l_ref[kv] = alpha * l_ref[kv] + jnp.sum(p, axis=0, keepdims=True)
        acc_ref[kv] = alpha * acc_ref[kv] + _dot(vt, p.astype(BF16))

    ahead = NSA_AHEAD
    _one_ahead(list(groups) + [("window", kv) for kv in groups] + [("first_chunk", kv) for kv in range(ahead)],
               cmp_phase_start, cmp_phase_finish, depth=ahead)

    def sel_trip(c, carry):
        pending = [st_ref[k] for k in range(ahead)]
        for kv in groups:
            nk = kv + ahead
            if nk < KV_HEADS:
                pending.append(sel_start(c, nk))
            elif nk == KV_HEADS:
                parked = [sel_start(c + 1, k) for k in range(ahead)]
            sel_finish(c, kv, pending.pop(0), False)
            if nk == KV_HEADS:
                for k in range(ahead):
                    st_ref[k] = parked[k]
        return carry

    lax.fori_loop(0, last_chunk, sel_trip, 0)

    def tail_finish(kv, logits):
        sel_finish(last_chunk, kv, logits, True)
        o_sel = acc_ref[kv] * (1.0 / l_ref[kv])

        def gate_row(branch):
            return jnp.concatenate(
                [gates_t[3 * (GQA_R * kv + r) + branch:3 * (GQA_R * kv + r) + branch + 1, :] for r in range(GQA_R)],
                axis=1)
        o_t = gate_row(0) * ocmp_ref[kv] + gate_row(1) * o_sel + gate_row(2) * owin_ref[kv]
        o_q = o_t.T
        o_ref[0, :, KV_DIM * kv:KV_DIM * (kv + 1)] = jnp.concatenate(
            [o_q[qb * r:qb * (r + 1)] for r in range(GQA_R)], axis=1).astype(o_ref.dtype)

    _one_ahead(groups, functools.partial(sel_start, last_chunk), tail_finish,
               depth=ahead, primed=[st_ref[k] for k in range(ahead)])


def nsa_core(pa, vt, pb, k_cmp, v_cmp_t, b, s):
    qb = NSA_Q_BLOCK
    qd = N_HEADS * HEAD_DIM
    n_cmp = s // NSA_CMP_STRIDE
    n_sel = s // NSA_SEL_BLOCK
    n_slab = s // VT_SLAB
    pa3 = pa.reshape(b, s, -1)
    pb3 = pb.reshape(b, s, -1)
    vt4 = vt.reshape(b, n_slab, 2 * KV_DIM, VT_SLAB)
    sel_lo = np.arange(n_sel)[:, None] * NSA_SEL_BLOCK
    cmp_lo = np.arange(n_cmp)[None, :] * NSA_CMP_STRIDE
    ov = np.clip(np.minimum(sel_lo + NSA_SEL_BLOCK, cmp_lo + NSA_CMP_LEN) - np.maximum(sel_lo, cmp_lo), 0, None)
    ov = jnp.asarray((ov / NSA_CMP_LEN).astype(np.float32), BF16)
    kcol = qd // KV_DIM
    seqspec = lambda col: pl.BlockSpec((1, s, KV_DIM), lambda i, j: (i, 0, col))
    slabspec = lambda k: pl.BlockSpec((1, n_slab, KV_DIM, VT_SLAB), lambda i, j: (i, 0, k, 0))
    rq = GQA_R * qb
    out = pl.pallas_call(
        functools.partial(_nsa_kernel, seq=s),
        grid=(b, s // qb),
        in_specs=[pl.BlockSpec((1, qb, qd), lambda i, j: (i, j, 0)),
                  seqspec(kcol), slabspec(0), seqspec(kcol + 1), slabspec(1),
                  pl.BlockSpec((1, KV_HEADS, n_cmp, HEAD_DIM), lambda i, j: (i, 0, 0, 0)),
                  pl.BlockSpec((1, KV_HEADS, HEAD_DIM, n_cmp), lambda i, j: (i, 0, 0, 0)),
                  pl.BlockSpec((1, qb, KV_DIM), lambda i, j: (i, j, 2)),
                  pl.BlockSpec((n_sel, n_cmp), lambda i, j: (0, 0))],
        out_specs=pl.BlockSpec((1, qb, qd), lambda i, j: (i, j, 0)),
        out_shape=jax.ShapeDtypeStruct((b, s, qd), BF16),
        scratch_shapes=[pltpu.VMEM((KV_HEADS, HEAD_DIM, rq), BF16),
                        pltpu.VMEM((KV_HEADS, n_sel, rq), F32),
                        pltpu.VMEM((KV_HEADS, HEAD_DIM, rq), F32),
                        pltpu.VMEM((KV_HEADS, HEAD_DIM, rq), F32),
                        pltpu.VMEM((KV_HEADS, HEAD_DIM, rq), F32),
                        pltpu.VMEM((KV_HEADS, 1, rq), F32),
                        pltpu.VMEM((KV_HEADS, 1, rq), F32),
                        pltpu.VMEM((NSA_AHEAD, NSA_SEL_CHUNK, rq), F32)],
        compiler_params=_cparams("parallel", "arbitrary"),
        name="nsa_core",
    )(pa3, pa3, vt4, pa3, vt4, k_cmp, v_cmp_t, pb3, ov)
    return out.reshape(b * s, qd)


def _rope_tables(positions):
    half = ROPE_DIM // 2
    inv_freq = ROPE_THETA ** (-jnp.arange(0, ROPE_DIM, 2, dtype=F32) / ROPE_DIM)
    ang = positions.astype(F32).reshape(-1)[:, None] * inv_freq
    cos, sin = jnp.cos(ang), jnp.sin(ang)
    t = cos.shape[0]
    ones = jnp.ones((t, HEAD_DIM - ROPE_DIM), F32)
    zeros = jnp.zeros((t, HEAD_DIM - ROPE_DIM), F32)
    zh = jnp.zeros((t, half), F32)
    c = jnp.concatenate([cos, cos, ones], axis=1)
    sa = jnp.concatenate([-sin, zh, zeros], axis=1)
    sb = jnp.concatenate([zh, sin, zeros], axis=1)
    return tuple(jnp.tile(v, (1, LANES // HEAD_DIM)) for v in (c, sa, sb))


def _pad_cols(w, n):
    return jnp.pad(w, ((0, 0), (0, n - w.shape[1])))


def mamba2_mixer(h, ln_w, w_in, conv_w, conv_b, dt_bias, a_log, d_skip, norm_w, w_out, b, s):
    w = _pad_cols(w_in, SSM_IN_PAD).astype(BF16)
    zx = norm_proj(h, ln_w, w, jnp.zeros((SSM_IN_PAD,), F32), F32, tm=SSM_IN_TOKEN_TILE, row_splits=2)
    y = ssd_core(zx, conv_w, conv_b, dt_bias, a_log, d_skip, norm_w, b, s)
    return out_proj(y, w_out.astype(BF16), jnp.zeros((D_MODEL,), F32), h)


def swa_mixer(h, ln_w, w_qkv, b_qkv, sinks, w_o, b_o, rope, b, s):
    qk = (N_HEADS + KV_HEADS) * HEAD_DIM
    w = w_qkv.astype(BF16)
    bias = b_qkv.astype(F32)
    q_k, vt = norm_proj(h, ln_w, w[:, :qk], bias[:qk], BF16, rope=rope, rope_cols=((0, qk),),
                        w_t=w[:, qk:].T, bias_t=bias[qk:])
    o = swa_core(q_k, vt, sinks, b, s)
    return out_proj(o, w_o.astype(BF16), b_o.astype(F32), h)


def nsa_mixer(h, ln_w, w_in, pe_k, k_w1, k_w2, pe_v, v_w1, v_w2, w_o, rope, b, s):
    qd = N_HEADS * HEAD_DIM
    cols = lambda k: w_in[:, qd + KV_DIM * k:qd + KV_DIM * (k + 1)]
    wa = jnp.concatenate([w_in[:, :qd], cols(2), cols(4)], axis=1).astype(BF16)
    wv_t = jnp.concatenate([cols(3), cols(5)], axis=1).astype(BF16).T
    wb = _pad_cols(jnp.concatenate([cols(0), cols(1), w_in[:, qd + 6 * KV_DIM:]], axis=1), 3 * KV_DIM).astype(BF16)
    pa, vt = norm_proj(h, ln_w, wa, jnp.zeros((wa.shape[1],), F32), BF16, rope=rope,
                       rope_cols=((0, wa.shape[1]),), w_t=wv_t, bias_t=jnp.zeros((wv_t.shape[0],), F32))
    pb = norm_proj(h, ln_w, wb, jnp.zeros((wb.shape[1],), F32), F32, rope=rope, rope_cols=((0, KV_DIM),))
    pb3 = pb.reshape(b, s, -1)
    k_cmp = compress(pb3[..., 0:KV_DIM], pe_k, k_w1, k_w2, b, s, False)
    v_cmp_t = compress(pb3[..., KV_DIM:2 * KV_DIM], pe_v, v_w1, v_w2, b, s, True)
    o = nsa_core(pa, vt, pb, k_cmp, v_cmp_t, b, s)
    return out_proj(o, w_o.astype(BF16), jnp.zeros((D_MODEL,), F32), h)


def kernel(x, positions, ln_ffn1, ffn1_w_in, ffn1_w_out, ln_mix, ln_ffn2, ffn2_w_in, ffn2_w_out, ssm_w_in, ssm_conv_w, ssm_conv_b, ssm_dt_bias, ssm_a_log, ssm_d, ssm_norm_w, ssm_w_out, swa_w_qkv, swa_b_qkv, swa_sinks, swa_w_o, swa_b_o, nsa_w_in, nsa_pe_k, nsa_k_w1, nsa_k_w2, nsa_pe_v, nsa_v_w1, nsa_v_w2, nsa_w_o, final_norm):
    b, s, d = x.shape
    depth = ln_ffn1.shape[0]
    rope = _rope_tables(positions)
    h = x.reshape(b * s, d)
    for i in range(depth):
        kind, inst = i % 3, i // 3
        h = ffn(h, ln_ffn1[i], ffn1_w_in[i], ffn1_w_out[i])
        if kind == 0:
            h = mamba2_mixer(h, ln_mix[i], ssm_w_in[inst], ssm_conv_w[inst], ssm_conv_b[inst],
                             ssm_dt_bias[inst], ssm_a_log[inst], ssm_d[inst], ssm_norm_w[inst],
                             ssm_w_out[inst], b, s)
        elif kind == 1:
            h = swa_mixer(h, ln_mix[i], swa_w_qkv[inst], swa_b_qkv[inst], swa_sinks[inst],
                          swa_w_o[inst], swa_b_o[inst], rope, b, s)
        else:
            h = nsa_mixer(h, ln_mix[i], nsa_w_in[inst], nsa_pe_k[inst], nsa_k_w1[inst], nsa_k_w2[inst],
                          nsa_pe_v[inst], nsa_v_w1[inst], nsa_v_w2[inst], nsa_w_o[inst], rope, b, s)
        h = ffn(h, ln_ffn2[i], ffn2_w_in[i], ffn2_w_out[i], final_norm if i == depth - 1 else None)
    return h.reshape(b, s, d)
```
